```python
import math
import jax, jax.numpy as jnp
from jax import lax
import numpy as np

D_MODEL = 1024
BATCH = 16
SEQ = 2048
DEPTH = 1

MEM_LEN = 256
CHUNK = 64
ML_HEADS = 4
ML_DV = D_MODEL // ML_HEADS
ML_DQK = ML_DV // 2
ML_QK_W = ML_HEADS * ML_DQK
ML_V_W = ML_HEADS * ML_DV
GDN_DK = 128
GDN_DV = 128
GDN_QK_HEADS = D_MODEL // GDN_DK
GDN_V_HEADS = 2 * GDN_QK_HEADS
GDN_QK_W = GDN_QK_HEADS * GDN_DK
GDN_V_W = GDN_V_HEADS * GDN_DV
GDN_CONV_CH = 2 * GDN_QK_W + GDN_V_W
CONV_K = 4
XA_HEADS = 4
XA_DH = D_MODEL // XA_HEADS
N_EXPERTS = 32
TOP_K = 4
D_EXPERT = D_MODEL
SWIGLU_LIMIT = 7.0
SWIGLU_ALPHA = 1.702
MOE_BLOCK = 256
DN_ALPHA = (2 * DEPTH) ** 0.25
DN_BETA = (8 * DEPTH) ** -0.25
LN_EPS = 1e-5
RMS_EPS = 1e-6
IN_SPLITS = (ML_QK_W, ML_QK_W, ML_V_W, ML_V_W, 2 * ML_HEADS,
             GDN_CONV_CH, GDN_V_W, GDN_V_HEADS, GDN_V_HEADS, D_MODEL, D_MODEL)
IN_W = sum(IN_SPLITS)

kernel_name = 'hybrid_mlstm_gdn_xattn_moe_deepnorm'


def _split(t, sizes):
    idx = np.cumsum(sizes)[:-1].tolist()
    return jnp.split(t, idx, axis=-1)


def layer_norm(x, g, b):
    xf = x.astype(jnp.float32)
    mu = jnp.mean(xf, axis=-1, keepdims=True)
    var = jnp.mean(jnp.square(xf - mu), axis=-1, keepdims=True)
    return ((xf - mu) * lax.rsqrt(var + LN_EPS) * g + b).astype(x.dtype)


def rms_norm(x, g):
    xf = x.astype(jnp.float32)
    return xf * lax.rsqrt(jnp.mean(jnp.square(xf), axis=-1, keepdims=True) + RMS_EPS) * g


def l2_normalize(x):
    return x * lax.rsqrt(jnp.sum(jnp.square(x), axis=-1, keepdims=True) + RMS_EPS)


def _to_chunks(t):
    b, s, h = t.shape[:3]
    t = t.reshape((b, s // CHUNK, CHUNK, h) + t.shape[3:])
    return jnp.moveaxis(t, 3, 1)


def _from_chunks(t):
    b, h, n, l, d = t.shape
    return jnp.moveaxis(t, 1, 3).reshape(b, n * l, h, d)


def causal_depthwise_conv(x, w):
    c = x.shape[-1]
    return lax.conv_general_dilated(x, w[:, None, :].astype(x.dtype), window_strides=(1,),
                                    padding=[(CONV_K - 1, 0)],
                                    dimension_numbers=('NWC', 'WIO', 'NWC'),
                                    feature_group_count=c)


def mlstm_chunkwise(q, k, v, li, lf):
    q, k, v, li, lf = (_to_chunks(t) for t in (q, k, v, li, lf))
    bsz, nh = q.shape[:2]
    dk, dv = q.shape[-1], v.shape[-1]
    causal = jnp.tril(jnp.ones((CHUNK, CHUNK), dtype=bool))
    b = jnp.cumsum(lf, axis=-1)
    g = b[..., -1]
    dmat = jnp.where(causal, b[..., :, None] - b[..., None, :] + li[..., None, :], -jnp.inf)
    m_intra = jnp.max(dmat, axis=-1)
    p = jnp.exp(dmat - m_intra[..., None]) * jnp.einsum('bhnid,bhnjd->bhnij', q, k)
    intra_num = jnp.einsum('bhnij,bhnjv->bhniv', p, v)
    intra_den = jnp.sum(p, axis=-1)
    a = g[..., None] - b + li
    m_chunk = jnp.max(a, axis=-1)
    kw = k * jnp.exp(a - m_chunk[..., None])[..., None]
    xs = tuple(jnp.moveaxis(t, 2, 0) for t in
               (q, kw, v, b, g, m_chunk, m_intra, intra_num, intra_den))
    init = (jnp.zeros((bsz, nh, dk, dv), jnp.float32),
            jnp.zeros((bsz, nh, dk), jnp.float32),
            jnp.zeros((bsz, nh), jnp.float32))

    def step(carry, inp):
        c_st, n_st, m_st = carry
        q_c, kw_c, v_c, b_c, g_c, mc_c, mi_c, num_c, den_c = inp
        inter_log = b_c + m_st[..., None]
        m_out = jnp.maximum(inter_log, mi_c)
        s_inter = jnp.exp(inter_log - m_out)
        s_intra = jnp.exp(mi_c - m_out)
        num = (s_inter[..., None] * jnp.einsum('bhld,bhdv->bhlv', q_c, c_st)
               + s_intra[..., None] * num_c)
        den = s_inter * jnp.einsum('bhld,bhd->bhl', q_c, n_st) + s_intra * den_c
        h = num / jnp.maximum(jnp.abs(den), jnp.exp(-m_out))[..., None]
        m_new = jnp.maximum(g_c + m_st, mc_c)
        dec = jnp.exp(g_c + m_st - m_new)
        s_new = jnp.exp(mc_c - m_new)
        c_st = dec[..., None, None] * c_st + s_new[..., None, None] * jnp.einsum('bhld,bhlv->bhdv', kw_c, v_c)
        n_st = dec[..., None] * n_st + s_new[..., None] * jnp.sum(kw_c, axis=-2)
        return (c_st, n_st, m_new), h

    _, h = lax.scan(step, init, xs)
    return _from_chunks(jnp.moveaxis(h, 0, 2))


def gated_delta_chunkwise(q, k, v, gdec, beta):
    q, k, v, gdec, beta = (_to_chunks(t) for t in (q, k, v, gdec, beta))
    bsz, nh = q.shape[:2]
    dk, dv = q.shape[-1], v.shape[-1]
    incl = jnp.tril(jnp.ones((CHUNK, CHUNK), dtype=bool))
    strict = jnp.tril(jnp.ones((CHUNK, CHUNK), dtype=bool), k=-1)
    gam = jnp.cumsum(gdec, axis=-1)
    dec = jnp.exp(jnp.where(incl, gam[..., :, None] - gam[..., None, :], -jnp.inf))
    kk = jnp.einsum('bhnid,bhnjd->bhnij', k, k)
    t_mat = jnp.where(strict, beta[..., :, None] * kk * dec, 0.0) + jnp.eye(CHUNK, dtype=jnp.float32)
    rhs = jnp.concatenate([v * beta[..., None], k * (beta * jnp.exp(gam))[..., None]], axis=-1)
    sol = lax.linalg.triangular_solve(t_mat, rhs, left_side=True, lower=True, unit_diagonal=True)
    u, w = sol[..., :dv], sol[..., dv:]
    attn = jnp.einsum('bhnid,bhnjd->bhnij', q, k) * dec
    qd = q * jnp.exp(gam)[..., None]
    g_tot = gam[..., -1]
    kd = k * jnp.exp(g_tot[..., None] - gam)[..., None]
    xs = tuple(jnp.moveaxis(t, 2, 0) for t in (qd, w, u, attn, kd, g_tot))

    def step(s_st, inp):
        qd_c, w_c, u_c, at_c, kd_c, gt_c = inp
        v_new = u_c - jnp.einsum('bhld,bhdv->bhlv', w_c, s_st)
        o = jnp.einsum('bhld,bhdv->bhlv', qd_c, s_st) + jnp.einsum('bhij,bhjv->bhiv', at_c, v_new)
        s_st = jnp.exp(gt_c)[..., None, None] * s_st + jnp.einsum('bhld,bhlv->bhdv', kd_c, v_new)
        return s_st, o

    _, o = lax.scan(step, jnp.zeros((bsz, nh, dk, dv), jnp.float32), xs)
    return _from_chunks(jnp.moveaxis(o, 0, 2))


def hybrid_mixer(xn, w_in, ml_gate_bias, ml_norm_g, conv_w, a_log, dt_bias, gdn_norm_g,
                 w_br_ml, w_br_gdn, w_out):
    f32 = jnp.float32
    bsz, s, _ = xn.shape
    proj = xn @ w_in
    mq, mk, mv, mo, mif, gqkv, gz, ga, gb, gate_ml, gate_gdn = _split(proj, IN_SPLITS)
    q = mq.reshape(bsz, s, ML_HEADS, ML_DQK).astype(f32)
    k = mk.reshape(bsz, s, ML_HEADS, ML_DQK).astype(f32) * (ML_DQK ** -0.5)
    v = mv.reshape(bsz, s, ML_HEADS, ML_DV).astype(f32)
    pre = mif.astype(f32) + ml_gate_bias.astype(f32)
    li = pre[..., :ML_HEADS]
    lf = jax.nn.log_sigmoid(pre[..., ML_HEADS:])
    hm = mlstm_chunkwise(q, k, v, li, lf)
    hm = rms_norm(hm, ml_norm_g.reshape(ML_HEADS, ML_DV)) * \
        jax.nn.sigmoid(mo.astype(f32)).reshape(bsz, s, ML_HEADS, ML_DV)
    y_ml = hm.reshape(bsz, s, ML_V_W).astype(xn.dtype) @ w_br_ml
    c = jax.nn.silu(causal_depthwise_conv(gqkv, conv_w)).astype(f32)
    cq, ck, cv = _split(c, (GDN_QK_W, GDN_QK_W, GDN_V_W))
    rep = GDN_V_HEADS // GDN_QK_HEADS
    q = jnp.repeat(l2_normalize(cq.reshape(bsz, s, GDN_QK_HEADS, GDN_DK)), rep, axis=2) * (GDN_DK ** -0.5)
    k = jnp.repeat(l2_normalize(ck.reshape(bsz, s, GDN_QK_HEADS, GDN_DK)), rep, axis=2)
    v = cv.reshape(bsz, s, GDN_V_HEADS, GDN_DV)
    beta = jax.nn.sigmoid(gb.astype(f32))
    gdec = -jnp.exp(a_log.astype(f32)) * jax.nn.softplus(ga.astype(f32) + dt_bias.astype(f32))
    o = gated_delta_chunkwise(q, k, v, gdec, beta)
    o = rms_norm(o, gdn_norm_g) * jax.nn.silu(gz.astype(f32)).reshape(bsz, s, GDN_V_HEADS, GDN_DV)
    y_gdn = o.reshape(bsz, s, GDN_V_W).astype(xn.dtype) @ w_br_gdn
    merged = jax.nn.sigmoid(gate_ml) * y_ml + jax.nn.sigmoid(gate_gdn) * y_gdn
    return merged @ w_out


def memory_cross_attention(xn, mem, wq, wk, wv, wo):
    bsz, s, _ = xn.shape
    m = mem.shape[1]
    q = (xn @ wq).reshape(bsz, s, XA_HEADS, XA_DH)
    k = (mem @ wk).reshape(bsz, m, XA_HEADS, XA_DH)
    v = (mem @ wv).reshape(bsz, m, XA_HEADS, XA_DH)
    sc = jnp.einsum('bshd,bmhd->bhsm', q, k).astype(jnp.float32) * (XA_DH ** -0.5)
    p = jax.nn.softmax(sc, axis=-1).astype(v.dtype)
    o = jnp.einsum('bhsm,bmhd->bshd', p, v).reshape(bsz, s, D_MODEL)
    return o @ wo


def clamped_swiglu(gate, up):
    gate = jnp.minimum(gate, SWIGLU_LIMIT)
    up = jnp.clip(up, -SWIGLU_LIMIT, SWIGLU_LIMIT)
    return (up + 1.0) * (gate * jax.nn.sigmoid(SWIGLU_ALPHA * gate))


def routed_moe(h, w_router, b_router, w_gu, b_gu, w_dn, b_dn):
    bsz, s, d = h.shape
    n_tok = bsz * s
    n_asg = n_tok * TOP_K
    x2 = h.reshape(n_tok, d)
    logits = (x2 @ w_router).astype(jnp.float32) + b_router.astype(jnp.float32)
    top_v, top_e = lax.top_k(logits, TOP_K)
    gate_w = jax.nn.softmax(top_v, axis=-1)
    flat_e = top_e.reshape(n_asg)
    order = jnp.argsort(flat_e)
    sorted_e = flat_e[order]
    counts = jnp.bincount(flat_e, length=N_EXPERTS)
    padded = (counts + MOE_BLOCK - 1) // MOE_BLOCK * MOE_BLOCK
    grp_start = jnp.cumsum(counts) - counts
    pad_end = jnp.cumsum(padded)
    pad_start = pad_end - padded
    dest = pad_start[sorted_e] + (jnp.arange(n_asg) - grp_start[sorted_e])
    n_blocks = -(-n_asg // MOE_BLOCK) + N_EXPERTS
    n_rows = n_blocks * MOE_BLOCK
    row_tok = jnp.zeros((n_rows,), jnp.int32).at[dest].set((order // TOP_K).astype(jnp.int32))
    row_w = jnp.zeros((n_rows,), jnp.float32).at[dest].set(gate_w.reshape(n_asg)[order])
    block_e = jnp.minimum(jnp.searchsorted(pad_end, jnp.arange(n_blocks) * MOE_BLOCK, side='right'),
                          N_EXPERTS - 1)

    def expert_block(args):
        tok, wt, e = args
        xb = x2[tok]
        gu = xb @ w_gu[e] + b_gu[e]
        y = clamped_swiglu(gu[:, :D_EXPERT], gu[:, D_EXPERT:]) @ w_dn[e] + b_dn[e]
        return y.astype(jnp.float32) * wt[:, None]

    yb = lax.map(expert_block, (row_tok.reshape(n_blocks, MOE_BLOCK),
                                row_w.reshape(n_blocks, MOE_BLOCK), block_e))
    out = jnp.zeros((n_tok, d), jnp.float32).at[row_tok].add(yb.reshape(n_rows, d))
    return out.astype(h.dtype).reshape(bsz, s, d)


def setup_inputs(seed: int = 0) -> dict:
    key = jax.random.key(seed)
    ks = jax.random.split(key, 40)
    f32 = jnp.float32
    L = DEPTH

    def nrm(i, shape, scale):
        return jax.random.normal(ks[i], shape, f32) * scale

    dt = jnp.exp(jax.random.uniform(ks[9], (L, GDN_V_HEADS), f32, math.log(1e-3), math.log(1e-1)))
    return {
        'x': nrm(0, (BATCH, SEQ, D_MODEL), 1.0),
        'mem': nrm(1, (BATCH, MEM_LEN, D_MODEL), 1.0),
        'ln_in_g': 1.0 + nrm(2, (D_MODEL,), 0.02),
        'ln_in_b': nrm(3, (D_MODEL,), 0.02),
        'w_in': nrm(4, (L, D_MODEL, IN_W), D_MODEL ** -0.5),
        'ml_gate_bias': jnp.concatenate(
            [-2.0 + nrm(5, (L, ML_HEADS), 0.1),
             jnp.linspace(3.0, 6.0, ML_HEADS, dtype=f32)[None, :] + nrm(6, (L, ML_HEADS), 0.1)], axis=-1),
        'ml_norm_g': 1.0 + nrm(7, (L, ML_V_W), 0.02),
        'gdn_conv_w': nrm(8, (L, CONV_K, GDN_CONV_CH), CONV_K ** -0.5),
        'gdn_a_log': jnp.log(jax.random.uniform(ks[10], (L, GDN_V_HEADS), f32, 1.0, 16.0)),
        'gdn_dt_bias': dt + jnp.log(-jnp.expm1(-dt)),
        'gdn_norm_g': 1.0 + nrm(11, (L, GDN_DV), 0.02),
        'w_branch_ml': nrm(12, (L, ML_V_W, D_MODEL), ML_V_W ** -0.5),
        'w_branch_gdn': nrm(13, (L, GDN_V_W, D_MODEL), GDN_V_W ** -0.5),
        'w_mix_out': nrm(14, (L, D_MODEL, D_MODEL), DN_BETA * D_MODEL ** -0.5),
        'ln1_g': 1.0 + nrm(15, (L, D_MODEL), 0.02),
        'ln1_b': nrm(16, (L, D_MODEL), 0.02),
        'xa_wq': nrm(17, (L, D_MODEL, D_MODEL), D_MODEL ** -0.5),
        'xa_wk': nrm(18, (L, D_MODEL, D_MODEL), D_MODEL ** -0.5),
        'xa_wv': nrm(19, (L, D_MODEL, D_MODEL), D_MODEL ** -0.5),
        'xa_wo': nrm(20, (L, D_MODEL, D_MODEL), DN_BETA * D_MODEL ** -0.5),
        'ln2_g': 1.0 + nrm(21, (L, D_MODEL), 0.02),
        'ln2_b': nrm(22, (L, D_MODEL), 0.02),
        'w_router': nrm(23, (L, D_MODEL, N_EXPERTS), D_MODEL ** -0.5),
        'b_router': nrm(24, (L, N_EXPERTS), 0.01),
        'w_gu': nrm(25, (L, N_EXPERTS, D_MODEL, 2 * D_EXPERT), D_MODEL ** -0.5),
        'b_gu': nrm(26, (L, N_EXPERTS, 2 * D_EXPERT), 0.01),
        'w_dn': nrm(27, (L, N_EXPERTS, D_EXPERT, D_MODEL), DN_BETA * D_EXPERT ** -0.5),
        'b_dn': nrm(28, (L, N_EXPERTS, D_MODEL), 0.01),
        'ln3_g': 1.0 + nrm(29, (L, D_MODEL), 0.02),
        'ln3_b': nrm(30, (L, D_MODEL), 0.02),
    }


def reference(x, mem, ln_in_g, ln_in_b, w_in, ml_gate_bias, ml_norm_g, gdn_conv_w, gdn_a_log,
              gdn_dt_bias, gdn_norm_g, w_branch_ml, w_branch_gdn, w_mix_out, ln1_g, ln1_b,
              xa_wq, xa_wk, xa_wv, xa_wo, ln2_g, ln2_b, w_router, b_router, w_gu, b_gu,
              w_dn, b_dn, ln3_g, ln3_b):
    h = layer_norm(x, ln_in_g, ln_in_b)
    for l in range(DEPTH):
        mix = hybrid_mixer(h, w_in[l], ml_gate_bias[l], ml_norm_g[l], gdn_conv_w[l], gdn_a_log[l],
                           gdn_dt_bias[l], gdn_norm_g[l], w_branch_ml[l], w_branch_gdn[l], w_mix_out[l])
        h = layer_norm(DN_ALPHA * h + mix, ln1_g[l], ln1_b[l])
        xa = memory_cross_attention(h, mem, xa_wq[l], xa_wk[l], xa_wv[l], xa_wo[l])
        h = layer_norm(DN_ALPHA * h + xa, ln2_g[l], ln2_b[l])
        ff = routed_moe(h, w_router[l], b_router[l], w_gu[l], b_gu[l], w_dn[l], b_dn[l])
        h = layer_norm(DN_ALPHA * h + ff, ln3_g[l], ln3_b[l])
    return h
```

```python
import functools

import jax
import jax.numpy as jnp
from jax import lax
from jax.experimental import pallas as pl
from jax.experimental.pallas import tpu as pltpu

F32 = jnp.float32
BF16 = jnp.bfloat16

CHUNK = 64
ML_HEADS = 4
GDN_DK = 128
GDN_DV = 128
CONV_K = 4
XA_HEADS = 4
N_EXPERTS = 32
TOP_K = 4
SWIGLU_LIMIT = 7.0
SWIGLU_ALPHA = 1.702
MOE_BLOCK = 256
DEPTH = 1
DN_ALPHA = (2 * DEPTH) ** 0.25
LN_EPS = 1e-5
RMS_EPS = 1e-6
LANES = 128
VMEM_LIMIT = 56 * 1024 * 1024
NEG_BIG = -1e30


def _params(*sem):
    return pltpu.CompilerParams(dimension_semantics=sem, vmem_limit_bytes=VMEM_LIMIT)


def _dot(a, b):
    return jnp.dot(a, b, preferred_element_type=F32)


def _dot_nt(a, b):
    return lax.dot_general(a, b, (((1,), (1,)), ((), ())), preferred_element_type=F32)


def _layer_norm(x, g, b):
    mu = jnp.mean(x, axis=-1, keepdims=True)
    xc = x - mu
    var = jnp.mean(xc * xc, axis=-1, keepdims=True)
    return xc * lax.rsqrt(var + LN_EPS) * g + b


def _sigmoid(x):
    return 1.0 / (1.0 + jnp.exp(-x))


def _log_sigmoid(x):
    return jnp.minimum(x, 0.0) - jnp.log(1.0 + jnp.exp(-jnp.abs(x)))


def _softplus(x):
    return jnp.maximum(x, 0.0) + jnp.log(1.0 + jnp.exp(-jnp.abs(x)))


def _ln_proj_kernel(x_ref, g_ref, b_ref, w_ref, ws_ref, h_ref, p_ref, s_ref, xn_ref):
    @pl.when(pl.program_id(1) == 0)
    def _():
        h = _layer_norm(x_ref[...], g_ref[...], b_ref[...])
        h_ref[...] = h
        hb = h.astype(BF16)
        xn_ref[...] = hb
        s_ref[...] = _dot(hb, ws_ref[...])

    p_ref[...] = _dot(xn_ref[...], w_ref[...]).astype(BF16)


def _ln_proj(x2, g, b, w_big, w_small, tm, tn):
    t, d = x2.shape
    nw = w_big.shape[1]
    return pl.pallas_call(
        _ln_proj_kernel,
        grid=(t // tm, nw // tn),
        in_specs=[
            pl.BlockSpec((tm, d), lambda i, j: (i, 0)),
            pl.BlockSpec((1, d), lambda i, j: (0, 0)),
            pl.BlockSpec((1, d), lambda i, j: (0, 0)),
            pl.BlockSpec((d, tn), lambda i, j: (0, j)),
            pl.BlockSpec((d, LANES), lambda i, j: (0, 0)),
        ],
        out_specs=[
            pl.BlockSpec((tm, d), lambda i, j: (i, 0)),
            pl.BlockSpec((tm, tn), lambda i, j: (i, j)),
            pl.BlockSpec((tm, LANES), lambda i, j: (i, 0)),
        ],
        out_shape=[
            jax.ShapeDtypeStruct((t, d), F32),
            jax.ShapeDtypeStruct((t, nw), BF16),
            jax.ShapeDtypeStruct((t, LANES), F32),
        ],
        scratch_shapes=[pltpu.VMEM((tm, d), BF16)],
        compiler_params=_params("arbitrary", "arbitrary"),
        name="ln_in_proj",
    )(x2, g, b, w_big, w_small)


def _mlstm_kernel(q_ref, k_ref, v_ref, o_ref, gc_ref, gr_ref, bc_ref, br_ref, ng_ref,
                  out_ref, c_ref, n_ref, m_ref, *, dqk, dv):
    L = CHUNK

    @pl.when(pl.program_id(1) == 0)
    def _():
        c_ref[...] = jnp.zeros_like(c_ref)
        n_ref[...] = jnp.zeros_like(n_ref)
        m_ref[...] = jnp.zeros_like(m_ref)

    ii = lax.broadcasted_iota(jnp.int32, (L, L), 0)
    jj = lax.broadcasted_iota(jnp.int32, (L, L), 1)
    causal = jj <= ii
    gcol = gc_ref[...]
    grow = gr_ref[0]
    scale = dqk ** -0.5
    for h in range(ML_HEADS):
        li_c = gcol[:, h:h + 1] + bc_ref[:, h:h + 1]
        lf_c = _log_sigmoid(gcol[:, ML_HEADS + h:ML_HEADS + h + 1] + bc_ref[:, ML_HEADS + h:ML_HEADS + h + 1])
        li_r = grow[h:h + 1, :] + br_ref[h:h + 1, :]
        lf_r = _log_sigmoid(grow[ML_HEADS + h:ML_HEADS + h + 1, :] + br_ref[ML_HEADS + h:ML_HEADS + h + 1, :])
        b_c = jnp.sum(jnp.where(causal, lf_r, 0.0), axis=1, keepdims=True)
        b_r = jnp.sum(jnp.where(ii <= jj, lf_c, 0.0), axis=0, keepdims=True)
        g = jnp.sum(lf_r, axis=1, keepdims=True)
        dmat = jnp.where(causal, b_c - b_r + li_r, -jnp.inf)
        m_intra = jnp.max(dmat, axis=1, keepdims=True)
        qh = q_ref[:, h * dqk:(h + 1) * dqk]
        kh = k_ref[:, h * dqk:(h + 1) * dqk]
        vh = v_ref[:, h * dv:(h + 1) * dv]
        kf = kh.astype(F32) * scale
        p = jnp.exp(dmat - m_intra) * _dot_nt(qh, kf.astype(BF16))
        intra_num = _dot(p.astype(BF16), vh)
        intra_den = jnp.sum(p, axis=1, keepdims=True)
        a_c = g - b_c + li_c
        m_chunk = jnp.max(a_c, axis=0, keepdims=True)
        kw = kf * jnp.exp(a_c - m_chunk)

        c_st = c_ref[h]
        n_st = n_ref[h:h + 1, :]
        m_st = m_ref[h:h + 1, 0:1]
        inter_log = b_c + m_st
        m_out = jnp.maximum(inter_log, m_intra)
        s_inter = jnp.exp(inter_log - m_out)
        s_intra = jnp.exp(m_intra - m_out)
        num = s_inter * _dot(qh, c_st.astype(BF16)) + s_intra * intra_num
        qn = jnp.sum(qh.astype(F32) * n_st, axis=1, keepdims=True)
        den = s_inter * qn + s_intra * intra_den
        hh = num / jnp.maximum(jnp.abs(den), jnp.exp(-m_out))
        m_new = jnp.maximum(g + m_st, m_chunk)
        dec = jnp.exp(g + m_st - m_new)
        s_new = jnp.exp(m_chunk - m_new)
        c_ref[h] = dec * c_st + s_new * _dot(kw.T.astype(BF16), vh)
        n_ref[h:h + 1, :] = dec * n_st + s_new * jnp.sum(kw, axis=0, keepdims=True)
        m_ref[h:h + 1, :] = jnp.broadcast_to(m_new, (1, LANES))

        rms = lax.rsqrt(jnp.mean(hh * hh, axis=1, keepdims=True) + RMS_EPS)
        og = _sigmoid(o_ref[:, h * dv:(h + 1) * dv].astype(F32))
        out_ref[:, h * dv:(h + 1) * dv] = (hh * rms * ng_ref[:, h * dv:(h + 1) * dv] * og).astype(BF16)


def _mlstm(proj, small, small_t, bias_c, bias_r, norm_g, bsz, seq, cols, dqk, dv):
    n = seq // CHUNK
    L = CHUNK
    qk_w, v_w = ML_HEADS * dqk, ML_HEADS * dv
    t = bsz * seq
    kern = functools.partial(_mlstm_kernel, dqk=dqk, dv=dv)
    row = lambda b, c: b * n + c
    return pl.pallas_call(
        kern,
        grid=(bsz, n),
        in_specs=[
            pl.BlockSpec((L, qk_w), lambda b, c: (row(b, c), cols["mq"] // qk_w)),
            pl.BlockSpec((L, qk_w), lambda b, c: (row(b, c), cols["mk"] // qk_w)),
            pl.BlockSpec((L, v_w), lambda b, c: (row(b, c), cols["mv"] // v_w)),
            pl.BlockSpec((L, v_w), lambda b, c: (row(b, c), cols["mo"] // v_w)),
            pl.BlockSpec((L, LANES), lambda b, c: (row(b, c), 0)),
            pl.BlockSpec((1, small_t.shape[1], L), lambda b, c: (row(b, c), 0, 0)),
            pl.BlockSpec(bias_c.shape, lambda b, c: (0, 0)),
            pl.BlockSpec(bias_r.shape, lambda b, c: (0, 0)),
            pl.BlockSpec((1, v_w), lambda b, c: (0, 0)),
        ],
        out_specs=pl.BlockSpec((L, v_w), lambda b, c: (row(b, c), 0)),
        out_shape=jax.ShapeDtypeStruct((t, v_w), BF16),
        scratch_shapes=[
            pltpu.VMEM((ML_HEADS, dqk, dv), F32),
            pltpu.VMEM((8, dqk), F32),
            pltpu.VMEM((8, LANES), F32),
        ],
        compiler_params=_params("arbitrary", "arbitrary"),
        name="mlstm",
    )(proj, proj, proj, proj, small, small_t, bias_c, bias_r, norm_g)


def _gdn_kernel(x_ref, z_ref, gc_ref, gr_ref, cw_ref, al_c_ref, al_r_ref, dt_c_ref, dt_r_ref, ng_ref,
                out_ref, s_ref, xs_ref, *, n_qk, n_v, ga_off, gb_off):
    L = CHUNK
    dk, dv = GDN_DK, GDN_DV
    qk_w = n_qk * dk
    rep = n_v // n_qk

    @pl.when(pl.program_id(1) == 0)
    def _():
        s_ref[...] = jnp.zeros_like(s_ref)
        xs_ref[0:8, :] = jnp.zeros((8, xs_ref.shape[1]), F32)

    xs_ref[8:8 + L, :] = x_ref[...].astype(F32)
    conv = cw_ref[0:1, :] * xs_ref[8 - (CONV_K - 1):8 - (CONV_K - 1) + L, :]
    for j in range(1, CONV_K):
        off = 8 - (CONV_K - 1) + j
        conv = conv + cw_ref[j:j + 1, :] * xs_ref[off:off + L, :]
    xs_ref[0:8, :] = xs_ref[L:L + 8, :]
    c = conv * _sigmoid(conv)

    ii = lax.broadcasted_iota(jnp.int32, (L, L), 0)
    jj = lax.broadcasted_iota(jnp.int32, (L, L), 1)
    incl = jj <= ii
    strict = jj < ii
    eye = (ii == jj).astype(F32)
    gcol = gc_ref[...]
    grow = gr_ref[0]

    for hk in range(n_qk):
        cq = c[:, hk * dk:(hk + 1) * dk]
        ck = c[:, qk_w + hk * dk:qk_w + (hk + 1) * dk]
        qn = cq * lax.rsqrt(jnp.sum(cq * cq, axis=1, keepdims=True) + RMS_EPS) * (dk ** -0.5)
        kn = ck * lax.rsqrt(jnp.sum(ck * ck, axis=1, keepdims=True) + RMS_EPS)
        qb = qn.astype(BF16)
        kb = kn.astype(BF16)
        kk = _dot_nt(kb, kb)
        qk = _dot_nt(qb, kb)
        for r in range(rep):
            h = hk * rep + r
            beta_c = _sigmoid(gcol[:, gb_off + h:gb_off + h + 1])
            gd_c = -jnp.exp(al_c_ref[:, h:h + 1]) * _softplus(gcol[:, ga_off + h:ga_off + h + 1] + dt_c_ref[:, h:h + 1])
            gd_r = -jnp.exp(al_r_ref[h:h + 1, :]) * _softplus(grow[ga_off + h:ga_off + h + 1, :] + dt_r_ref[h:h + 1, :])
            gam_c = jnp.sum(jnp.where(incl, gd_r, 0.0), axis=1, keepdims=True)
            gam_r = jnp.sum(jnp.where(ii <= jj, gd_c, 0.0), axis=0, keepdims=True)
            g_tot = jnp.sum(gd_r, axis=1, keepdims=True)
            decm = jnp.exp(jnp.where(incl, gam_c - gam_r, -jnp.inf))
            a = jnp.where(strict, beta_c * kk * decm, 0.0)
            x = eye - a
            pw = a
            for _ in range(5):
                pwb = pw.astype(BF16)
                pw = _dot(pwb, pwb)
                x = x + _dot(x.astype(BF16), pw.astype(BF16))
            vh = c[:, 2 * qk_w + h * dv:2 * qk_w + (h + 1) * dv]
            eg = jnp.exp(gam_c)
            xb = x.astype(BF16)
            u = _dot(xb, (vh * beta_c).astype(BF16))
            w = _dot(xb, (kn * (beta_c * eg)).astype(BF16))
            attn = qk * decm
            qd = qn * eg
            kd = kn * jnp.exp(g_tot - gam_c)
            s_st = s_ref[h]
            sb = s_st.astype(BF16)
            v_new = u - _dot(w.astype(BF16), sb)
            vnb = v_new.astype(BF16)
            o = _dot(qd.astype(BF16), sb) + _dot(attn.astype(BF16), vnb)
            s_ref[h] = jnp.exp(g_tot) * s_st + _dot(kd.T.astype(BF16), vnb)
            rms = lax.rsqrt(jnp.mean(o * o, axis=1, keepdims=True) + RMS_EPS)
            zz = z_ref[:, h * dv:(h + 1) * dv].astype(F32)
            out_ref[:, h * dv:(h + 1) * dv] = (o * rms * ng_ref[...] * (zz * _sigmoid(zz))).astype(BF16)


def _gdn(proj, small, small_t, conv_w, al_c, al_r, dt_c, dt_r, norm_g, bsz, seq, cols, n_qk, n_v,
         ga_off, gb_off):
    n = seq // CHUNK
    L = CHUNK
    conv_ch = 2 * n_qk * GDN_DK + n_v * GDN_DV
    v_w = n_v * GDN_DV
    t = bsz * seq
    kern = functools.partial(_gdn_kernel, n_qk=n_qk, n_v=n_v, ga_off=ga_off, gb_off=gb_off)
    row = lambda b, c: b * n + c
    full = lambda a: pl.BlockSpec(a.shape, lambda b, c: (0,) * a.ndim)
    return pl.pallas_call(
        kern,
        grid=(bsz, n),
        in_specs=[
            pl.BlockSpec((L, conv_ch), lambda b, c: (row(b, c), cols["gqkv"] // conv_ch)),
            pl.BlockSpec((L, v_w), lambda b, c: (row(b, c), cols["gz"] // v_w)),
            pl.BlockSpec((L, LANES), lambda b, c: (row(b, c), 0)),
            pl.BlockSpec((1, small_t.shape[1], L), lambda b, c: (row(b, c), 0, 0)),
            full(conv_w), full(al_c), full(al_r), full(dt_c), full(dt_r), full(norm_g),
        ],
        out_specs=pl.BlockSpec((L, v_w), lambda b, c: (row(b, c), 0)),
        out_shape=jax.ShapeDtypeStruct((t, v_w), BF16),
        scratch_shapes=[
            pltpu.VMEM((n_v, GDN_DK, GDN_DV), F32),
            pltpu.VMEM((8 + L + 8, conv_ch), F32),
        ],
        compiler_params=_params("arbitrary", "arbitrary"),
        name="gdn",
    )(proj, proj, small, small_t, conv_w, al_c, al_r, dt_c, dt_r, norm_g)


def _merge_kernel(hm_ref, og_ref, gm_ref, gg_ref, h0_ref, wbm_ref, wbg_ref, wo_ref, g_ref, b_ref, out_ref):
    y_ml = _dot(hm_ref[...], wbm_ref[...])
    y_gdn = _dot(og_ref[...], wbg_ref[...])
    merged = _sigmoid(gm_ref[...].astype(F32)) * y_ml + _sigmoid(gg_ref[...].astype(F32)) * y_gdn
    mix = _dot(merged.astype(BF16), wo_ref[...])
    out_ref[...] = _layer_norm(DN_ALPHA * h0_ref[...] + mix, g_ref[...], b_ref[...])


def _merge(hm, og, proj, h0, w_bm, w_bg, w_out, g, b, cols, tm):
    t, d = h0.shape
    full = lambda a: pl.BlockSpec(a.shape, lambda i: (0,) * a.ndim)
    return pl.pallas_call(
        _merge_kernel,
        grid=(t // tm,),
        in_specs=[
            pl.BlockSpec((tm, hm.shape[1]), lambda i: (i, 0)),
            pl.BlockSpec((tm, og.shape[1]), lambda i: (i, 0)),
            pl.BlockSpec((tm, d), lambda i: (i, cols["gate_ml"] // d)),
            pl.BlockSpec((tm, d), lambda i: (i, cols["gate_gdn"] // d)),
            pl.BlockSpec((tm, d), lambda i: (i, 0)),
            full(w_bm), full(w_bg), full(w_out), full(g), full(b),
        ],
        out_specs=pl.BlockSpec((tm, d), lambda i: (i, 0)),
        out_shape=jax.ShapeDtypeStruct((t, d), F32),
        compiler_params=_params("arbitrary"),
        name="merge_out_ln1",
    )(hm, og, proj, proj, h0, w_bm, w_bg, w_out, g, b)


def _kv_kernel(m_ref, wk_ref, wv_ref, k_ref, v_ref):
    mb = m_ref[...].astype(BF16)
    k_ref[...] = _dot(mb, wk_ref[...]).astype(BF16)
    v_ref[...] = _dot(mb, wv_ref[...]).astype(BF16)


def _kv_proj(mem2, wk, wv, tm):
    t, d = mem2.shape
    full = lambda a: pl.BlockSpec(a.shape, lambda i: (0,) * a.ndim)
    return pl.pallas_call(
        _kv_kernel,
        grid=(t // tm,),
        in_specs=[pl.BlockSpec((tm, d), lambda i: (i, 0)), full(wk), full(wv)],
        out_specs=[pl.BlockSpec((tm, d), lambda i: (i, 0))] * 2,
        out_shape=[jax.ShapeDtypeStruct((t, d), BF16)] * 2,
        compiler_params=_params("arbitrary"),
        name="mem_kv_proj",
    )(mem2, wk, wv)


def _xattn_kernel(h1_ref, k_ref, v_ref, wq_ref, wo_ref, g_ref, b_ref, wr_ref, br_ref,
                  h2_ref, gw_ref, route_ref, cnt_ref, carry_ref, *, dh):
    tm = h1_ref.shape[0]

    @pl.when((pl.program_id(0) == 0) & (pl.program_id(1) == 0))
    def _():
        carry_ref[...] = jnp.zeros_like(carry_ref)

    h1 = h1_ref[...]
    q = _dot(h1.astype(BF16), wq_ref[...])
    outs = []
    for hd in range(XA_HEADS):
        qh = q[:, hd * dh:(hd + 1) * dh].astype(BF16)
        kh = k_ref[:, hd * dh:(hd + 1) * dh]
        vh = v_ref[:, hd * dh:(hd + 1) * dh]
        sc = _dot_nt(qh, kh) * (dh ** -0.5)
        e = jnp.exp(sc - jnp.max(sc, axis=1, keepdims=True))
        p = e / jnp.sum(e, axis=1, keepdims=True)
        outs.append(_dot(p.astype(BF16), vh))
    o = jnp.concatenate(outs, axis=1)
    xa = _dot(o.astype(BF16), wo_ref[...])
    h2 = _layer_norm(DN_ALPHA * h1 + xa, g_ref[...], b_ref[...])
    h2_ref[...] = h2

    logits = _dot(h2.astype(BF16), wr_ref[...]) + br_ref[...]
    lane = lax.broadcasted_iota(jnp.int32, (tm, LANES), 1)
    lane_f = lane.astype(F32)
    work = logits
    vals, idxs = [], []
    for _ in range(TOP_K):
        m = jnp.max(work, axis=1, keepdims=True)
        idx = jnp.min(jnp.where(work == m, lane_f, float(LANES)), axis=1, keepdims=True)
        vals.append(m)
        idxs.append(idx)
        work = jnp.where(lane_f == idx, -jnp.inf, work)
    es = [jnp.exp(v - vals[0]) for v in vals]
    tot = es[0]
    for e_ in es[1:]:
        tot = tot + e_
    onehot = jnp.zeros((tm, LANES), F32)
    for idx in idxs:
        onehot = onehot + (lane_f == idx).astype(F32)
    ri = lax.broadcasted_iota(jnp.int32, (tm, tm), 0)
    ci = lax.broadcasted_iota(jnp.int32, (tm, tm), 1)
    tri = (ci < ri).astype(BF16)
    carry = carry_ref[0:1, :]
    ranks = carry + _dot(tri, onehot.astype(BF16))
    gw = jnp.zeros((tm, LANES), F32)
    route = jnp.zeros((tm, LANES), F32)
    for k_ in range(TOP_K):
        rk = jnp.sum(jnp.where(lane_f == idxs[k_], ranks, 0.0), axis=1, keepdims=True)
        gw = gw + jnp.where(lane == k_, es[k_] / tot, 0.0)
        route = route + jnp.where(lane == k_, idxs[k_], 0.0) + jnp.where(lane == TOP_K + k_, rk, 0.0)
    gw_ref[...] = gw
    route_ref[...] = route.astype(jnp.int32)
    carry = carry + jnp.sum(onehot, axis=0, keepdims=True)
    carry_ref[...] = jnp.broadcast_to(carry, carry_ref.shape)
    cnt_ref[...] = jnp.broadcast_to(carry, cnt_ref.shape)


def _xattn(h1, kmem, vmem, wq, wo, g, b, w_r, b_r, bsz, seq, mem_len, tm):
    t, d = h1.shape
    nt = seq // tm
    full = lambda a: pl.BlockSpec(a.shape, lambda i, j: (0,) * a.ndim)
    kern = functools.partial(_xattn_kernel, dh=d // XA_HEADS)
    return pl.pallas_call(
        kern,
        grid=(bsz, nt),
        in_specs=[
            pl.BlockSpec((tm, d), lambda i, j: (i * nt + j, 0)),
            pl.BlockSpec((mem_len, d), lambda i, j: (i, 0)),
            pl.BlockSpec((mem_len, d), lambda i, j: (i, 0)),
            full(wq), full(wo), full(g), full(b), full(w_r), full(b_r),
        ],
        out_specs=[
            pl.BlockSpec((tm, d), lambda i, j: (i * nt + j, 0)),
            pl.BlockSpec((tm, LANES), lambda i, j: (i * nt + j, 0)),
            pl.BlockSpec((tm, LANES), lambda i, j: (i * nt + j, 0)),
            pl.BlockSpec((8, LANES), lambda i, j: (0, 0)),
        ],
        out_shape=[
            jax.ShapeDtypeStruct((t, d), F32),
            jax.ShapeDtypeStruct((t, LANES), F32),
            jax.ShapeDtypeStruct((t, LANES), jnp.int32),
            jax.ShapeDtypeStruct((8, LANES), F32),
        ],
        scratch_shapes=[pltpu.VMEM((8, LANES), F32)],
        compiler_params=_params("arbitrary", "arbitrary"),
        name="xattn_ln2_router",
    )(h1, kmem, vmem, wq, wo, g, b, w_r, b_r)


def _dispatch_kernel(pad_lo_ref, pad_n_ref, na_ref, dest_ref, h_ref, xs_ref, zero_ref, sem, zsem):
    tm = h_ref.shape[0]
    n_blocks = xs_ref.shape[0] // MOE_BLOCK

    @pl.when(pl.program_id(0) == 0)
    def _():
        zero_ref[...] = jnp.zeros_like(zero_ref)
        zero_row = zero_ref.at[pl.ds(0, 1), :]

        def per_expert(e, carry):
            lo = pad_lo_ref[e]
            n = pad_n_ref[e]

            def start(r, c):
                pltpu.make_async_copy(zero_row, xs_ref.at[pl.ds(lo + r, 1), :], zsem).start()
                return c

            def wait(r, c):
                pltpu.make_async_copy(zero_row, xs_ref.at[pl.ds(0, 1), :], zsem).wait()
                return c

            lax.fori_loop(0, n, start, 0)
            lax.fori_loop(0, n, wait, 0)
            return carry

        lax.fori_loop(0, N_EXPERTS, per_expert, 0)

        def tail(b, carry):
            cp = pltpu.make_async_copy(zero_ref, xs_ref.at[pl.ds(b * MOE_BLOCK, MOE_BLOCK), :], zsem)
            cp.start()
            cp.wait()
            return carry

        lax.fori_loop(na_ref[0], n_blocks, tail, 0)

    def issue(i, carry):
        tok = i // TOP_K
        pltpu.make_async_copy(h_ref.at[pl.ds(tok, 1), :], xs_ref.at[pl.ds(dest_ref[i], 1), :], sem).start()
        return carry

    lax.fori_loop(0, tm * TOP_K, issue, 0)
    for _ in range(TOP_K):
        pltpu.make_async_copy(h_ref, xs_ref.at[pl.ds(0, tm), :], sem).wait()


def _dispatch(pad_lo, pad_n, nact, dest_flat, h2, n_rows, tm):
    t, d = h2.shape
    grid_spec = pltpu.PrefetchScalarGridSpec(
        num_scalar_prefetch=3,
        grid=(t // tm,),
        in_specs=[
            pl.BlockSpec((tm * TOP_K,), lambda i, *_: (i,), memory_space=pltpu.SMEM),
            pl.BlockSpec((tm, d), lambda i, *_: (i, 0)),
        ],
        out_specs=pl.BlockSpec(memory_space=pl.ANY),
        scratch_shapes=[pltpu.VMEM((MOE_BLOCK, d), F32), pltpu.SemaphoreType.DMA(()),
                        pltpu.SemaphoreType.DMA(())],
    )
    return pl.pallas_call(
        _dispatch_kernel,
        grid_spec=grid_spec,
        out_shape=jax.ShapeDtypeStruct((n_rows, d), F32),
        compiler_params=_params("arbitrary"),
        name="moe_dispatch",
    )(pad_lo, pad_n, nact, dest_flat, h2)


def _expert_kernel(be_ref, na_ref, x_ref, wgu_ref, bgu_ref, wdn_ref, bdn_ref, y_ref, *, d_exp):
    blk = pl.program_id(0)

    @pl.when(blk >= na_ref[0])
    def _():
        y_ref[...] = jnp.zeros_like(y_ref)

    @pl.when(blk < na_ref[0])
    def _():
        gu = _dot(x_ref[...].astype(BF16), wgu_ref[0]) + bgu_ref[0]
        gate = jnp.minimum(gu[:, :d_exp], SWIGLU_LIMIT)
        up = jnp.clip(gu[:, d_exp:], -SWIGLU_LIMIT, SWIGLU_LIMIT)
        act = (up + 1.0) * (gate * _sigmoid(SWIGLU_ALPHA * gate))
        y_ref[...] = _dot(act.astype(BF16), wdn_ref[0]) + bdn_ref[0]


def _experts(block_e, nact, xs, w_gu, b_gu, w_dn, b_dn):
    n_rows, d = xs.shape
    n_blocks = n_rows // MOE_BLOCK
    d_exp = w_dn.shape[1]
    kern = functools.partial(_expert_kernel, d_exp=d_exp)
    emap = lambda i, be, na: (be[i], 0, 0)
    grid_spec = pltpu.PrefetchScalarGridSpec(
        num_scalar_prefetch=2,
        grid=(n_blocks,),
        in_specs=[
            pl.BlockSpec((MOE_BLOCK, d), lambda i, be, na: (i, 0)),
            pl.BlockSpec((1, d, 2 * d_exp), emap),
            pl.BlockSpec((1, 1, 2 * d_exp), emap),
            pl.BlockSpec((1, d_exp, d), emap),
            pl.BlockSpec((1, 1, d), emap),
        ],
        out_specs=pl.BlockSpec((MOE_BLOCK, d), lambda i, be, na: (i, 0)),
    )
    return pl.pallas_call(
        kern,
        grid_spec=grid_spec,
        out_shape=jax.ShapeDtypeStruct((n_rows, d), F32),
        compiler_params=_params("arbitrary"),
        name="moe_experts",
    )(block_e, nact, xs, w_gu, b_gu, w_dn, b_dn)


def _combine_kernel(dest_ref, gw_ref, h_ref, g_ref, b_ref, y_ref, out_ref, buf_ref, sem):
    tm = h_ref.shape[0]

    def issue(i, carry):
        tok = i // TOP_K
        k_ = i % TOP_K
        pltpu.make_async_copy(y_ref.at[pl.ds(dest_ref[i], 1), :], buf_ref.at[k_, pl.ds(tok, 1), :], sem).start()
        return carry

    lax.fori_loop(0, tm * TOP_K, issue, 0)
    for k_ in range(TOP_K):
        pltpu.make_async_copy(y_ref.at[pl.ds(0, tm), :], buf_ref.at[k_], sem).wait()
    gw = gw_ref[...]
    ff = buf_ref[0] * gw[:, 0:1]
    for k_ in range(1, TOP_K):
        ff = ff + buf_ref[k_] * gw[:, k_:k_ + 1]
    out_ref[...] = _layer_norm(DN_ALPHA * h_ref[...] + ff, g_ref[...], b_ref[...])


def _combine(dest_flat, gw, h2, g, b, yb, tm):
    t, d = h2.shape
    return pl.pallas_call(
        _combine_kernel,
        grid=(t // tm,),
        in_specs=[
            pl.BlockSpec((tm * TOP_K,), lambda i: (i,), memory_space=pltpu.SMEM),
            pl.BlockSpec((tm, LANES), lambda i: (i, 0)),
            pl.BlockSpec((tm, d), lambda i: (i, 0)),
            pl.BlockSpec((1, d), lambda i: (0, 0)),
            pl.BlockSpec((1, d), lambda i: (0, 0)),
            pl.BlockSpec(memory_space=pl.ANY),
        ],
        out_specs=pl.BlockSpec((tm, d), lambda i: (i, 0)),
        out_shape=jax.ShapeDtypeStruct((t, d), F32),
        scratch_shapes=[pltpu.VMEM((TOP_K, tm, d), F32), pltpu.SemaphoreType.DMA(())],
        compiler_params=_params("arbitrary"),
        name="moe_combine_ln3",
    )(dest_flat, gw, h2, g, b, yb)


def _pick(n, pref):
    return pref if n % pref == 0 else n


def kernel(x, mem, ln_in_g, ln_in_b, w_in, ml_gate_bias, ml_norm_g, gdn_conv_w, gdn_a_log, gdn_dt_bias, gdn_norm_g, w_branch_ml, w_branch_gdn, w_mix_out, ln1_g, ln1_b, xa_wq, xa_wk, xa_wv, xa_wo, ln2_g, ln2_b, w_router, b_router, w_gu, b_gu, w_dn, b_dn, ln3_g, ln3_b):
    bsz, seq, d = x.shape
    mem_len = mem.shape[1]
    t = bsz * seq
    ml_dv = d // ML_HEADS
    ml_dqk = ml_dv // 2
    ml_qk_w, ml_v_w = ML_HEADS * ml_dqk, ML_HEADS * ml_dv
    n_qk = d // GDN_DK
    n_v = 2 * n_qk
    gdn_qk_w, gdn_v_w = n_qk * GDN_DK, n_v * GDN_DV
    conv_ch = 2 * gdn_qk_w + gdn_v_w
    splits = (ml_qk_w, ml_qk_w, ml_v_w, ml_v_w, 2 * ML_HEADS, conv_ch, gdn_v_w, n_v, n_v, d, d)
    names = ("mq", "mk", "mv", "mo", "mif", "gqkv", "gz", "ga", "gb", "gate_ml", "gate_gdn")
    starts = {}
    acc = 0
    for nm, sz in zip(names, splits):
        starts[nm] = (acc, sz)
        acc += sz
    row2 = lambda a: a.reshape(1, -1).astype(F32)

    h = x.reshape(t, d)
    for l in range(DEPTH):
        w = w_in[l]
        seg = lambda nm: w[:, starts[nm][0]:starts[nm][0] + starts[nm][1]]
        order = ("gqkv", "gz", "mv", "mo", "gate_ml", "gate_gdn", "mq", "mk")
        cols = {}
        off = 0
        for nm in order:
            assert off % starts[nm][1] == 0
            cols[nm] = off
            off += starts[nm][1]
        w_big = jnp.concatenate([seg(nm) for nm in order], axis=1).astype(BF16)
        n_small = 2 * ML_HEADS + 2 * n_v
        w_small = jnp.concatenate([seg("mif"), seg("ga"), seg("gb"),
                                   jnp.zeros((d, LANES - n_small), F32)], axis=1).astype(BF16)
        ga_off, gb_off = 2 * ML_HEADS, 2 * ML_HEADS + n_v

        if l == 0:
            h0, proj, small = _ln_proj(h, row2(ln_in_g), row2(ln_in_b), w_big, w_small,
                                       _pick(t, 1024), _pick(off, 1024))
        else:
            raise NotImplementedError("DEPTH > 1")
        n_ch = seq // CHUNK
        small_t = small[:, :n_small].reshape(bsz * n_ch, CHUNK, n_small).transpose(0, 2, 1)

        bias = ml_gate_bias[l].astype(F32)
        hm = _mlstm(proj, small, small_t, bias.reshape(1, -1), bias.reshape(-1, 1), row2(ml_norm_g[l]),
                    bsz, seq, cols, ml_dqk, ml_dv)
        al = gdn_a_log[l].astype(F32)
        dt = gdn_dt_bias[l].astype(F32)
        og = _gdn(proj, small, small_t, gdn_conv_w[l].astype(F32), al.reshape(1, -1), al.reshape(-1, 1),
                  dt.reshape(1, -1), dt.reshape(-1, 1), row2(gdn_norm_g[l]), bsz, seq, cols, n_qk, n_v,
                  ga_off, gb_off)
        h1 = _merge(hm, og, proj, h0, w_branch_ml[l].astype(BF16), w_branch_gdn[l].astype(BF16),
                    w_mix_out[l].astype(BF16), row2(ln1_g[l]), row2(ln1_b[l]), cols, _pick(t, 512))

        kmem, vmem = _kv_proj(mem.reshape(bsz * mem_len, d), xa_wk[l].astype(BF16), xa_wv[l].astype(BF16),
                              _pick(bsz * mem_len, 512))
        w_r = jnp.concatenate([w_router[l], jnp.zeros((d, LANES - N_EXPERTS), F32)], axis=1).astype(BF16)
        b_r = jnp.concatenate([b_router[l].astype(F32), jnp.full((LANES - N_EXPERTS,), NEG_BIG, F32)]).reshape(1, -1)
        tm_x = _pick(seq, 256)
        h2, gw, route, cnt = _xattn(h1, kmem, vmem, xa_wq[l].astype(BF16), xa_wo[l].astype(BF16),
                                    row2(ln2_g[l]), row2(ln2_b[l]), w_r, b_r, bsz, seq, mem_len, tm_x)

        counts = cnt[0, :N_EXPERTS].astype(jnp.int32)
        padded = (counts + MOE_BLOCK - 1) // MOE_BLOCK * MOE_BLOCK
        pad_end = jnp.cumsum(padded)
        pad_start = pad_end - padded
        n_asg = t * TOP_K
        n_blocks = -(-n_asg // MOE_BLOCK) + N_EXPERTS
        n_rows = n_blocks * MOE_BLOCK
        top_e = route[:, :TOP_K]
        rank = route[:, TOP_K:2 * TOP_K]
        sel = top_e[:, :, None] == jnp.arange(N_EXPERTS, dtype=jnp.int32)[None, None, :]
        dest = (jnp.sum(jnp.where(sel, pad_start[None, None, :], 0), axis=-1) + rank).astype(jnp.int32)
        dest_flat = dest.reshape(n_asg)
        blk_row = jnp.arange(n_blocks, dtype=jnp.int32) * MOE_BLOCK
        block_e = jnp.minimum(jnp.searchsorted(pad_end, blk_row, side="right"), N_EXPERTS - 1).astype(jnp.int32)
        nact = (pad_end[-1:] // MOE_BLOCK).astype(jnp.int32)
        pad_lo = (pad_start + counts).astype(jnp.int32)
        pad_n = (padded - counts).astype(jnp.int32)

        tm_d = _pick(t, 256)
        xs = _dispatch(pad_lo, pad_n, nact, dest_flat, h2, n_rows, tm_d)
        yb = _experts(block_e, nact, xs, w_gu[l].astype(BF16), b_gu[l].astype(F32)[:, None, :],
                      w_dn[l].astype(BF16), b_dn[l].astype(F32)[:, None, :])
        h = _combine(dest_flat, gw, h2, row2(ln3_g[l]), row2(ln3_b[l]), yb, tm_d)
    return h.reshape(bsz, seq, d)
```

```python
import functools

import jax
import jax.numpy as jnp
from jax import lax
from jax.experimental import pallas as pl
from jax.experimental.pallas import tpu as pltpu

F32 = jnp.float32
BF16 = jnp.bfloat16

CHUNK = 64
ML_HEADS = 4
GDN_DK = 128
GDN_DV = 128
CONV_K = 4
XA_HEADS = 4
N_EXPERTS = 32
TOP_K = 4
SWIGLU_LIMIT = 7.0
SWIGLU_ALPHA = 1.702
MOE_BLOCK = 256
DEPTH = 1
DN_ALPHA = (2 * DEPTH) ** 0.25
LN_EPS = 1e-5
RMS_EPS = 1e-6
LANES = 128
VMEM_LIMIT = 56 * 1024 * 1024
NEG_BIG = -1e30
ISSUE_UNROLL = 4


def _params(*sem):
    return pltpu.CompilerParams(dimension_semantics=sem, vmem_limit_bytes=VMEM_LIMIT)


def _dot(a, b):
    return jnp.dot(a, b, preferred_element_type=F32)


def _dot_nt(a, b):
    return lax.dot_general(a, b, (((1,), (1,)), ((), ())), preferred_element_type=F32)


def _layer_norm(x, g, b):
    mu = jnp.mean(x, axis=-1, keepdims=True)
    xc = x - mu
    var = jnp.mean(xc * xc, axis=-1, keepdims=True)
    return xc * lax.rsqrt(var + LN_EPS) * g + b


def _sigmoid(x):
    return 1.0 / (1.0 + jnp.exp(-x))


def _log_sigmoid(x):
    return jnp.minimum(x, 0.0) - jnp.log(1.0 + jnp.exp(-jnp.abs(x)))


def _softplus(x):
    return jnp.maximum(x, 0.0) + jnp.log(1.0 + jnp.exp(-jnp.abs(x)))


def _ln_proj_kernel(x_ref, g_ref, b_ref, w_ref, ws_ref, h_ref, p_ref, s_ref, xn_ref):
    @pl.when(pl.program_id(1) == 0)
    def _():
        h = _layer_norm(x_ref[...], g_ref[...], b_ref[...])
        h_ref[...] = h
        hb = h.astype(BF16)
        xn_ref[...] = hb
        s_ref[...] = _dot(hb, ws_ref[...])

    p_ref[...] = _dot(xn_ref[...], w_ref[...]).astype(BF16)


def _ln_proj(x2, g, b, w_big, w_small, tm, tn):
    t, d = x2.shape
    nw = w_big.shape[1]
    return pl.pallas_call(
        _ln_proj_kernel,
        grid=(t // tm, nw // tn),
        in_specs=[
            pl.BlockSpec((tm, d), lambda i, j: (i, 0)),
            pl.BlockSpec((1, d), lambda i, j: (0, 0)),
            pl.BlockSpec((1, d), lambda i, j: (0, 0)),
            pl.BlockSpec((d, tn), lambda i, j: (0, j)),
            pl.BlockSpec((d, LANES), lambda i, j: (0, 0)),
        ],
        out_specs=[
            pl.BlockSpec((tm, d), lambda i, j: (i, 0)),
            pl.BlockSpec((tm, tn), lambda i, j: (i, j)),
            pl.BlockSpec((tm, LANES), lambda i, j: (i, 0)),
        ],
        out_shape=[
            jax.ShapeDtypeStruct((t, d), F32),
            jax.ShapeDtypeStruct((t, nw), BF16),
            jax.ShapeDtypeStruct((t, LANES), F32),
        ],
        scratch_shapes=[pltpu.VMEM((tm, d), BF16)],
        compiler_params=_params("arbitrary", "arbitrary"),
        name="ln_in_proj",
    )(x2, g, b, w_big, w_small)


def _mlstm_kernel(q_ref, k_ref, v_ref, o_ref, gc_ref, gr_ref, bc_ref, br_ref, ng_ref,
                  out_ref, c_ref, n_ref, m_ref, *, dqk, dv):
    L = CHUNK

    @pl.when(pl.program_id(1) == 0)
    def _():
        c_ref[...] = jnp.zeros_like(c_ref)
        n_ref[...] = jnp.zeros_like(n_ref)
        m_ref[...] = jnp.zeros_like(m_ref)

    ii = lax.broadcasted_iota(jnp.int32, (L, L), 0)
    jj = lax.broadcasted_iota(jnp.int32, (L, L), 1)
    causal = jj <= ii
    gcol = gc_ref[...]
    grow = gr_ref[0]
    scale = dqk ** -0.5
    for h in range(ML_HEADS):
        li_c = gcol[:, h:h + 1] + bc_ref[:, h:h + 1]
        lf_c = _log_sigmoid(gcol[:, ML_HEADS + h:ML_HEADS + h + 1] + bc_ref[:, ML_HEADS + h:ML_HEADS + h + 1])
        li_r = grow[h:h + 1, :] + br_ref[h:h + 1, :]
        lf_r = _log_sigmoid(grow[ML_HEADS + h:ML_HEADS + h + 1, :] + br_ref[ML_HEADS + h:ML_HEADS + h + 1, :])
        b_c = jnp.sum(jnp.where(causal, lf_r, 0.0), axis=1, keepdims=True)
        b_r = jnp.sum(jnp.where(ii <= jj, lf_c, 0.0), axis=0, keepdims=True)
        g = jnp.sum(lf_r, axis=1, keepdims=True)
        dmat = jnp.where(causal, b_c - b_r + li_r, -jnp.inf)
        m_intra = jnp.max(dmat, axis=1, keepdims=True)
        qh = q_ref[:, h * dqk:(h + 1) * dqk]
        kh = k_ref[:, h * dqk:(h + 1) * dqk]
        vh = v_ref[:, h * dv:(h + 1) * dv]
        kf = kh.astype(F32) * scale
        p = jnp.exp(dmat - m_intra) * _dot_nt(qh, kf.astype(BF16))
        intra_num = _dot(p.astype(BF16), vh)
        intra_den = jnp.sum(p, axis=1, keepdims=True)
        a_c = g - b_c + li_c
        m_chunk = jnp.max(a_c, axis=0, keepdims=True)
        kw = kf * jnp.exp(a_c - m_chunk)

        c_st = c_ref[h]
        n_st = n_ref[h:h + 1, :]
        m_st = m_ref[h:h + 1, 0:1]
        inter_log = b_c + m_st
        m_out = jnp.maximum(inter_log, m_intra)
        s_inter = jnp.exp(inter_log - m_out)
        s_intra = jnp.exp(m_intra - m_out)
        num = s_inter * _dot(qh, c_st.astype(BF16)) + s_intra * intra_num
        qn = jnp.sum(qh.astype(F32) * n_st, axis=1, keepdims=True)
        den = s_inter * qn + s_intra * intra_den
        hh = num / jnp.maximum(jnp.abs(den), jnp.exp(-m_out))
        m_new = jnp.maximum(g + m_st, m_chunk)
        dec = jnp.exp(g + m_st - m_new)
        s_new = jnp.exp(m_chunk - m_new)
        c_ref[h] = dec * c_st + s_new * _dot(kw.T.astype(BF16), vh)
        n_ref[h:h + 1, :] = dec * n_st + s_new * jnp.sum(kw, axis=0, keepdims=True)
        m_ref[h:h + 1, :] = jnp.broadcast_to(m_new, (1, LANES))

        rms = lax.rsqrt(jnp.mean(hh * hh, axis=1, keepdims=True) + RMS_EPS)
        og = _sigmoid(o_ref[:, h * dv:(h + 1) * dv].astype(F32))
        out_ref[:, h * dv:(h + 1) * dv] = (hh * rms * ng_ref[:, h * dv:(h + 1) * dv] * og).astype(BF16)


def _mlstm(proj, small, small_t, bias_c, bias_r, norm_g, bsz, seq, cols, dqk, dv):
    n = seq // CHUNK
    L = CHUNK
    qk_w, v_w = ML_HEADS * dqk, ML_HEADS * dv
    t = bsz * seq
    kern = functools.partial(_mlstm_kernel, dqk=dqk, dv=dv)
    row = lambda b, c: b * n + c
    return pl.pallas_call(
        kern,
        grid=(bsz, n),
        in_specs=[
            pl.BlockSpec((L, qk_w), lambda b, c: (row(b, c), cols["mq"] // qk_w)),
            pl.BlockSpec((L, qk_w), lambda b, c: (row(b, c), cols["mk"] // qk_w)),
            pl.BlockSpec((L, v_w), lambda b, c: (row(b, c), cols["mv"] // v_w)),
            pl.BlockSpec((L, v_w), lambda b, c: (row(b, c), cols["mo"] // v_w)),
            pl.BlockSpec((L, LANES), lambda b, c: (row(b, c), 0)),
            pl.BlockSpec((1, small_t.shape[1], L), lambda b, c: (row(b, c), 0, 0)),
            pl.BlockSpec(bias_c.shape, lambda b, c: (0, 0)),
            pl.BlockSpec(bias_r.shape, lambda b, c: (0, 0)),
            pl.BlockSpec((1, v_w), lambda b, c: (0, 0)),
        ],
        out_specs=pl.BlockSpec((L, v_w), lambda b, c: (row(b, c), 0)),
        out_shape=jax.ShapeDtypeStruct((t, v_w), BF16),
        scratch_shapes=[
            pltpu.VMEM((ML_HEADS, dqk, dv), F32),
            pltpu.VMEM((8, dqk), F32),
            pltpu.VMEM((8, LANES), F32),
        ],
        compiler_params=_params("arbitrary", "arbitrary"),
        name="mlstm",
    )(proj, proj, proj, proj, small, small_t, bias_c, bias_r, norm_g)


def _gdn_kernel(x_ref, z_ref, gc_ref, gr_ref, cw_ref, al_c_ref, al_r_ref, dt_c_ref, dt_r_ref, ng_ref,
                out_ref, s_ref, xs_ref, c_ref, qn_ref, kn_ref, kk_ref, qk_ref, pw_ref, x_ref_, rhs_ref,
                attn_ref, qd_ref, kd_ref, gt_ref, sol_ref, vn_ref, *, n_qk, n_v, ga_off, gb_off):
    L = CHUNK
    dk, dv = GDN_DK, GDN_DV
    qk_w = n_qk * dk
    rep = n_v // n_qk

    @pl.when(pl.program_id(1) == 0)
    def _():
        s_ref[...] = jnp.zeros_like(s_ref)
        xs_ref[0:8, :] = jnp.zeros((8, xs_ref.shape[1]), F32)

    xs_ref[8:8 + L, :] = x_ref[...].astype(F32)
    conv = cw_ref[0:1, :] * xs_ref[8 - (CONV_K - 1):8 - (CONV_K - 1) + L, :]
    for j in range(1, CONV_K):
        off = 8 - (CONV_K - 1) + j
        conv = conv + cw_ref[j:j + 1, :] * xs_ref[off:off + L, :]
    xs_ref[0:8, :] = xs_ref[L:L + 8, :]
    c_ref[...] = conv * _sigmoid(conv)

    ii = lax.broadcasted_iota(jnp.int32, (L, L), 0)
    jj = lax.broadcasted_iota(jnp.int32, (L, L), 1)
    incl = jj <= ii
    strict = jj < ii
    eye = (ii == jj).astype(F32)

    for hk in range(n_qk):
        cq = c_ref[:, hk * dk:(hk + 1) * dk]
        ck = c_ref[:, qk_w + hk * dk:qk_w + (hk + 1) * dk]
        qn = cq * lax.rsqrt(jnp.sum(cq * cq, axis=1, keepdims=True) + RMS_EPS) * (dk ** -0.5)
        kn = ck * lax.rsqrt(jnp.sum(ck * ck, axis=1, keepdims=True) + RMS_EPS)
        qn_ref[hk] = qn
        kn_ref[hk] = kn
        kb = kn.astype(BF16)
        kk_ref[hk] = _dot_nt(kb, kb)
        qk_ref[hk] = _dot_nt(qn.astype(BF16), kb)

    for h in range(n_v):
        hk = h // rep
        beta_c = _sigmoid(gc_ref[:, gb_off + h:gb_off + h + 1])
        gd_c = -jnp.exp(al_c_ref[:, h:h + 1]) * _softplus(gc_ref[:, ga_off + h:ga_off + h + 1] + dt_c_ref[:, h:h + 1])
        gd_r = -jnp.exp(al_r_ref[h:h + 1, :]) * _softplus(gr_ref[0, ga_off + h:ga_off + h + 1, :] + dt_r_ref[h:h + 1, :])
        gam_c = jnp.sum(jnp.where(incl, gd_r, 0.0), axis=1, keepdims=True)
        gam_r = jnp.sum(jnp.where(ii <= jj, gd_c, 0.0), axis=0, keepdims=True)
        g_tot = jnp.sum(gd_r, axis=1, keepdims=True)
        decm = jnp.exp(jnp.where(incl, gam_c - gam_r, -jnp.inf))
        a = jnp.where(strict, beta_c * kk_ref[hk] * decm, 0.0)
        pw_ref[h] = a
        x_ref_[h] = eye - a
        attn_ref[h] = (qk_ref[hk] * decm).astype(BF16)
        eg = jnp.exp(gam_c)
        kn = kn_ref[hk]
        vh = c_ref[:, 2 * qk_w + h * dv:2 * qk_w + (h + 1) * dv]
        rhs_ref[h, :, 0:dv] = (vh * beta_c).astype(BF16)
        rhs_ref[h, :, dv:dv + dk] = (kn * (beta_c * eg)).astype(BF16)
        qd_ref[h] = (qn_ref[hk] * eg).astype(BF16)
        kd_ref[h] = kn * jnp.exp(g_tot - gam_c)
        gt_ref[h] = jnp.broadcast_to(jnp.exp(g_tot), (1, LANES))

    n_sq = 5
    for lvl in range(n_sq + 1):
        for h in range(n_v):
            pwb = pw_ref[h].astype(BF16)
            if lvl > 0:
                xv = x_ref_[h]
                x_ref_[h] = xv + _dot(xv.astype(BF16), pwb)
            if lvl < n_sq:
                pw_ref[h] = _dot(pwb, pwb)

    for h in range(n_v):
        sol_ref[h] = _dot(x_ref_[h].astype(BF16), rhs_ref[h])
    for h in range(n_v):
        sol = sol_ref[h]
        vn_ref[h] = (sol[:, 0:dv] - _dot(sol[:, dv:dv + dk].astype(BF16), s_ref[h].astype(BF16))).astype(BF16)
    for h in range(n_v):
        s_st = s_ref[h]
        vnb = vn_ref[h]
        o = _dot(qd_ref[h], s_st.astype(BF16)) + _dot(attn_ref[h], vnb)
        s_ref[h] = gt_ref[h][:, 0:1] * s_st + _dot(kd_ref[h].T.astype(BF16), vnb)
        rms = lax.rsqrt(jnp.mean(o * o, axis=1, keepdims=True) + RMS_EPS)
        zz = z_ref[:, h * dv:(h + 1) * dv].astype(F32)
        out_ref[:, h * dv:(h + 1) * dv] = (o * rms * ng_ref[...] * (zz * _sigmoid(zz))).astype(BF16)


def _gdn(proj, small, small_t, conv_w, al_c, al_r, dt_c, dt_r, norm_g, bsz, seq, cols, n_qk, n_v,
         ga_off, gb_off):
    n = seq // CHUNK
    L = CHUNK
    conv_ch = 2 * n_qk * GDN_DK + n_v * GDN_DV
    v_w = n_v * GDN_DV
    t = bsz * seq
    kern = functools.partial(_gdn_kernel, n_qk=n_qk, n_v=n_v, ga_off=ga_off, gb_off=gb_off)
    row = lambda b, c: b * n + c
    full = lambda a: pl.BlockSpec(a.shape, lambda b, c: (0,) * a.ndim)
    return pl.pallas_call(
        kern,
        grid=(bsz, n),
        in_specs=[
            pl.BlockSpec((L, conv_ch), lambda b, c: (row(b, c), cols["gqkv"] // conv_ch)),
            pl.BlockSpec((L, v_w), lambda b, c: (row(b, c), cols["gz"] // v_w)),
            pl.BlockSpec((L, LANES), lambda b, c: (row(b, c), 0)),
            pl.BlockSpec((1, small_t.shape[1], L), lambda b, c: (row(b, c), 0, 0)),
            full(conv_w), full(al_c), full(al_r), full(dt_c), full(dt_r), full(norm_g),
        ],
        out_specs=pl.BlockSpec((L, v_w), lambda b, c: (row(b, c), 0)),
        out_shape=jax.ShapeDtypeStruct((t, v_w), BF16),
        scratch_shapes=[
            pltpu.VMEM((n_v, GDN_DK, GDN_DV), F32),
            pltpu.VMEM((8 + L + 8, conv_ch), F32),
            pltpu.VMEM((L, conv_ch), F32),
            pltpu.VMEM((n_qk, L, GDN_DK), F32),
            pltpu.VMEM((n_qk, L, GDN_DK), F32),
            pltpu.VMEM((n_qk, L, L), F32),
            pltpu.VMEM((n_qk, L, L), F32),
            pltpu.VMEM((n_v, L, L), F32),
            pltpu.VMEM((n_v, L, L), F32),
            pltpu.VMEM((n_v, L, GDN_DV + GDN_DK), BF16),
            pltpu.VMEM((n_v, L, L), BF16),
            pltpu.VMEM((n_v, L, GDN_DK), BF16),
            pltpu.VMEM((n_v, L, GDN_DK), F32),
            pltpu.VMEM((n_v, 1, LANES), F32),
            pltpu.VMEM((n_v, L, GDN_DV + GDN_DK), F32),
            pltpu.VMEM((n_v, L, GDN_DV), BF16),
        ],
        compiler_params=_params("arbitrary", "arbitrary"),
        name="gdn",
    )(proj, proj, small, small_t, conv_w, al_c, al_r, dt_c, dt_r, norm_g)


def _merge_kernel(hm_ref, og_ref, gm_ref, gg_ref, h0_ref, wbm_ref, wbg_ref, wo_ref, g_ref, b_ref, out_ref):
    y_ml = _dot(hm_ref[...], wbm_ref[...])
    y_gdn = _dot(og_ref[...], wbg_ref[...])
    merged = _sigmoid(gm_ref[...].astype(F32)) * y_ml + _sigmoid(gg_ref[...].astype(F32)) * y_gdn
    mix = _dot(merged.astype(BF16), wo_ref[...])
    out_ref[...] = _layer_norm(DN_ALPHA * h0_ref[...] + mix, g_ref[...], b_ref[...])


def _merge(hm, og, proj, h0, w_bm, w_bg, w_out, g, b, cols, tm):
    t, d = h0.shape
    full = lambda a: pl.BlockSpec(a.shape, lambda i: (0,) * a.ndim)
    return pl.pallas_call(
        _merge_kernel,
        grid=(t // tm,),
        in_specs=[
            pl.BlockSpec((tm, hm.shape[1]), lambda i: (i, 0)),
            pl.BlockSpec((tm, og.shape[1]), lambda i: (i, 0)),
            pl.BlockSpec((tm, d), lambda i: (i, cols["gate_ml"] // d)),
            pl.BlockSpec((tm, d), lambda i: (i, cols["gate_gdn"] // d)),
            pl.BlockSpec((tm, d), lambda i: (i, 0)),
            full(w_bm), full(w_bg), full(w_out), full(g), full(b),
        ],
        out_specs=pl.BlockSpec((tm, d), lambda i: (i, 0)),
        out_shape=jax.ShapeDtypeStruct((t, d), F32),
        compiler_params=_params("arbitrary"),
        name="merge_out_ln1",
    )(hm, og, proj, proj, h0, w_bm, w_bg, w_out, g, b)


def _kv_kernel(m_ref, wk_ref, wv_ref, k_ref, v_ref):
    mb = m_ref[...].astype(BF16)
    k_ref[...] = _dot(mb, wk_ref[...]).astype(BF16)
    v_ref[...] = _dot(mb, wv_ref[...]).astype(BF16)


def _kv_proj(mem2, wk, wv, tm):
    t, d = mem2.shape
    full = lambda a: pl.BlockSpec(a.shape, lambda i: (0,) * a.ndim)
    return pl.pallas_call(
        _kv_kernel,
        grid=(t // tm,),
        in_specs=[pl.BlockSpec((tm, d), lambda i: (i, 0)), full(wk), full(wv)],
        out_specs=[pl.BlockSpec((tm, d), lambda i: (i, 0))] * 2,
        out_shape=[jax.ShapeDtypeStruct((t, d), BF16)] * 2,
        compiler_params=_params("arbitrary"),
        name="mem_kv_proj",
    )(mem2, wk, wv)


def _xattn_kernel(h1_ref, k_ref, v_ref, wq_ref, wo_ref, g_ref, b_ref, wr_ref, br_ref,
                  h2_ref, gw_ref, route_ref, cnt_ref, carry_ref, *, dh):
    tm = h1_ref.shape[0]

    @pl.when((pl.program_id(0) == 0) & (pl.program_id(1) == 0))
    def _():
        carry_ref[...] = jnp.zeros_like(carry_ref)

    h1 = h1_ref[...]
    q = _dot(h1.astype(BF16), wq_ref[...])
    outs = []
    for hd in range(XA_HEADS):
        qh = q[:, hd * dh:(hd + 1) * dh].astype(BF16)
        kh = k_ref[:, hd * dh:(hd + 1) * dh]
        vh = v_ref[:, hd * dh:(hd + 1) * dh]
        sc = _dot_nt(qh, kh) * (dh ** -0.5)
        e = jnp.exp(sc - jnp.max(sc, axis=1, keepdims=True))
        p = e / jnp.sum(e, axis=1, keepdims=True)
        outs.append(_dot(p.astype(BF16), vh))
    o = jnp.concatenate(outs, axis=1)
    xa = _dot(o.astype(BF16), wo_ref[...])
    h2 = _layer_norm(DN_ALPHA * h1 + xa, g_ref[...], b_ref[...])
    h2_ref[...] = h2

    logits = _dot(h2.astype(BF16), wr_ref[...]) + br_ref[...]
    lane = lax.broadcasted_iota(jnp.int32, (tm, LANES), 1)
    lane_f = lane.astype(F32)
    work = logits
    vals, idxs = [], []
    for _ in range(TOP_K):
        m = jnp.max(work, axis=1, keepdims=True)
        idx = jnp.min(jnp.where(work == m, lane_f, float(LANES)), axis=1, keepdims=True)
        vals.append(m)
        idxs.append(idx)
        work = jnp.where(lane_f == idx, -jnp.inf, work)
    es = [jnp.exp(v - vals[0]) for v in vals]
    tot = es[0]
    for e_ in es[1:]:
        tot = tot + e_
    onehot = jnp.zeros((tm, LANES), F32)
    for idx in idxs:
        onehot = onehot + (lane_f == idx).astype(F32)
    ri = lax.broadcasted_iota(jnp.int32, (tm, tm), 0)
    ci = lax.broadcasted_iota(jnp.int32, (tm, tm), 1)
    tri = (ci < ri).astype(BF16)
    carry = carry_ref[0:1, :]
    ranks = carry + _dot(tri, onehot.astype(BF16))
    gw = jnp.zeros((tm, LANES), F32)
    route = jnp.zeros((tm, LANES), F32)
    for k_ in range(TOP_K):
        rk = jnp.sum(jnp.where(lane_f == idxs[k_], ranks, 0.0), axis=1, keepdims=True)
        gw = gw + jnp.where(lane == k_, es[k_] / tot, 0.0)
        route = route + jnp.where(lane == k_, idxs[k_], 0.0) + jnp.where(lane == TOP_K + k_, rk, 0.0)
    gw_ref[...] = gw
    route_ref[...] = route.astype(jnp.int32)
    carry = carry + jnp.sum(onehot, axis=0, keepdims=True)
    carry_ref[...] = jnp.broadcast_to(carry, carry_ref.shape)
    cnt_ref[...] = jnp.broadcast_to(carry, cnt_ref.shape)


def _xattn(h1, kmem, vmem, wq, wo, g, b, w_r, b_r, bsz, seq, mem_len, tm):
    t, d = h1.shape
    nt = seq // tm
    full = lambda a: pl.BlockSpec(a.shape, lambda i, j: (0,) * a.ndim)
    kern = functools.partial(_xattn_kernel, dh=d // XA_HEADS)
    return pl.pallas_call(
        kern,
        grid=(bsz, nt),
        in_specs=[
            pl.BlockSpec((tm, d), lambda i, j: (i * nt + j, 0)),
            pl.BlockSpec((mem_len, d), lambda i, j: (i, 0)),
            pl.BlockSpec((mem_len, d), lambda i, j: (i, 0)),
            full(wq), full(wo), full(g), full(b), full(w_r), full(b_r),
        ],
        out_specs=[
            pl.BlockSpec((tm, d), lambda i, j: (i * nt + j, 0)),
            pl.BlockSpec((tm, LANES), lambda i, j: (i * nt + j, 0)),
            pl.BlockSpec((tm, LANES), lambda i, j: (i * nt + j, 0)),
            pl.BlockSpec((8, LANES), lambda i, j: (0, 0)),
        ],
        out_shape=[
            jax.ShapeDtypeStruct((t, d), F32),
            jax.ShapeDtypeStruct((t, LANES), F32),
            jax.ShapeDtypeStruct((t, LANES), jnp.int32),
            jax.ShapeDtypeStruct((8, LANES), F32),
        ],
        scratch_shapes=[pltpu.VMEM((8, LANES), F32)],
        compiler_params=_params("arbitrary", "arbitrary"),
        name="xattn_ln2_router",
    )(h1, kmem, vmem, wq, wo, g, b, w_r, b_r)


def _dispatch_kernel(pad_lo_ref, pad_n_ref, na_ref, dest_ref, h_ref, xs_ref, zero_ref, sem, zsem):
    tm = h_ref.shape[0]
    n_blocks = xs_ref.shape[0] // MOE_BLOCK

    @pl.when(pl.program_id(0) == 0)
    def _():
        zero_ref[...] = jnp.zeros_like(zero_ref)
        zero_row = zero_ref.at[pl.ds(0, 1), :]

        def per_expert(e, carry):
            lo = pad_lo_ref[e]
            n = pad_n_ref[e]

            def start(r, c):
                pltpu.make_async_copy(zero_row, xs_ref.at[pl.ds(lo + r, 1), :], zsem).start()
                return c

            def wait(r, c):
                pltpu.make_async_copy(zero_row, xs_ref.at[pl.ds(0, 1), :], zsem).wait()
                return c

            lax.fori_loop(0, n, start, 0)
            lax.fori_loop(0, n, wait, 0)
            return carry

        lax.fori_loop(0, N_EXPERTS, per_expert, 0)

        def tail(b, carry):
            cp = pltpu.make_async_copy(zero_ref, xs_ref.at[pl.ds(b * MOE_BLOCK, MOE_BLOCK), :], zsem)
            cp.start()
            cp.wait()
            return carry

        lax.fori_loop(na_ref[0], n_blocks, tail, 0)

    def issue(tok, carry):
        src = h_ref.at[pl.ds(tok, 1), :]
        for k_ in range(TOP_K):
            pltpu.make_async_copy(src, xs_ref.at[pl.ds(dest_ref[tok * TOP_K + k_], 1), :], sem).start()
        return carry

    lax.fori_loop(0, tm, issue, 0, unroll=ISSUE_UNROLL)
    for _ in range(TOP_K):
        pltpu.make_async_copy(h_ref, xs_ref.at[pl.ds(0, tm), :], sem).wait()


def _dispatch(pad_lo, pad_n, nact, dest_flat, h2, n_rows, tm):
    t, d = h2.shape
    grid_spec = pltpu.PrefetchScalarGridSpec(
        num_scalar_prefetch=3,
        grid=(t // tm,),
        in_specs=[
            pl.BlockSpec((tm * TOP_K,), lambda i, *_: (i,), memory_space=pltpu.SMEM),
            pl.BlockSpec((tm, d), lambda i, *_: (i, 0)),
        ],
        out_specs=pl.BlockSpec(memory_space=pl.ANY),
        scratch_shapes=[pltpu.VMEM((MOE_BLOCK, d), F32), pltpu.SemaphoreType.DMA(()),
                        pltpu.SemaphoreType.DMA(())],
    )
    return pl.pallas_call(
        _dispatch_kernel,
        grid_spec=grid_spec,
        out_shape=jax.ShapeDtypeStruct((n_rows, d), F32),
        compiler_params=_params("arbitrary"),
        name="moe_dispatch",
    )(pad_lo, pad_n, nact, dest_flat, h2)


def _expert_kernel(be_ref, na_ref, x_ref, wgu_ref, bgu_ref, wdn_ref, bdn_ref, y_ref, *, d_exp):
    blk = pl.program_id(0)

    @pl.when(blk >= na_ref[0])
    def _():
        y_ref[...] = jnp.zeros_like(y_ref)

    @pl.when(blk < na_ref[0])
    def _():
        gu = _dot(x_ref[...].astype(BF16), wgu_ref[0]) + bgu_ref[0]
        gate = jnp.minimum(gu[:, :d_exp], SWIGLU_LIMIT)
        up = jnp.clip(gu[:, d_exp:], -SWIGLU_LIMIT, SWIGLU_LIMIT)
        act = (up + 1.0) * (gate * _sigmoid(SWIGLU_ALPHA * gate))
        y_ref[...] = _dot(act.astype(BF16), wdn_ref[0]) + bdn_ref[0]


def _experts(block_e, nact, xs, w_gu, b_gu, w_dn, b_dn):
    n_rows, d = xs.shape
    n_blocks = n_rows // MOE_BLOCK
    d_exp = w_dn.shape[1]
    kern = functools.partial(_expert_kernel, d_exp=d_exp)
    emap = lambda i, be, na: (be[i], 0, 0)
    grid_spec = pltpu.PrefetchScalarGridSpec(
        num_scalar_prefetch=2,
        grid=(n_blocks,),
        in_specs=[
            pl.BlockSpec((MOE_BLOCK, d), lambda i, be, na: (i, 0)),
            pl.BlockSpec((1, d, 2 * d_exp), emap),
            pl.BlockSpec((1, 1, 2 * d_exp), emap),
            pl.BlockSpec((1, d_exp, d), emap),
            pl.BlockSpec((1, 1, d), emap),
        ],
        out_specs=pl.BlockSpec((MOE_BLOCK, d), lambda i, be, na: (i, 0)),
    )
    return pl.pallas_call(
        kern,
        grid_spec=grid_spec,
        out_shape=jax.ShapeDtypeStruct((n_rows, d), F32),
        compiler_params=_params("arbitrary"),
        name="moe_experts",
    )(block_e, nact, xs, w_gu, b_gu, w_dn, b_dn)


def _combine_kernel(dest_ref, gw_ref, h_ref, g_ref, b_ref, y_ref, out_ref, buf_ref, sem):
    tm = h_ref.shape[0]

    def issue(tok, carry):
        for k_ in range(TOP_K):
            pltpu.make_async_copy(y_ref.at[pl.ds(dest_ref[tok * TOP_K + k_], 1), :],
                                  buf_ref.at[k_, pl.ds(tok, 1), :], sem).start()
        return carry

    lax.fori_loop(0, tm, issue, 0, unroll=ISSUE_UNROLL)
    for k_ in range(TOP_K):
        pltpu.make_async_copy(y_ref.at[pl.ds(0, tm), :], buf_ref.at[k_], sem).wait()
    gw = gw_ref[...]
    ff = buf_ref[0] * gw[:, 0:1]
    for k_ in range(1, TOP_K):
        ff = ff + buf_ref[k_] * gw[:, k_:k_ + 1]
    out_ref[...] = _layer_norm(DN_ALPHA * h_ref[...] + ff, g_ref[...], b_ref[...])


def _combine(dest_flat, gw, h2, g, b, yb, tm):
    t, d = h2.shape
    return pl.pallas_call(
        _combine_kernel,
        grid=(t // tm,),
        in_specs=[
            pl.BlockSpec((tm * TOP_K,), lambda i: (i,), memory_space=pltpu.SMEM),
            pl.BlockSpec((tm, LANES), lambda i: (i, 0)),
            pl.BlockSpec((tm, d), lambda i: (i, 0)),
            pl.BlockSpec((1, d), lambda i: (0, 0)),
            pl.BlockSpec((1, d), lambda i: (0, 0)),
            pl.BlockSpec(memory_space=pl.ANY),
        ],
        out_specs=pl.BlockSpec((tm, d), lambda i: (i, 0)),
        out_shape=jax.ShapeDtypeStruct((t, d), F32),
        scratch_shapes=[pltpu.VMEM((TOP_K, tm, d), F32), pltpu.SemaphoreType.DMA(())],
        compiler_params=_params("arbitrary"),
        name="moe_combine_ln3",
    )(dest_flat, gw, h2, g, b, yb)


def _pick(n, pref):
    return pref if n % pref == 0 else n


def kernel(x, mem, ln_in_g, ln_in_b, w_in, ml_gate_bias, ml_norm_g, gdn_conv_w, gdn_a_log, gdn_dt_bias, gdn_norm_g, w_branch_ml, w_branch_gdn, w_mix_out, ln1_g, ln1_b, xa_wq, xa_wk, xa_wv, xa_wo, ln2_g, ln2_b, w_router, b_router, w_gu, b_gu, w_dn, b_dn, ln3_g, ln3_b):
    bsz, seq, d = x.shape
    mem_len = mem.shape[1]
    t = bsz * seq
    ml_dv = d // ML_HEADS
    ml_dqk = ml_dv // 2
    ml_qk_w, ml_v_w = ML_HEADS * ml_dqk, ML_HEADS * ml_dv
    n_qk = d // GDN_DK
    n_v = 2 * n_qk
    gdn_qk_w, gdn_v_w = n_qk * GDN_DK, n_v * GDN_DV
    conv_ch = 2 * gdn_qk_w + gdn_v_w
    splits = (ml_qk_w, ml_qk_w, ml_v_w, ml_v_w, 2 * ML_HEADS, conv_ch, gdn_v_w, n_v, n_v, d, d)
    names = ("mq", "mk", "mv", "mo", "mif", "gqkv", "gz", "ga", "gb", "gate_ml", "gate_gdn")
    starts = {}
    acc = 0
    for nm, sz in zip(names, splits):
        starts[nm] = (acc, sz)
        acc += sz
    row2 = lambda a: a.reshape(1, -1).astype(F32)

    h = x.reshape(t, d)
    for l in range(DEPTH):
        w = w_in[l]
        seg = lambda nm: w[:, starts[nm][0]:starts[nm][0] + starts[nm][1]]
        order = ("gqkv", "gz", "mv", "mo", "gate_ml", "gate_gdn", "mq", "mk")
        cols = {}
        off = 0
        for nm in order:
            assert off % starts[nm][1] == 0
            cols[nm] = off
            off += starts[nm][1]
        w_big = jnp.concatenate([seg(nm) for nm in order], axis=1).astype(BF16)
        n_small = 2 * ML_HEADS + 2 * n_v
        w_small = jnp.concatenate([seg("mif"), seg("ga"), seg("gb"),
                                   jnp.zeros((d, LANES - n_small), F32)], axis=1).astype(BF16)
        ga_off, gb_off = 2 * ML_HEADS, 2 * ML_HEADS + n_v

        if l == 0:
            h0, proj, small = _ln_proj(h, row2(ln_in_g), row2(ln_in_b), w_big, w_small,
                                       _pick(t, 1024), _pick(off, 1024))
        else:
            raise NotImplementedError("DEPTH > 1")
        n_ch = seq // CHUNK
        small_t = small[:, :n_small].reshape(bsz * n_ch, CHUNK, n_small).transpose(0, 2, 1)

        bias = ml_gate_bias[l].astype(F32)
        hm = _mlstm(proj, small, small_t, bias.reshape(1, -1), bias.reshape(-1, 1), row2(ml_norm_g[l]),
                    bsz, seq, cols, ml_dqk, ml_dv)
        al = gdn_a_log[l].astype(F32)
        dt = gdn_dt_bias[l].astype(F32)
        og = _gdn(proj, small, small_t, gdn_conv_w[l].astype(F32), al.reshape(1, -1), al.reshape(-1, 1),
                  dt.reshape(1, -1), dt.reshape(-1, 1), row2(gdn_norm_g[l]), bsz, seq, cols, n_qk, n_v,
                  ga_off, gb_off)
        h1 = _merge(hm, og, proj, h0, w_branch_ml[l].astype(BF16), w_branch_gdn[l].astype(BF16),
                    w_mix_out[l].astype(BF16), row2(ln1_g[l]), row2(ln1_b[l]), cols, _pick(t, 512))

        kmem, vmem = _kv_proj(mem.reshape(bsz * mem_len, d), xa_wk[l].astype(BF16), xa_wv[l].astype(BF16),
                              _pick(bsz * mem_len, 512))
        w_r = jnp.concatenate([w_router[l], jnp.zeros((d, LANES - N_EXPERTS), F32)], axis=1).astype(BF16)
        b_r = jnp.concatenate([b_router[l].astype(F32), jnp.full((LANES - N_EXPERTS,), NEG_BIG, F32)]).reshape(1, -1)
        tm_x = _pick(seq, 256)
        h2, gw, route, cnt = _xattn(h1, kmem, vmem, xa_wq[l].astype(BF16), xa_wo[l].astype(BF16),
                                    row2(ln2_g[l]), row2(ln2_b[l]), w_r, b_r, bsz, seq, mem_len, tm_x)

        counts = cnt[0, :N_EXPERTS].astype(jnp.int32)
        padded = (counts + MOE_BLOCK - 1) // MOE_BLOCK * MOE_BLOCK
        pad_end = jnp.cumsum(padded)
        pad_start = pad_end - padded
        n_asg = t * TOP_K
        n_blocks = -(-n_asg // MOE_BLOCK) + N_EXPERTS
        n_rows = n_blocks * MOE_BLOCK
        top_e = route[:, :TOP_K]
        rank = route[:, TOP_K:2 * TOP_K]
        sel = top_e[:, :, None] == jnp.arange(N_EXPERTS, dtype=jnp.int32)[None, None, :]
        dest = (jnp.sum(jnp.where(sel, pad_start[None, None, :], 0), axis=-1) + rank).astype(jnp.int32)
        dest_flat = dest.reshape(n_asg)
        blk_row = jnp.arange(n_blocks, dtype=jnp.int32) * MOE_BLOCK
        block_e = jnp.minimum(jnp.sum(pad_end[None, :] <= blk_row[:, None], axis=1), N_EXPERTS - 1).astype(jnp.int32)
        nact = (pad_end[-1:] // MOE_BLOCK).astype(jnp.int32)
        pad_lo = (pad_start + counts).astype(jnp.int32)
        pad_n = (padded - counts).astype(jnp.int32)

        tm_d = _pick(t, 256)
        xs = _dispatch(pad_lo, pad_n, nact, dest_flat, h2, n_rows, tm_d)
        yb = _experts(block_e, nact, xs, w_gu[l].astype(BF16), b_gu[l].astype(F32)[:, None, :],
                      w_dn[l].astype(BF16), b_dn[l].astype(F32)[:, None, :])
        h = _combine(dest_flat, gw, h2, row2(ln3_g[l]), row2(ln3_b[l]), yb, tm_d)
    return h.reshape(bsz, seq, d)
```

```python
import functools

import jax
import jax.numpy as jnp
from jax import lax
from jax.experimental import pallas as pl
from jax.experimental.pallas import tpu as pltpu

F32 = jnp.float32
BF16 = jnp.bfloat16

CHUNK = 64
ML_HEADS = 4
GDN_DK = 128
GDN_DV = 128
CONV_K = 4
XA_HEADS = 4
N_EXPERTS = 32
TOP_K = 4
SWIGLU_LIMIT = 7.0
SWIGLU_ALPHA = 1.702
MOE_BLOCK = 256
DEPTH = 1
DN_ALPHA = (2 * DEPTH) ** 0.25
LN_EPS = 1e-5
RMS_EPS = 1e-6
LANES = 128
VMEM_LIMIT = 56 * 1024 * 1024
NEG_BIG = -1e30
ISSUE_UNROLL = 4
ML_CHUNKS_PER_STEP = 4
GDN_CHUNKS_PER_STEP = 2
GDN_CONV_COLS = 512


def _params(*sem):
    return pltpu.CompilerParams(dimension_semantics=sem, vmem_limit_bytes=VMEM_LIMIT)


def _dot(a, b):
    return jnp.dot(a, b, preferred_element_type=F32)


def _dot_nt(a, b):
    return lax.dot_general(a, b, (((1,), (1,)), ((), ())), preferred_element_type=F32)


def _layer_norm(x, g, b):
    mu = jnp.mean(x, axis=-1, keepdims=True)
    xc = x - mu
    var = jnp.mean(xc * xc, axis=-1, keepdims=True)
    return xc * lax.rsqrt(var + LN_EPS) * g + b


def _sigmoid(x):
    return 1.0 / (1.0 + jnp.exp(-x))


def _log_sigmoid(x):
    return jnp.minimum(x, 0.0) - jnp.log(1.0 + jnp.exp(-jnp.abs(x)))


def _softplus(x):
    return jnp.maximum(x, 0.0) + jnp.log(1.0 + jnp.exp(-jnp.abs(x)))


def _split3(x):
    hi = x.astype(BF16)
    r = x - hi.astype(F32)
    mid = r.astype(BF16)
    lo = (r - mid.astype(F32)).astype(BF16)
    return hi, mid, lo


def _cumsum_lanes(x, upper):
    hi, mid, lo = _split3(x)
    return _dot(hi, upper) + _dot(mid, upper) + _dot(lo, upper)


def _ln_proj_kernel(x_ref, g_ref, b_ref, w_ref, ws_ref, h_ref, p_ref, s_ref, xn_ref):
    @pl.when(pl.program_id(1) == 0)
    def _():
        h = _layer_norm(x_ref[...], g_ref[...], b_ref[...])
        h_ref[...] = h
        hb = h.astype(BF16)
        xn_ref[...] = hb
        s_ref[...] = _dot(hb, ws_ref[...])

    p_ref[...] = _dot(xn_ref[...], w_ref[...]).astype(BF16)


def _ln_proj(x2, g, b, w_big, w_small, tm, tn):
    t, d = x2.shape
    nw = w_big.shape[1]
    return pl.pallas_call(
        _ln_proj_kernel,
        grid=(t // tm, nw // tn),
        in_specs=[
            pl.BlockSpec((tm, d), lambda i, j: (i, 0)),
            pl.BlockSpec((1, d), lambda i, j: (0, 0)),
            pl.BlockSpec((1, d), lambda i, j: (0, 0)),
            pl.BlockSpec((d, tn), lambda i, j: (0, j)),
            pl.BlockSpec((d, LANES), lambda i, j: (0, 0)),
        ],
        out_specs=[
            pl.BlockSpec((tm, d), lambda i, j: (i, 0)),
            pl.BlockSpec((tm, tn), lambda i, j: (i, j)),
            pl.BlockSpec((tm, LANES), lambda i, j: (i, 0)),
        ],
        out_shape=[
            jax.ShapeDtypeStruct((t, d), F32),
            jax.ShapeDtypeStruct((t, nw), BF16),
            jax.ShapeDtypeStruct((t, LANES), F32),
        ],
        scratch_shapes=[pltpu.VMEM((tm, d), BF16)],
        compiler_params=_params("arbitrary", "arbitrary"),
        name="ln_in_proj",
    )(x2, g, b, w_big, w_small)


def _mlstm_kernel(q_ref, k_ref, v_ref, o_ref, gr_ref, br_ref, ng_ref,
                  out_ref, c_ref, m_ref, sqk_ref, kt_ref, p_ref, ktw_ref, b3_ref, li3_ref, kw3_ref,
                  gs_ref, mc_ref, stb_ref, stm_ref, std_ref, num_ref, kv_ref, *, dqk, dv, n_ch):
    L = CHUNK
    H = ML_HEADS

    @pl.when(pl.program_id(1) == 0)
    def _():
        c_ref[...] = jnp.zeros_like(c_ref)
        m_ref[...] = jnp.zeros_like(m_ref)

    ii = lax.broadcasted_iota(jnp.int32, (L, L), 0)
    jj = lax.broadcasted_iota(jnp.int32, (L, L), 1)
    causal = jj <= ii
    diag = ii == jj
    upper = (ii <= jj).astype(BF16)
    scale = dqk ** -0.5
    heads = [(g, h) for g in range(n_ch) for h in range(H)]

    for g, h in heads:
        rows = slice(g * L, (g + 1) * L)
        kf = k_ref[rows, h * dqk:(h + 1) * dqk].astype(F32) * scale
        sqk_ref[g * H + h] = _dot_nt(q_ref[rows, h * dqk:(h + 1) * dqk], kf.astype(BF16))
        kt_ref[g * H + h] = kf.T

    pre = gr_ref[:, 0:2 * H, :].reshape(n_ch * 2 * H, L) + jnp.concatenate([br_ref[...]] * n_ch, axis=0)
    b_all = _cumsum_lanes(_log_sigmoid(pre), upper)
    gs_all = b_all[:, L - 1:L]
    a_all = gs_all - b_all + pltpu.roll(pre, H, axis=0)
    mc_all = jnp.max(a_all, axis=1, keepdims=True)
    kw_all = jnp.exp(a_all - mc_all)
    gs_ref[...] = jnp.broadcast_to(gs_all, gs_ref.shape)
    mc_ref[...] = jnp.broadcast_to(mc_all, mc_ref.shape)
    for g, h in heads:
        r = g * 2 * H + H + h
        b3_ref[g * H + h] = b_all[r:r + 1, :]
        li3_ref[g * H + h] = pre[r - H:r - H + 1, :]
        kw3_ref[g * H + h] = kw_all[r:r + 1, :]

    b_r = b3_ref[...]
    b_c = jnp.sum(jnp.where(diag, b_r, 0.0), axis=2, keepdims=True)
    dmat = jnp.where(causal, b_c - b_r + li3_ref[...], -jnp.inf)
    m_intra = jnp.max(dmat, axis=2, keepdims=True)
    p = jnp.exp(dmat - m_intra) * sqk_ref[...]
    p_ref[...] = p.astype(BF16)
    stb_ref[...] = jnp.broadcast_to(b_c, stb_ref.shape)
    stm_ref[...] = jnp.broadcast_to(m_intra, stm_ref.shape)
    std_ref[...] = jnp.broadcast_to(jnp.sum(p, axis=2, keepdims=True), std_ref.shape)
    ktw_ref[...] = (kt_ref[...] * kw3_ref[...]).astype(BF16)

    ones = jnp.ones((L, LANES), BF16)
    for g, h in heads:
        vh = v_ref[g * L:(g + 1) * L, h * dv:(h + 1) * dv]
        num_ref[g * H + h] = _dot(p_ref[g * H + h], vh)
        kv_ref[g * H + h] = _dot(ktw_ref[g * H + h], jnp.concatenate([vh, ones], axis=1))

    rep = dv // LANES
    wide = lambda s: jnp.concatenate([s] * rep, axis=1)
    for g, h in heads:
        i = g * H + h
        r = g * 2 * H + H + h
        rows = slice(g * L, (g + 1) * L)
        c_st = c_ref[h]
        m_st = m_ref[h:h + 1, :]
        qc = _dot(q_ref[rows, h * dqk:(h + 1) * dqk], c_st.astype(BF16))
        m_intra = stm_ref[i]
        inter_log = stb_ref[i] + m_st
        m_out = jnp.maximum(inter_log, m_intra)
        s_inter = jnp.exp(inter_log - m_out)
        s_intra = jnp.exp(m_intra - m_out)
        num = wide(s_inter) * qc[:, 0:dv] + wide(s_intra) * num_ref[i]
        den = s_inter * qc[:, dv:dv + LANES] + s_intra * std_ref[i]
        hh = num / wide(jnp.maximum(jnp.abs(den), jnp.exp(-m_out)))
        gs = gs_ref[r:r + 1, :]
        mc = mc_ref[r:r + 1, :]
        m_new = jnp.maximum(gs + m_st, mc)
        dec = jnp.exp(gs + m_st - m_new)
        s_new = jnp.exp(mc - m_new)
        c_ref[h] = (jnp.concatenate([dec] * (rep + 1), axis=1) * c_st
                    + jnp.concatenate([s_new] * (rep + 1), axis=1) * kv_ref[i])
        m_ref[h:h + 1, :] = m_new

        rms = lax.rsqrt(jnp.mean(hh * hh, axis=1, keepdims=True) + RMS_EPS)
        og = _sigmoid(o_ref[rows, h * dv:(h + 1) * dv].astype(F32))
        out_ref[rows, h * dv:(h + 1) * dv] = (hh * rms * ng_ref[:, h * dv:(h + 1) * dv] * og).astype(BF16)


def _mlstm(proj, small_t, bias_r, norm_g, bsz, seq, cols, dqk, dv, n_ch):
    n = seq // (CHUNK * n_ch)
    L = CHUNK
    R = n_ch * L
    H = ML_HEADS
    qk_w, v_w = H * dqk, H * dv
    t = bsz * seq
    nh = n_ch * H
    kern = functools.partial(_mlstm_kernel, dqk=dqk, dv=dv, n_ch=n_ch)
    row = lambda b, c: b * n + c
    return pl.pallas_call(
        kern,
        grid=(bsz, n),
        in_specs=[
            pl.BlockSpec((R, qk_w), lambda b, c: (row(b, c), cols["mq"] // qk_w)),
            pl.BlockSpec((R, qk_w), lambda b, c: (row(b, c), cols["mk"] // qk_w)),
            pl.BlockSpec((R, v_w), lambda b, c: (row(b, c), cols["mv"] // v_w)),
            pl.BlockSpec((R, v_w), lambda b, c: (row(b, c), cols["mo"] // v_w)),
            pl.BlockSpec((n_ch, small_t.shape[1], L), lambda b, c: (row(b, c), 0, 0)),
            pl.BlockSpec(bias_r.shape, lambda b, c: (0, 0)),
            pl.BlockSpec((1, v_w), lambda b, c: (0, 0)),
        ],
        out_specs=pl.BlockSpec((R, v_w), lambda b, c: (row(b, c), 0)),
        out_shape=jax.ShapeDtypeStruct((t, v_w), BF16),
        scratch_shapes=[
            pltpu.VMEM((H, dqk, dv + LANES), F32),
            pltpu.VMEM((8, LANES), F32),
            pltpu.VMEM((nh, L, L), F32),
            pltpu.VMEM((nh, dqk, L), F32),
            pltpu.VMEM((nh, L, L), BF16),
            pltpu.VMEM((nh, dqk, L), BF16),
            pltpu.VMEM((nh, 1, L), F32),
            pltpu.VMEM((nh, 1, L), F32),
            pltpu.VMEM((nh, 1, L), F32),
            pltpu.VMEM((n_ch * 2 * H, LANES), F32),
            pltpu.VMEM((n_ch * 2 * H, LANES), F32),
            pltpu.VMEM((nh, L, LANES), F32),
            pltpu.VMEM((nh, L, LANES), F32),
            pltpu.VMEM((nh, L, LANES), F32),
            pltpu.VMEM((nh, L, dv), F32),
            pltpu.VMEM((nh, dqk, dv + LANES), F32),
        ],
        compiler_params=_params("arbitrary", "arbitrary"),
        name="mlstm",
    )(proj, proj, proj, proj, small_t, bias_r, norm_g)


def _gdn_kernel(x_ref, z_ref, gr_ref, cw_ref, al_ref, dt_ref, ng_ref,
                out_ref, s_ref, xs_ref, c_ref, qn_ref, kn_ref, knt_ref, kk_ref, qk_ref, pw_ref, x_ref_, rhs_ref,
                attn_ref, qd_ref, kdt_ref, gam3_ref, beta3_ref, kdwb3_ref, gt_ref, sol_ref, vn_ref,
                *, n_qk, n_v, ga_off, gb_off, n_ch):
    L = CHUNK
    R = n_ch * L
    dk, dv = GDN_DK, GDN_DV
    qk_w = n_qk * dk
    rep = n_v // n_qk
    conv_ch = 2 * qk_w + n_v * dv

    @pl.when(pl.program_id(1) == 0)
    def _():
        s_ref[...] = jnp.zeros_like(s_ref)
        xs_ref[0:8, :] = jnp.zeros((8, xs_ref.shape[1]), F32)

    xs_ref[8:8 + R, :] = x_ref[...].astype(F32)
    for cb in range(0, conv_ch, GDN_CONV_COLS):
        cs = slice(cb, cb + GDN_CONV_COLS)
        conv = cw_ref[0:1, cs] * xs_ref[8 - (CONV_K - 1):8 - (CONV_K - 1) + R, cs]
        for j in range(1, CONV_K):
            off = 8 - (CONV_K - 1) + j
            conv = conv + cw_ref[j:j + 1, cs] * xs_ref[off:off + R, cs]
        c_ref[:, cs] = conv * _sigmoid(conv)
    xs_ref[0:8, :] = xs_ref[R:R + 8, :]

    ii = lax.broadcasted_iota(jnp.int32, (L, L), 0)
    jj = lax.broadcasted_iota(jnp.int32, (L, L), 1)
    incl = jj <= ii
    strict = jj < ii
    diag = ii == jj
    eye = diag.astype(F32)
    upper = (ii <= jj).astype(BF16)

    for g in range(n_ch):
        rows = slice(g * L, (g + 1) * L)
        for hk in range(n_qk):
            iq = g * n_qk + hk
            cq = c_ref[rows, hk * dk:(hk + 1) * dk]
            ck = c_ref[rows, qk_w + hk * dk:qk_w + (hk + 1) * dk]
            qn = cq * lax.rsqrt(jnp.sum(cq * cq, axis=1, keepdims=True) + RMS_EPS) * (dk ** -0.5)
            kn = ck * lax.rsqrt(jnp.sum(ck * ck, axis=1, keepdims=True) + RMS_EPS)
            qn_ref[iq] = qn
            kn_ref[iq] = kn
            knt_ref[iq] = kn.T
            kb = kn.astype(BF16)
            kk_ref[iq] = _dot_nt(kb, kb)
            qk_ref[iq] = _dot_nt(qn.astype(BF16), kb)

    for g in range(n_ch):
        gd = -jnp.exp(al_ref[...]) * _softplus(gr_ref[g, ga_off:ga_off + n_v, :] + dt_ref[...])
        gam = _cumsum_lanes(gd, upper)
        beta = _sigmoid(gr_ref[g, gb_off:gb_off + n_v, :])
        g_tot = gam[:, L - 1:L]
        kdwb = jnp.exp(g_tot - gam) * beta
        gt_ref[g * n_v:(g + 1) * n_v, :] = jnp.broadcast_to(jnp.exp(g_tot), (n_v, LANES))
        for h in range(n_v):
            gam3_ref[g * n_v + h] = gam[h:h + 1, :]
            beta3_ref[g * n_v + h] = beta[h:h + 1, :]
            kdwb3_ref[g * n_v + h] = kdwb[h:h + 1, :]

    for g in range(n_ch):
        sv = slice(g * n_v, (g + 1) * n_v)
        sq = slice(g * n_qk, (g + 1) * n_qk)
        rows = slice(g * L, (g + 1) * L)
        gam_r = gam3_ref[sv]
        beta_r = beta3_ref[sv]
        gam_c = jnp.sum(jnp.where(diag, gam_r, 0.0), axis=2, keepdims=True)
        decm = jnp.exp(jnp.where(incl, gam_c - gam_r, -jnp.inf))
        db = decm * beta_r
        a = jnp.where(strict, jnp.repeat(kk_ref[sq], rep, axis=0) * db, 0.0)
        pw_ref[sv] = a
        x_ref_[sv] = eye - a
        attn_ref[sv] = (jnp.repeat(qk_ref[sq], rep, axis=0) * db).astype(BF16)
        eg_c = jnp.exp(gam_c)
        rhs_ref[sv, :, dv:dv + dk] = (jnp.repeat(kn_ref[sq], rep, axis=0) * eg_c).astype(BF16)
        qd_ref[sv] = (jnp.repeat(qn_ref[sq], rep, axis=0) * eg_c).astype(BF16)
        kdt_ref[sv] = (jnp.repeat(knt_ref[sq], rep, axis=0) * kdwb3_ref[sv]).astype(BF16)
        for h in range(n_v):
            rhs_ref[g * n_v + h, :, 0:dv] = c_ref[rows, 2 * qk_w + h * dv:2 * qk_w + (h + 1) * dv].astype(BF16)

    n_sq = 5
    for lvl in range(n_sq + 1):
        for i in range(n_ch * n_v):
            pwb = pw_ref[i].astype(BF16)
            if lvl > 0:
                xv = x_ref_[i]
                x_ref_[i] = xv + _dot(xv.astype(BF16), pwb)
            if lvl < n_sq:
                pw_ref[i] = _dot(pwb, pwb)

    for i in range(n_ch * n_v):
        sol_ref[i] = _dot(x_ref_[i].astype(BF16), rhs_ref[i])

    for g in range(n_ch):
        rows = slice(g * L, (g + 1) * L)
        for h in range(n_v):
            sol = sol_ref[g * n_v + h]
            vn_ref[h] = (sol[:, 0:dv] - _dot(sol[:, dv:dv + dk].astype(BF16), s_ref[h].astype(BF16))).astype(BF16)
        for h in range(n_v):
            i = g * n_v + h
            s_st = s_ref[h]
            vnb = vn_ref[h]
            o = _dot(qd_ref[i], s_st.astype(BF16)) + _dot(attn_ref[i], vnb)
            s_ref[h] = gt_ref[i:i + 1, :] * s_st + _dot(kdt_ref[i], vnb)
            rms = lax.rsqrt(jnp.mean(o * o, axis=1, keepdims=True) + RMS_EPS)
            zz = z_ref[rows, h * dv:(h + 1) * dv].astype(F32)
            out_ref[rows, h * dv:(h + 1) * dv] = (o * rms * ng_ref[...] * (zz * _sigmoid(zz))).astype(BF16)


def _gdn(proj, small_t, conv_w, al, dt, norm_g, bsz, seq, cols, n_qk, n_v, ga_off, gb_off, n_ch):
    n = seq // (CHUNK * n_ch)
    L = CHUNK
    R = n_ch * L
    conv_ch = 2 * n_qk * GDN_DK + n_v * GDN_DV
    v_w = n_v * GDN_DV
    t = bsz * seq
    nq, nv = n_ch * n_qk, n_ch * n_v
    kern = functools.partial(_gdn_kernel, n_qk=n_qk, n_v=n_v, ga_off=ga_off, gb_off=gb_off, n_ch=n_ch)
    row = lambda b, c: b * n + c
    full = lambda a: pl.BlockSpec(a.shape, lambda b, c: (0,) * a.ndim)
    return pl.pallas_call(
        kern,
        grid=(bsz, n),
        in_specs=[
            pl.BlockSpec((R, conv_ch), lambda b, c: (row(b, c), cols["gqkv"] // conv_ch)),
            pl.BlockSpec((R, v_w), lambda b, c: (row(b, c), cols["gz"] // v_w)),
            pl.BlockSpec((n_ch, small_t.shape[1], L), lambda b, c: (row(b, c), 0, 0)),
            full(conv_w), full(al), full(dt), full(norm_g),
        ],
        out_specs=pl.BlockSpec((R, v_w), lambda b, c: (row(b, c), 0)),
        out_shape=jax.ShapeDtypeStruct((t, v_w), BF16),
        scratch_shapes=[
            pltpu.VMEM((n_v, GDN_DK, GDN_DV), F32),
            pltpu.VMEM((8 + R + 8, conv_ch), F32),
            pltpu.VMEM((R, conv_ch), F32),
            pltpu.VMEM((nq, L, GDN_DK), F32),
            pltpu.VMEM((nq, L, GDN_DK), F32),
            pltpu.VMEM((nq, GDN_DK, L), F32),
            pltpu.VMEM((nq, L, L), F32),
            pltpu.VMEM((nq, L, L), F32),
            pltpu.VMEM((nv, L, L), F32),
            pltpu.VMEM((nv, L, L), F32),
            pltpu.VMEM((nv, L, GDN_DV + GDN_DK), BF16),
            pltpu.VMEM((nv, L, L), BF16),
            pltpu.VMEM((nv, L, GDN_DK), BF16),
            pltpu.VMEM((nv, GDN_DK, L), BF16),
            pltpu.VMEM((nv, 1, L), F32),
            pltpu.VMEM((nv, 1, L), F32),
            pltpu.VMEM((nv, 1, L), F32),
            pltpu.VMEM((nv, LANES), F32),
            pltpu.VMEM((nv, L, GDN_DV + GDN_DK), F32),
            pltpu.VMEM((n_v, L, GDN_DV), BF16),
        ],
        compiler_params=_params("arbitrary", "arbitrary"),
        name="gdn",
    )(proj, proj, small_t, conv_w, al, dt, norm_g)


def _merge_kernel(hm_ref, og_ref, gm_ref, gg_ref, h0_ref, wbm_ref, wbg_ref, wo_ref, g_ref, b_ref, out_ref):
    y_ml = _dot(hm_ref[...], wbm_ref[...])
    y_gdn = _dot(og_ref[...], wbg_ref[...])
    merged = _sigmoid(gm_ref[...].astype(F32)) * y_ml + _sigmoid(gg_ref[...].astype(F32)) * y_gdn
    mix = _dot(merged.astype(BF16), wo_ref[...])
    out_ref[...] = _layer_norm(DN_ALPHA * h0_ref[...] + mix, g_ref[...], b_ref[...])


def _merge(hm, og, proj, h0, w_bm, w_bg, w_out, g, b, cols, tm):
    t, d = h0.shape
    full = lambda a: pl.BlockSpec(a.shape, lambda i: (0,) * a.ndim)
    return pl.pallas_call(
        _merge_kernel,
        grid=(t // tm,),
        in_specs=[
            pl.BlockSpec((tm, hm.shape[1]), lambda i: (i, 0)),
            pl.BlockSpec((tm, og.shape[1]), lambda i: (i, 0)),
            pl.BlockSpec((tm, d), lambda i: (i, cols["gate_ml"] // d)),
            pl.BlockSpec((tm, d), lambda i: (i, cols["gate_gdn"] // d)),
            pl.BlockSpec((tm, d), lambda i: (i, 0)),
            full(w_bm), full(w_bg), full(w_out), full(g), full(b),
        ],
        out_specs=pl.BlockSpec((tm, d), lambda i: (i, 0)),
        out_shape=jax.ShapeDtypeStruct((t, d), F32),
        compiler_params=_params("arbitrary"),
        name="merge_out_ln1",
    )(hm, og, proj, proj, h0, w_bm, w_bg, w_out, g, b)


def _kv_kernel(m_ref, wk_ref, wv_ref, k_ref, v_ref):
    mb = m_ref[...].astype(BF16)
    k_ref[...] = _dot(mb, wk_ref[...]).astype(BF16)
    v_ref[...] = _dot(mb, wv_ref[...]).astype(BF16)


def _kv_proj(mem2, wk, wv, tm):
    t, d = mem2.shape
    full = lambda a: pl.BlockSpec(a.shape, lambda i: (0,) * a.ndim)
    return pl.pallas_call(
        _kv_kernel,
        grid=(t // tm,),
        in_specs=[pl.BlockSpec((tm, d), lambda i: (i, 0)), full(wk), full(wv)],
        out_specs=[pl.BlockSpec((tm, d), lambda i: (i, 0))] * 2,
        out_shape=[jax.ShapeDtypeStruct((t, d), BF16)] * 2,
        compiler_params=_params("arbitrary"),
        name="mem_kv_proj",
    )(mem2, wk, wv)


def _xattn_kernel(h1_ref, k_ref, v_ref, wq_ref, wo_ref, g_ref, b_ref, wr_ref, br_ref,
                  h2_ref, gw_ref, route_ref, cnt_ref, carry_ref, *, dh):
    tm = h1_ref.shape[0]

    @pl.when((pl.program_id(0) == 0) & (pl.program_id(1) == 0))
    def _():
        carry_ref[...] = jnp.zeros_like(carry_ref)

    h1 = h1_ref[...]
    q = _dot(h1.astype(BF16), wq_ref[...])
    outs = []
    for hd in range(XA_HEADS):
        qh = q[:, hd * dh:(hd + 1) * dh].astype(BF16)
        kh = k_ref[:, hd * dh:(hd + 1) * dh]
        vh = v_ref[:, hd * dh:(hd + 1) * dh]
        sc = _dot_nt(qh, kh) * (dh ** -0.5)
        e = jnp.exp(sc - jnp.max(sc, axis=1, keepdims=True))
        p = e / jnp.sum(e, axis=1, keepdims=True)
        outs.append(_dot(p.astype(BF16), vh))
    o = jnp.concatenate(outs, axis=1)
    xa = _dot(o.astype(BF16), wo_ref[...])
    h2 = _layer_norm(DN_ALPHA * h1 + xa, g_ref[...], b_ref[...])
    h2_ref[...] = h2

    logits = _dot(h2.astype(BF16), wr_ref[...]) + br_ref[...]
    lane = lax.broadcasted_iota(jnp.int32, (tm, LANES), 1)
    lane_f = lane.astype(F32)
    work = logits
    vals, idxs = [], []
    for _ in range(TOP_K):
        m = jnp.max(work, axis=1, keepdims=True)
        idx = jnp.min(jnp.where(work == m, lane_f, float(LANES)), axis=1, keepdims=True)
        vals.append(m)
        idxs.append(idx)
        work = jnp.where(lane_f == idx, -jnp.inf, work)
    es = [jnp.exp(v - vals[0]) for v in vals]
    tot = es[0]
    for e_ in es[1:]:
        tot = tot + e_
    onehot = jnp.zeros((tm, LANES), F32)
    for idx in idxs:
        onehot = onehot + (lane_f == idx).astype(F32)
    ri = lax.broadcasted_iota(jnp.int32, (tm, tm), 0)
    ci = lax.broadcasted_iota(jnp.int32, (tm, tm), 1)
    tri = (ci < ri).astype(BF16)
    carry = carry_ref[0:1, :]
    ranks = carry + _dot(tri, onehot.astype(BF16))
    gw = jnp.zeros((tm, LANES), F32)
    route = jnp.zeros((tm, LANES), F32)
    for k_ in range(TOP_K):
        rk = jnp.sum(jnp.where(lane_f == idxs[k_], ranks, 0.0), axis=1, keepdims=True)
        gw = gw + jnp.where(lane == k_, es[k_] / tot, 0.0)
        route = route + jnp.where(lane == k_, idxs[k_], 0.0) + jnp.where(lane == TOP_K + k_, rk, 0.0)
    gw_ref[...] = gw
    route_ref[...] = route.astype(jnp.int32)
    carry = carry + jnp.sum(onehot, axis=0, keepdims=True)
    carry_ref[...] = jnp.broadcast_to(carry, carry_ref.shape)
    cnt_ref[...] = jnp.broadcast_to(carry, cnt_ref.shape)


def _xattn(h1, kmem, vmem, wq, wo, g, b, w_r, b_r, bsz, seq, mem_len, tm):
    t, d = h1.shape
    nt = seq // tm
    full = lambda a: pl.BlockSpec(a.shape, lambda i, j: (0,) * a.ndim)
    kern = functools.partial(_xattn_kernel, dh=d // XA_HEADS)
    return pl.pallas_call(
        kern,
        grid=(bsz, nt),
        in_specs=[
            pl.BlockSpec((tm, d), lambda i, j: (i * nt + j, 0)),
            pl.BlockSpec((mem_len, d), lambda i, j: (i, 0)),
            pl.BlockSpec((mem_len, d), lambda i, j: (i, 0)),
            full(wq), full(wo), full(g), full(b), full(w_r), full(b_r),
        ],
        out_specs=[
            pl.BlockSpec((tm, d), lambda i, j: (i * nt + j, 0)),
            pl.BlockSpec((tm, LANES), lambda i, j: (i * nt + j, 0)),
            pl.BlockSpec((tm, LANES), lambda i, j: (i * nt + j, 0)),
            pl.BlockSpec((8, LANES), lambda i, j: (0, 0)),
        ],
        out_shape=[
            jax.ShapeDtypeStruct((t, d), F32),
            jax.ShapeDtypeStruct((t, LANES), F32),
            jax.ShapeDtypeStruct((t, LANES), jnp.int32),
            jax.ShapeDtypeStruct((8, LANES), F32),
        ],
        scratch_shapes=[pltpu.VMEM((8, LANES), F32)],
        compiler_params=_params("arbitrary", "arbitrary"),
        name="xattn_ln2_router",
    )(h1, kmem, vmem, wq, wo, g, b, w_r, b_r)


def _dispatch_kernel(pad_lo_ref, pad_n_ref, na_ref, dest_ref, h_ref, xs_ref, zero_ref, sem, zsem):
    tm = h_ref.shape[0]
    n_blocks = xs_ref.shape[0] // MOE_BLOCK

    @pl.when(pl.program_id(0) == 0)
    def _():
        zero_ref[...] = jnp.zeros_like(zero_ref)
        zero_row = zero_ref.at[pl.ds(0, 1), :]

        def per_expert(e, carry):
            lo = pad_lo_ref[e]
            n = pad_n_ref[e]

            def start(r, c):
                pltpu.make_async_copy(zero_row, xs_ref.at[pl.ds(lo + r, 1), :], zsem).start()
                return c

            def wait(r, c):
                pltpu.make_async_copy(zero_row, xs_ref.at[pl.ds(0, 1), :], zsem).wait()
                return c

            lax.fori_loop(0, n, start, 0)
            lax.fori_loop(0, n, wait, 0)
            return carry

        lax.fori_loop(0, N_EXPERTS, per_expert, 0)

        def tail(b, carry):
            cp = pltpu.make_async_copy(zero_ref, xs_ref.at[pl.ds(b * MOE_BLOCK, MOE_BLOCK), :], zsem)
            cp.start()
            cp.wait()
            return carry

        lax.fori_loop(na_ref[0], n_blocks, tail, 0)

    def issue(tok, carry):
        src = h_ref.at[pl.ds(tok, 1), :]
        for k_ in range(TOP_K):
            pltpu.make_async_copy(src, xs_ref.at[pl.ds(dest_ref[tok * TOP_K + k_], 1), :], sem).start()
        return carry

    lax.fori_loop(0, tm, issue, 0, unroll=ISSUE_UNROLL)
    for _ in range(TOP_K):
        pltpu.make_async_copy(h_ref, xs_ref.at[pl.ds(0, tm), :], sem).wait()


def _dispatch(pad_lo, pad_n, nact, dest_flat, h2, n_rows, tm):
    t, d = h2.shape
    grid_spec = pltpu.PrefetchScalarGridSpec(
        num_scalar_prefetch=3,
        grid=(t // tm,),
        in_specs=[
            pl.BlockSpec((tm * TOP_K,), lambda i, *_: (i,), memory_space=pltpu.SMEM),
            pl.BlockSpec((tm, d), lambda i, *_: (i, 0)),
        ],
        out_specs=pl.BlockSpec(memory_space=pl.ANY),
        scratch_shapes=[pltpu.VMEM((MOE_BLOCK, d), F32), pltpu.SemaphoreType.DMA(()),
                        pltpu.SemaphoreType.DMA(())],
    )
    return pl.pallas_call(
        _dispatch_kernel,
        grid_spec=grid_spec,
        out_shape=jax.ShapeDtypeStruct((n_rows, d), F32),
        compiler_params=_params("arbitrary"),
        name="moe_dispatch",
    )(pad_lo, pad_n, nact, dest_flat, h2)


def _expert_kernel(be_ref, na_ref, x_ref, wgu_ref, bgu_ref, wdn_ref, bdn_ref, y_ref, wgu_s, wdn_s, *, d_exp):
    blk = pl.program_id(0)
    active = blk < na_ref[0]

    @pl.when(jnp.logical_not(active))
    def _():
        y_ref[...] = jnp.zeros_like(y_ref)

    @pl.when(active & ((blk == 0) | (be_ref[blk] != be_ref[jnp.maximum(blk - 1, 0)])))
    def _():
        wgu_s[...] = wgu_ref[0].astype(BF16)
        wdn_s[...] = wdn_ref[0].astype(BF16)

    @pl.when(active)
    def _():
        gu = _dot(x_ref[...].astype(BF16), wgu_s[...]) + bgu_ref[0]
        gate = jnp.minimum(gu[:, :d_exp], SWIGLU_LIMIT)
        up = jnp.clip(gu[:, d_exp:], -SWIGLU_LIMIT, SWIGLU_LIMIT)
        act = (up + 1.0) * (gate * _sigmoid(SWIGLU_ALPHA * gate))
        y_ref[...] = _dot(act.astype(BF16), wdn_s[...]) + bdn_ref[0]


def _experts(block_e, nact, xs, w_gu, b_gu, w_dn, b_dn):
    n_rows, d = xs.shape
    n_blocks = n_rows // MOE_BLOCK
    d_exp = w_dn.shape[1]
    kern = functools.partial(_expert_kernel, d_exp=d_exp)
    emap = lambda i, be, na: (be[i], 0, 0)
    grid_spec = pltpu.PrefetchScalarGridSpec(
        num_scalar_prefetch=2,
        grid=(n_blocks,),
        in_specs=[
            pl.BlockSpec((MOE_BLOCK, d), lambda i, be, na: (i, 0)),
            pl.BlockSpec((1, d, 2 * d_exp), emap),
            pl.BlockSpec((1, 1, 2 * d_exp), emap),
            pl.BlockSpec((1, d_exp, d), emap),
            pl.BlockSpec((1, 1, d), emap),
        ],
        out_specs=pl.BlockSpec((MOE_BLOCK, d), lambda i, be, na: (i, 0)),
        scratch_shapes=[pltpu.VMEM((d, 2 * d_exp), BF16), pltpu.VMEM((d_exp, d), BF16)],
    )
    return pl.pallas_call(
        kern,
        grid_spec=grid_spec,
        out_shape=jax.ShapeDtypeStruct((n_rows, d), F32),
        compiler_params=_params("arbitrary"),
        name="moe_experts",
    )(block_e, nact, xs, w_gu, b_gu, w_dn, b_dn)


def _combine_kernel(dest_ref, gw_ref, h_ref, g_ref, b_ref, y_ref, out_ref, buf_ref, sem):
    tm = h_ref.shape[0]

    def issue(tok, carry):
        for k_ in range(TOP_K):
            pltpu.make_async_copy(y_ref.at[pl.ds(dest_ref[tok * TOP_K + k_], 1), :],
                                  buf_ref.at[k_, pl.ds(tok, 1), :], sem).start()
        return carry

    lax.fori_loop(0, tm, issue, 0, unroll=ISSUE_UNROLL)
    for k_ in range(TOP_K):
        pltpu.make_async_copy(y_ref.at[pl.ds(0, tm), :], buf_ref.at[k_], sem).wait()
    gw = gw_ref[...]
    ff = buf_ref[0] * gw[:, 0:1]
    for k_ in range(1, TOP_K):
        ff = ff + buf_ref[k_] * gw[:, k_:k_ + 1]
    out_ref[...] = _layer_norm(DN_ALPHA * h_ref[...] + ff, g_ref[...], b_ref[...])


def _combine(dest_flat, gw, h2, g, b, yb, tm):
    t, d = h2.shape
    return pl.pallas_call(
        _combine_kernel,
        grid=(t // tm,),
        in_specs=[
            pl.BlockSpec((tm * TOP_K,), lambda i: (i,), memory_space=pltpu.SMEM),
            pl.BlockSpec((tm, LANES), lambda i: (i, 0)),
            pl.BlockSpec((tm, d), lambda i: (i, 0)),
            pl.BlockSpec((1, d), lambda i: (0, 0)),
            pl.BlockSpec((1, d), lambda i: (0, 0)),
            pl.BlockSpec(memory_space=pl.ANY),
        ],
        out_specs=pl.BlockSpec((tm, d), lambda i: (i, 0)),
        out_shape=jax.ShapeDtypeStruct((t, d), F32),
        scratch_shapes=[pltpu.VMEM((TOP_K, tm, d), F32), pltpu.SemaphoreType.DMA(())],
        compiler_params=_params("arbitrary"),
        name="moe_combine_ln3",
    )(dest_flat, gw, h2, g, b, yb)


def _pick(n, pref):
    return pref if n % pref == 0 else n


def kernel(x, mem, ln_in_g, ln_in_b, w_in, ml_gate_bias, ml_norm_g, gdn_conv_w, gdn_a_log, gdn_dt_bias, gdn_norm_g, w_branch_ml, w_branch_gdn, w_mix_out, ln1_g, ln1_b, xa_wq, xa_wk, xa_wv, xa_wo, ln2_g, ln2_b, w_router, b_router, w_gu, b_gu, w_dn, b_dn, ln3_g, ln3_b):
    bsz, seq, d = x.shape
    mem_len = mem.shape[1]
    t = bsz * seq
    ml_dv = d // ML_HEADS
    ml_dqk = ml_dv // 2
    ml_qk_w, ml_v_w = ML_HEADS * ml_dqk, ML_HEADS * ml_dv
    n_qk = d // GDN_DK
    n_v = 2 * n_qk
    gdn_qk_w, gdn_v_w = n_qk * GDN_DK, n_v * GDN_DV
    conv_ch = 2 * gdn_qk_w + gdn_v_w
    splits = (ml_qk_w, ml_qk_w, ml_v_w, ml_v_w, 2 * ML_HEADS, conv_ch, gdn_v_w, n_v, n_v, d, d)
    names = ("mq", "mk", "mv", "mo", "mif", "gqkv", "gz", "ga", "gb", "gate_ml", "gate_gdn")
    starts = {}
    acc = 0
    for nm, sz in zip(names, splits):
        starts[nm] = (acc, sz)
        acc += sz
    row2 = lambda a: a.reshape(1, -1).astype(F32)

    h = x.reshape(t, d)
    for l in range(DEPTH):
        w = w_in[l]
        seg = lambda nm: w[:, starts[nm][0]:starts[nm][0] + starts[nm][1]]
        order = ("gqkv", "gz", "mv", "mo", "gate_ml", "gate_gdn", "mq", "mk")
        cols = {}
        off = 0
        for nm in order:
            assert off % starts[nm][1] == 0
            cols[nm] = off
            off += starts[nm][1]
        w_big = jnp.concatenate([seg(nm) for nm in order], axis=1).astype(BF16)
        n_small = 2 * ML_HEADS + 2 * n_v
        w_small = jnp.concatenate([seg("mif"), seg("ga"), seg("gb"),
                                   jnp.zeros((d, LANES - n_small), F32)], axis=1).astype(BF16)
        ga_off, gb_off = 2 * ML_HEADS, 2 * ML_HEADS + n_v

        if l == 0:
            h0, proj, small = _ln_proj(h, row2(ln_in_g), row2(ln_in_b), w_big, w_small,
                                       _pick(t, 1024), _pick(off, 1024))
        else:
            raise NotImplementedError("DEPTH > 1")
        n_ch = seq // CHUNK
        small_t = small[:, :n_small].reshape(bsz * n_ch, CHUNK, n_small).transpose(0, 2, 1)

        bias = ml_gate_bias[l].astype(F32)
        hm = _mlstm(proj, small_t, bias.reshape(-1, 1), row2(ml_norm_g[l]),
                    bsz, seq, cols, ml_dqk, ml_dv, ML_CHUNKS_PER_STEP)
        al = gdn_a_log[l].astype(F32)
        dt = gdn_dt_bias[l].astype(F32)
        og = _gdn(proj, small_t, gdn_conv_w[l].astype(F32), al.reshape(-1, 1), dt.reshape(-1, 1),
                  row2(gdn_norm_g[l]), bsz, seq, cols, n_qk, n_v, ga_off, gb_off, GDN_CHUNKS_PER_STEP)
        h1 = _merge(hm, og, proj, h0, w_branch_ml[l].astype(BF16), w_branch_gdn[l].astype(BF16),
                    w_mix_out[l].astype(BF16), row2(ln1_g[l]), row2(ln1_b[l]), cols, _pick(t, 512))

        kmem, vmem = _kv_proj(mem.reshape(bsz * mem_len, d), xa_wk[l].astype(BF16), xa_wv[l].astype(BF16),
                              _pick(bsz * mem_len, 512))
        w_r = jnp.concatenate([w_router[l], jnp.zeros((d, LANES - N_EXPERTS), F32)], axis=1).astype(BF16)
        b_r = jnp.concatenate([b_router[l].astype(F32), jnp.full((LANES - N_EXPERTS,), NEG_BIG, F32)]).reshape(1, -1)
        tm_x = _pick(seq, 256)
        h2, gw, route, cnt = _xattn(h1, kmem, vmem, xa_wq[l].astype(BF16), xa_wo[l].astype(BF16),
                                    row2(ln2_g[l]), row2(ln2_b[l]), w_r, b_r, bsz, seq, mem_len, tm_x)

        counts = cnt[0, :N_EXPERTS].astype(jnp.int32)
        padded = (counts + MOE_BLOCK - 1) // MOE_BLOCK * MOE_BLOCK
        pad_end = jnp.cumsum(padded)
        pad_start = pad_end - padded
        n_asg = t * TOP_K
        n_blocks = -(-n_asg // MOE_BLOCK) + N_EXPERTS
        n_rows = n_blocks * MOE_BLOCK
        top_e = route[:, :TOP_K]
        rank = route[:, TOP_K:2 * TOP_K]
        sel = top_e[:, :, None] == jnp.arange(N_EXPERTS, dtype=jnp.int32)[None, None, :]
        dest = (jnp.sum(jnp.where(sel, pad_start[None, None, :], 0), axis=-1) + rank).astype(jnp.int32)
        dest_flat = dest.reshape(n_asg)
        blk_row = jnp.arange(n_blocks, dtype=jnp.int32) * MOE_BLOCK
        block_e = jnp.minimum(jnp.sum(pad_end[None, :] <= blk_row[:, None], axis=1), N_EXPERTS - 1).astype(jnp.int32)
        nact = (pad_end[-1:] // MOE_BLOCK).astype(jnp.int32)
        pad_lo = (pad_start + counts).astype(jnp.int32)
        pad_n = (padded - counts).astype(jnp.int32)

        tm_d = _pick(t, 256)
        xs = _dispatch(pad_lo, pad_n, nact, dest_flat, h2, n_rows, tm_d)
        yb = _experts(block_e, nact, xs, w_gu[l], b_gu[l].astype(F32)[:, None, :],
                      w_dn[l], b_dn[l].astype(F32)[:, None, :])
        h = _combine(dest_flat, gw, h2, row2(ln3_g[l]), row2(ln3_b[l]), yb, tm_d)
    return h.reshape(bsz, seq, d)
```

```python
import functools

import jax
import jax.numpy as jnp
from jax import lax
from jax.experimental import pallas as pl
from jax.experimental.pallas import tpu as pltpu

F32 = jnp.float32
BF16 = jnp.bfloat16

CHUNK = 64
ML_HEADS = 4
GDN_DK = 128
GDN_DV = 128
CONV_K = 4
XA_HEADS = 4
N_EXPERTS = 32
TOP_K = 4
SWIGLU_LIMIT = 7.0
SWIGLU_ALPHA = 1.702
MOE_BLOCK = 256
DEPTH = 1
DN_ALPHA = (2 * DEPTH) ** 0.25
LN_EPS = 1e-5
RMS_EPS = 1e-6
LANES = 128
VMEM_LIMIT = 56 * 1024 * 1024
NEG_BIG = -1e30
ML_CHUNKS_PER_STEP = 4
GDN_CHUNKS_PER_STEP = 2
GDN_CONV_COLS = 512


def _params(*sem):
    return pltpu.CompilerParams(dimension_semantics=sem, vmem_limit_bytes=VMEM_LIMIT)


def _dot(a, b):
    return jnp.dot(a, b, preferred_element_type=F32)


def _dot_nt(a, b):
    return lax.dot_general(a, b, (((1,), (1,)), ((), ())), preferred_element_type=F32)


def _layer_norm(x, g, b):
    mu = jnp.mean(x, axis=-1, keepdims=True)
    xc = x - mu
    var = jnp.mean(xc * xc, axis=-1, keepdims=True)
    return xc * lax.rsqrt(var + LN_EPS) * g + b


def _sigmoid(x):
    return 1.0 / (1.0 + jnp.exp(-x))


def _log_sigmoid(x):
    return jnp.minimum(x, 0.0) - jnp.log(1.0 + jnp.exp(-jnp.abs(x)))


def _softplus(x):
    return jnp.maximum(x, 0.0) + jnp.log(1.0 + jnp.exp(-jnp.abs(x)))


def _split3(x):
    hi = x.astype(BF16)
    r = x - hi.astype(F32)
    mid = r.astype(BF16)
    lo = (r - mid.astype(F32)).astype(BF16)
    return hi, mid, lo


def _cumsum_lanes(x, upper):
    hi, mid, lo = _split3(x)
    return _dot(hi, upper) + _dot(mid, upper) + _dot(lo, upper)


def _ln_proj_kernel(x_ref, g_ref, b_ref, w_ref, ws_ref, h_ref, p_ref, s_ref, xn_ref):
    @pl.when(pl.program_id(1) == 0)
    def _():
        h = _layer_norm(x_ref[...], g_ref[...], b_ref[...])
        h_ref[...] = h
        hb = h.astype(BF16)
        xn_ref[...] = hb
        s_ref[...] = _dot(hb, ws_ref[...])

    p_ref[...] = _dot(xn_ref[...], w_ref[...]).astype(BF16)


def _ln_proj(x2, g, b, w_big, w_small, tm, tn):
    t, d = x2.shape
    nw = w_big.shape[1]
    return pl.pallas_call(
        _ln_proj_kernel,
        grid=(t // tm, nw // tn),
        in_specs=[
            pl.BlockSpec((tm, d), lambda i, j: (i, 0)),
            pl.BlockSpec((1, d), lambda i, j: (0, 0)),
            pl.BlockSpec((1, d), lambda i, j: (0, 0)),
            pl.BlockSpec((d, tn), lambda i, j: (0, j)),
            pl.BlockSpec((d, LANES), lambda i, j: (0, 0)),
        ],
        out_specs=[
            pl.BlockSpec((tm, d), lambda i, j: (i, 0)),
            pl.BlockSpec((tm, tn), lambda i, j: (i, j)),
            pl.BlockSpec((tm, LANES), lambda i, j: (i, 0)),
        ],
        out_shape=[
            jax.ShapeDtypeStruct((t, d), F32),
            jax.ShapeDtypeStruct((t, nw), BF16),
            jax.ShapeDtypeStruct((t, LANES), F32),
        ],
        scratch_shapes=[pltpu.VMEM((tm, d), BF16)],
        compiler_params=_params("arbitrary", "arbitrary"),
        name="ln_in_proj",
    )(x2, g, b, w_big, w_small)


def _mlstm_kernel(q_ref, k_ref, v_ref, o_ref, gr_ref, br_ref, ng_ref,
                  out_ref, c_ref, m_ref, sqk_ref, kt_ref, p_ref, ktw_ref, b3_ref, li3_ref, kw3_ref,
                  gs_ref, mc_ref, stb_ref, stm_ref, std_ref, num_ref, kv_ref, *, dqk, dv, n_ch):
    L = CHUNK
    H = ML_HEADS

    @pl.when(pl.program_id(1) == 0)
    def _():
        c_ref[...] = jnp.zeros_like(c_ref)
        m_ref[...] = jnp.zeros_like(m_ref)

    ii = lax.broadcasted_iota(jnp.int32, (L, L), 0)
    jj = lax.broadcasted_iota(jnp.int32, (L, L), 1)
    causal = jj <= ii
    diag = ii == jj
    upper = (ii <= jj).astype(BF16)
    scale = dqk ** -0.5
    heads = [(g, h) for g in range(n_ch) for h in range(H)]

    for g, h in heads:
        rows = slice(g * L, (g + 1) * L)
        kf = k_ref[rows, h * dqk:(h + 1) * dqk].astype(F32) * scale
        sqk_ref[g * H + h] = _dot_nt(q_ref[rows, h * dqk:(h + 1) * dqk], kf.astype(BF16))
        kt_ref[g * H + h] = kf.T

    pre = gr_ref[:, 0:2 * H, :].reshape(n_ch * 2 * H, L) + jnp.concatenate([br_ref[...]] * n_ch, axis=0)
    b_all = _cumsum_lanes(_log_sigmoid(pre), upper)
    gs_all = b_all[:, L - 1:L]
    a_all = gs_all - b_all + pltpu.roll(pre, H, axis=0)
    mc_all = jnp.max(a_all, axis=1, keepdims=True)
    kw_all = jnp.exp(a_all - mc_all)
    gs_ref[...] = jnp.broadcast_to(gs_all, gs_ref.shape)
    mc_ref[...] = jnp.broadcast_to(mc_all, mc_ref.shape)
    for g, h in heads:
        r = g * 2 * H + H + h
        b3_ref[g * H + h] = b_all[r:r + 1, :]
        li3_ref[g * H + h] = pre[r - H:r - H + 1, :]
        kw3_ref[g * H + h] = kw_all[r:r + 1, :]

    b_r = b3_ref[...]
    b_c = jnp.sum(jnp.where(diag, b_r, 0.0), axis=2, keepdims=True)
    dmat = jnp.where(causal, b_c - b_r + li3_ref[...], -jnp.inf)
    m_intra = jnp.max(dmat, axis=2, keepdims=True)
    p = jnp.exp(dmat - m_intra) * sqk_ref[...]
    p_ref[...] = p.astype(BF16)
    stb_ref[...] = jnp.broadcast_to(b_c, stb_ref.shape)
    stm_ref[...] = jnp.broadcast_to(m_intra, stm_ref.shape)
    std_ref[...] = jnp.broadcast_to(jnp.sum(p, axis=2, keepdims=True), std_ref.shape)
    ktw_ref[...] = (kt_ref[...] * kw3_ref[...]).astype(BF16)

    ones = jnp.ones((L, LANES), BF16)
    for g, h in heads:
        vh = v_ref[g * L:(g + 1) * L, h * dv:(h + 1) * dv]
        num_ref[g * H + h] = _dot(p_ref[g * H + h], vh)
        kv_ref[g * H + h] = _dot(ktw_ref[g * H + h], jnp.concatenate([vh, ones], axis=1))

    rep = dv // LANES
    wide = lambda s: jnp.concatenate([s] * rep, axis=1)
    for g, h in heads:
        i = g * H + h
        r = g * 2 * H + H + h
        rows = slice(g * L, (g + 1) * L)
        c_st = c_ref[h]
        m_st = m_ref[h:h + 1, :]
        qc = _dot(q_ref[rows, h * dqk:(h + 1) * dqk], c_st.astype(BF16))
        m_intra = stm_ref[i]
        inter_log = stb_ref[i] + m_st
        m_out = jnp.maximum(inter_log, m_intra)
        s_inter = jnp.exp(inter_log - m_out)
        s_intra = jnp.exp(m_intra - m_out)
        num = wide(s_inter) * qc[:, 0:dv] + wide(s_intra) * num_ref[i]
        den = s_inter * qc[:, dv:dv + LANES] + s_intra * std_ref[i]
        hh = num / wide(jnp.maximum(jnp.abs(den), jnp.exp(-m_out)))
        gs = gs_ref[r:r + 1, :]
        mc = mc_ref[r:r + 1, :]
        m_new = jnp.maximum(gs + m_st, mc)
        dec = jnp.exp(gs + m_st - m_new)
        s_new = jnp.exp(mc - m_new)
        c_ref[h] = (jnp.concatenate([dec] * (rep + 1), axis=1) * c_st
                    + jnp.concatenate([s_new] * (rep + 1), axis=1) * kv_ref[i])
        m_ref[h:h + 1, :] = m_new

        rms = lax.rsqrt(jnp.mean(hh * hh, axis=1, keepdims=True) + RMS_EPS)
        og = _sigmoid(o_ref[rows, h * dv:(h + 1) * dv].astype(F32))
        out_ref[rows, h * dv:(h + 1) * dv] = (hh * rms * ng_ref[:, h * dv:(h + 1) * dv] * og).astype(BF16)


def _mlstm(proj, small_t, bias_r, norm_g, bsz, seq, cols, dqk, dv, n_ch):
    n = seq // (CHUNK * n_ch)
    L = CHUNK
    R = n_ch * L
    H = ML_HEADS
    qk_w, v_w = H * dqk, H * dv
    t = bsz * seq
    nh = n_ch * H
    kern = functools.partial(_mlstm_kernel, dqk=dqk, dv=dv, n_ch=n_ch)
    row = lambda b, c: b * n + c
    return pl.pallas_call(
        kern,
        grid=(bsz, n),
        in_specs=[
            pl.BlockSpec((R, qk_w), lambda b, c: (row(b, c), cols["mq"] // qk_w)),
            pl.BlockSpec((R, qk_w), lambda b, c: (row(b, c), cols["mk"] // qk_w)),
            pl.BlockSpec((R, v_w), lambda b, c: (row(b, c), cols["mv"] // v_w)),
            pl.BlockSpec((R, v_w), lambda b, c: (row(b, c), cols["mo"] // v_w)),
            pl.BlockSpec((n_ch, small_t.shape[1], L), lambda b, c: (row(b, c), 0, 0)),
            pl.BlockSpec(bias_r.shape, lambda b, c: (0, 0)),
            pl.BlockSpec((1, v_w), lambda b, c: (0, 0)),
        ],
        out_specs=pl.BlockSpec((R, v_w), lambda b, c: (row(b, c), 0)),
        out_shape=jax.ShapeDtypeStruct((t, v_w), BF16),
        scratch_shapes=[
            pltpu.VMEM((H, dqk, dv + LANES), F32),
            pltpu.VMEM((8, LANES), F32),
            pltpu.VMEM((nh, L, L), F32),
            pltpu.VMEM((nh, dqk, L), F32),
            pltpu.VMEM((nh, L, L), BF16),
            pltpu.VMEM((nh, dqk, L), BF16),
            pltpu.VMEM((nh, 1, L), F32),
            pltpu.VMEM((nh, 1, L), F32),
            pltpu.VMEM((nh, 1, L), F32),
            pltpu.VMEM((n_ch * 2 * H, LANES), F32),
            pltpu.VMEM((n_ch * 2 * H, LANES), F32),
            pltpu.VMEM((nh, L, LANES), F32),
            pltpu.VMEM((nh, L, LANES), F32),
            pltpu.VMEM((nh, L, LANES), F32),
            pltpu.VMEM((nh, L, dv), F32),
            pltpu.VMEM((nh, dqk, dv + LANES), F32),
        ],
        compiler_params=_params("arbitrary", "arbitrary"),
        name="mlstm",
    )(proj, proj, proj, proj, small_t, bias_r, norm_g)


def _gdn_kernel(x_ref, z_ref, gr_ref, cw_ref, al_ref, dt_ref, ng_ref,
                out_ref, s_ref, xs_ref, c_ref, qn_ref, kn_ref, knt_ref, kk_ref, qk_ref, pw_ref, x_ref_, rhs_ref,
                attn_ref, qd_ref, kdt_ref, gam3_ref, beta3_ref, kdwb3_ref, gt_ref, sol_ref, vn_ref,
                *, n_qk, n_v, ga_off, gb_off, n_ch):
    L = CHUNK
    R = n_ch * L
    dk, dv = GDN_DK, GDN_DV
    qk_w = n_qk * dk
    rep = n_v // n_qk
    conv_ch = 2 * qk_w + n_v * dv

    @pl.when(pl.program_id(1) == 0)
    def _():
        s_ref[...] = jnp.zeros_like(s_ref)
        xs_ref[0:8, :] = jnp.zeros((8, xs_ref.shape[1]), F32)

    xs_ref[8:8 + R, :] = x_ref[...].astype(F32)
    for cb in range(0, conv_ch, GDN_CONV_COLS):
        cs = slice(cb, cb + GDN_CONV_COLS)
        conv = cw_ref[0:1, cs] * xs_ref[8 - (CONV_K - 1):8 - (CONV_K - 1) + R, cs]
        for j in range(1, CONV_K):
            off = 8 - (CONV_K - 1) + j
            conv = conv + cw_ref[j:j + 1, cs] * xs_ref[off:off + R, cs]
        c_ref[:, cs] = conv * _sigmoid(conv)
    xs_ref[0:8, :] = xs_ref[R:R + 8, :]

    ii = lax.broadcasted_iota(jnp.int32, (L, L), 0)
    jj = lax.broadcasted_iota(jnp.int32, (L, L), 1)
    incl = jj <= ii
    strict = jj < ii
    diag = ii == jj
    eye = diag.astype(F32)
    upper = (ii <= jj).astype(BF16)

    for g in range(n_ch):
        rows = slice(g * L, (g + 1) * L)
        for hk in range(n_qk):
            iq = g * n_qk + hk
            cq = c_ref[rows, hk * dk:(hk + 1) * dk]
            ck = c_ref[rows, qk_w + hk * dk:qk_w + (hk + 1) * dk]
            qn = cq * lax.rsqrt(jnp.sum(cq * cq, axis=1, keepdims=True) + RMS_EPS) * (dk ** -0.5)
            kn = ck * lax.rsqrt(jnp.sum(ck * ck, axis=1, keepdims=True) + RMS_EPS)
            qn_ref[iq] = qn
            kn_ref[iq] = kn
            knt_ref[iq] = kn.T
            kb = kn.astype(BF16)
            kk_ref[iq] = _dot_nt(kb, kb)
            qk_ref[iq] = _dot_nt(qn.astype(BF16), kb)

    for g in range(n_ch):
        gd = -jnp.exp(al_ref[...]) * _softplus(gr_ref[g, ga_off:ga_off + n_v, :] + dt_ref[...])
        gam = _cumsum_lanes(gd, upper)
        beta = _sigmoid(gr_ref[g, gb_off:gb_off + n_v, :])
        g_tot = gam[:, L - 1:L]
        kdwb = jnp.exp(g_tot - gam) * beta
        gt_ref[g * n_v:(g + 1) * n_v, :] = jnp.broadcast_to(jnp.exp(g_tot), (n_v, LANES))
        for h in range(n_v):
            gam3_ref[g * n_v + h] = gam[h:h + 1, :]
            beta3_ref[g * n_v + h] = beta[h:h + 1, :]
            kdwb3_ref[g * n_v + h] = kdwb[h:h + 1, :]

    for g in range(n_ch):
        sv = slice(g * n_v, (g + 1) * n_v)
        sq = slice(g * n_qk, (g + 1) * n_qk)
        rows = slice(g * L, (g + 1) * L)
        gam_r = gam3_ref[sv]
        beta_r = beta3_ref[sv]
        gam_c = jnp.sum(jnp.where(diag, gam_r, 0.0), axis=2, keepdims=True)
        decm = jnp.exp(jnp.where(incl, gam_c - gam_r, -jnp.inf))
        db = decm * beta_r
        a = jnp.where(strict, jnp.repeat(kk_ref[sq], rep, axis=0) * db, 0.0)
        pw_ref[sv] = a
        x_ref_[sv] = eye - a
        attn_ref[sv] = (jnp.repeat(qk_ref[sq], rep, axis=0) * db).astype(BF16)
        eg_c = jnp.exp(gam_c)
        rhs_ref[sv, :, dv:dv + dk] = (jnp.repeat(kn_ref[sq], rep, axis=0) * eg_c).astype(BF16)
        qd_ref[sv] = (jnp.repeat(qn_ref[sq], rep, axis=0) * eg_c).astype(BF16)
        kdt_ref[sv] = (jnp.repeat(knt_ref[sq], rep, axis=0) * kdwb3_ref[sv]).astype(BF16)
        for h in range(n_v):
            rhs_ref[g * n_v + h, :, 0:dv] = c_ref[rows, 2 * qk_w + h * dv:2 * qk_w + (h + 1) * dv].astype(BF16)

    n_sq = 5
    for lvl in range(n_sq + 1):
        for i in range(n_ch * n_v):
            pwb = pw_ref[i].astype(BF16)
            if lvl > 0:
                xv = x_ref_[i]
                x_ref_[i] = xv + _dot(xv.astype(BF16), pwb)
            if lvl < n_sq:
                pw_ref[i] = _dot(pwb, pwb)

    for i in range(n_ch * n_v):
        sol_ref[i] = _dot(x_ref_[i].astype(BF16), rhs_ref[i])

    for g in range(n_ch):
        rows = slice(g * L, (g + 1) * L)
        for h in range(n_v):
            sol = sol_ref[g * n_v + h]
            vn_ref[h] = (sol[:, 0:dv] - _dot(sol[:, dv:dv + dk].astype(BF16), s_ref[h].astype(BF16))).astype(BF16)
        for h in range(n_v):
            i = g * n_v + h
            s_st = s_ref[h]
            vnb = vn_ref[h]
            o = _dot(qd_ref[i], s_st.astype(BF16)) + _dot(attn_ref[i], vnb)
            s_ref[h] = gt_ref[i:i + 1, :] * s_st + _dot(kdt_ref[i], vnb)
            rms = lax.rsqrt(jnp.mean(o * o, axis=1, keepdims=True) + RMS_EPS)
            zz = z_ref[rows, h * dv:(h + 1) * dv].astype(F32)
            out_ref[rows, h * dv:(h + 1) * dv] = (o * rms * ng_ref[...] * (zz * _sigmoid(zz))).astype(BF16)


def _gdn(proj, small_t, conv_w, al, dt, norm_g, bsz, seq, cols, n_qk, n_v, ga_off, gb_off, n_ch):
    n = seq // (CHUNK * n_ch)
    L = CHUNK
    R = n_ch * L
    conv_ch = 2 * n_qk * GDN_DK + n_v * GDN_DV
    v_w = n_v * GDN_DV
    t = bsz * seq
    nq, nv = n_ch * n_qk, n_ch * n_v
    kern = functools.partial(_gdn_kernel, n_qk=n_qk, n_v=n_v, ga_off=ga_off, gb_off=gb_off, n_ch=n_ch)
    row = lambda b, c: b * n + c
    full = lambda a: pl.BlockSpec(a.shape, lambda b, c: (0,) * a.ndim)
    return pl.pallas_call(
        kern,
        grid=(bsz, n),
        in_specs=[
            pl.BlockSpec((R, conv_ch), lambda b, c: (row(b, c), cols["gqkv"] // conv_ch)),
            pl.BlockSpec((R, v_w), lambda b, c: (row(b, c), cols["gz"] // v_w)),
            pl.BlockSpec((n_ch, small_t.shape[1], L), lambda b, c: (row(b, c), 0, 0)),
            full(conv_w), full(al), full(dt), full(norm_g),
        ],
        out_specs=pl.BlockSpec((R, v_w), lambda b, c: (row(b, c), 0)),
        out_shape=jax.ShapeDtypeStruct((t, v_w), BF16),
        scratch_shapes=[
            pltpu.VMEM((n_v, GDN_DK, GDN_DV), F32),
            pltpu.VMEM((8 + R + 8, conv_ch), F32),
            pltpu.VMEM((R, conv_ch), F32),
            pltpu.VMEM((nq, L, GDN_DK), F32),
            pltpu.VMEM((nq, L, GDN_DK), F32),
            pltpu.VMEM((nq, GDN_DK, L), F32),
            pltpu.VMEM((nq, L, L), F32),
            pltpu.VMEM((nq, L, L), F32),
            pltpu.VMEM((nv, L, L), F32),
            pltpu.VMEM((nv, L, L), F32),
            pltpu.VMEM((nv, L, GDN_DV + GDN_DK), BF16),
            pltpu.VMEM((nv, L, L), BF16),
            pltpu.VMEM((nv, L, GDN_DK), BF16),
            pltpu.VMEM((nv, GDN_DK, L), BF16),
            pltpu.VMEM((nv, 1, L), F32),
            pltpu.VMEM((nv, 1, L), F32),
            pltpu.VMEM((nv, 1, L), F32),
            pltpu.VMEM((nv, LANES), F32),
            pltpu.VMEM((nv, L, GDN_DV + GDN_DK), F32),
            pltpu.VMEM((n_v, L, GDN_DV), BF16),
        ],
        compiler_params=_params("arbitrary", "arbitrary"),
        name="gdn",
    )(proj, proj, small_t, conv_w, al, dt, norm_g)


def _merge_kernel(hm_ref, og_ref, gm_ref, gg_ref, h0_ref, wbm_ref, wbg_ref, wo_ref, g_ref, b_ref, out_ref):
    y_ml = _dot(hm_ref[...], wbm_ref[...])
    y_gdn = _dot(og_ref[...], wbg_ref[...])
    merged = _sigmoid(gm_ref[...].astype(F32)) * y_ml + _sigmoid(gg_ref[...].astype(F32)) * y_gdn
    mix = _dot(merged.astype(BF16), wo_ref[...])
    out_ref[...] = _layer_norm(DN_ALPHA * h0_ref[...] + mix, g_ref[...], b_ref[...])


def _merge(hm, og, proj, h0, w_bm, w_bg, w_out, g, b, cols, tm):
    t, d = h0.shape
    full = lambda a: pl.BlockSpec(a.shape, lambda i: (0,) * a.ndim)
    return pl.pallas_call(
        _merge_kernel,
        grid=(t // tm,),
        in_specs=[
            pl.BlockSpec((tm, hm.shape[1]), lambda i: (i, 0)),
            pl.BlockSpec((tm, og.shape[1]), lambda i: (i, 0)),
            pl.BlockSpec((tm, d), lambda i: (i, cols["gate_ml"] // d)),
            pl.BlockSpec((tm, d), lambda i: (i, cols["gate_gdn"] // d)),
            pl.BlockSpec((tm, d), lambda i: (i, 0)),
            full(w_bm), full(w_bg), full(w_out), full(g), full(b),
        ],
        out_specs=pl.BlockSpec((tm, d), lambda i: (i, 0)),
        out_shape=jax.ShapeDtypeStruct((t, d), F32),
        compiler_params=_params("arbitrary"),
        name="merge_out_ln1",
    )(hm, og, proj, proj, h0, w_bm, w_bg, w_out, g, b)


def _kv_kernel(m_ref, wk_ref, wv_ref, k_ref, v_ref):
    mb = m_ref[...].astype(BF16)
    k_ref[...] = _dot(mb, wk_ref[...]).astype(BF16)
    v_ref[...] = _dot(mb, wv_ref[...]).astype(BF16)


def _kv_proj(mem2, wk, wv, tm):
    t, d = mem2.shape
    full = lambda a: pl.BlockSpec(a.shape, lambda i: (0,) * a.ndim)
    return pl.pallas_call(
        _kv_kernel,
        grid=(t // tm,),
        in_specs=[pl.BlockSpec((tm, d), lambda i: (i, 0)), full(wk), full(wv)],
        out_specs=[pl.BlockSpec((tm, d), lambda i: (i, 0))] * 2,
        out_shape=[jax.ShapeDtypeStruct((t, d), BF16)] * 2,
        compiler_params=_params("arbitrary"),
        name="mem_kv_proj",
    )(mem2, wk, wv)


def _xattn_kernel(h1_ref, k_ref, v_ref, wq_ref, wo_ref, g_ref, b_ref, wr_ref, br_ref,
                  h2_ref, gw_ref, route_ref, cnt_ref, carry_ref, *, dh):
    tm = h1_ref.shape[0]

    @pl.when((pl.program_id(0) == 0) & (pl.program_id(1) == 0))
    def _():
        carry_ref[...] = jnp.zeros_like(carry_ref)

    h1 = h1_ref[...]
    q = _dot(h1.astype(BF16), wq_ref[...])
    outs = []
    for hd in range(XA_HEADS):
        qh = q[:, hd * dh:(hd + 1) * dh].astype(BF16)
        kh = k_ref[:, hd * dh:(hd + 1) * dh]
        vh = v_ref[:, hd * dh:(hd + 1) * dh]
        sc = _dot_nt(qh, kh) * (dh ** -0.5)
        e = jnp.exp(sc - jnp.max(sc, axis=1, keepdims=True))
        p = e / jnp.sum(e, axis=1, keepdims=True)
        outs.append(_dot(p.astype(BF16), vh))
    o = jnp.concatenate(outs, axis=1)
    xa = _dot(o.astype(BF16), wo_ref[...])
    h2 = _layer_norm(DN_ALPHA * h1 + xa, g_ref[...], b_ref[...])
    h2_ref[...] = h2

    logits = _dot(h2.astype(BF16), wr_ref[...]) + br_ref[...]
    lane = lax.broadcasted_iota(jnp.int32, (tm, LANES), 1)
    lane_f = lane.astype(F32)
    work = logits
    vals, idxs = [], []
    for _ in range(TOP_K):
        m = jnp.max(work, axis=1, keepdims=True)
        idx = jnp.min(jnp.where(work == m, lane_f, float(LANES)), axis=1, keepdims=True)
        vals.append(m)
        idxs.append(idx)
        work = jnp.where(lane_f == idx, -jnp.inf, work)
    es = [jnp.exp(v - vals[0]) for v in vals]
    tot = es[0]
    for e_ in es[1:]:
        tot = tot + e_
    onehot = jnp.zeros((tm, LANES), F32)
    for idx in idxs:
        onehot = onehot + (lane_f == idx).astype(F32)
    ri = lax.broadcasted_iota(jnp.int32, (tm, tm), 0)
    ci = lax.broadcasted_iota(jnp.int32, (tm, tm), 1)
    tri = (ci < ri).astype(BF16)
    carry = carry_ref[0:1, :]
    ranks = carry + _dot(tri, onehot.astype(BF16))
    gw = jnp.zeros((tm, LANES), F32)
    route = jnp.zeros((tm, LANES), F32)
    for k_ in range(TOP_K):
        rk = jnp.sum(jnp.where(lane_f == idxs[k_], ranks, 0.0), axis=1, keepdims=True)
        gw = gw + jnp.where(lane == k_, es[k_] / tot, 0.0)
        route = route + jnp.where(lane == k_, idxs[k_], 0.0) + jnp.where(lane == TOP_K + k_, rk, 0.0)
    gw_ref[...] = gw
    route_ref[...] = route.astype(jnp.int32)
    carry = carry + jnp.sum(onehot, axis=0, keepdims=True)
    carry_ref[...] = jnp.broadcast_to(carry, carry_ref.shape)
    cnt_ref[...] = jnp.broadcast_to(carry, cnt_ref.shape)


def _xattn(h1, kmem, vmem, wq, wo, g, b, w_r, b_r, bsz, seq, mem_len, tm):
    t, d = h1.shape
    nt = seq // tm
    full = lambda a: pl.BlockSpec(a.shape, lambda i, j: (0,) * a.ndim)
    kern = functools.partial(_xattn_kernel, dh=d // XA_HEADS)
    return pl.pallas_call(
        kern,
        grid=(bsz, nt),
        in_specs=[
            pl.BlockSpec((tm, d), lambda i, j: (i * nt + j, 0)),
            pl.BlockSpec((mem_len, d), lambda i, j: (i, 0)),
            pl.BlockSpec((mem_len, d), lambda i, j: (i, 0)),
            full(wq), full(wo), full(g), full(b), full(w_r), full(b_r),
        ],
        out_specs=[
            pl.BlockSpec((tm, d), lambda i, j: (i * nt + j, 0)),
            pl.BlockSpec((tm, LANES), lambda i, j: (i * nt + j, 0)),
            pl.BlockSpec((tm, LANES), lambda i, j: (i * nt + j, 0)),
            pl.BlockSpec((8, LANES), lambda i, j: (0, 0)),
        ],
        out_shape=[
            jax.ShapeDtypeStruct((t, d), F32),
            jax.ShapeDtypeStruct((t, LANES), F32),
            jax.ShapeDtypeStruct((t, LANES), jnp.int32),
            jax.ShapeDtypeStruct((8, LANES), F32),
        ],
        scratch_shapes=[pltpu.VMEM((8, LANES), F32)],
        compiler_params=_params("arbitrary", "arbitrary"),
        name="xattn_ln2_router",
    )(h1, kmem, vmem, wq, wo, g, b, w_r, b_r)


def _dispatch_kernel(pad_lo_ref, pad_n_ref, na_ref, dest_ref, h_ref, xs_ref, zero_ref, sem, zsem):
    tm = h_ref.shape[0]
    step = pl.program_id(0)
    n_blocks = xs_ref.shape[0] // MOE_BLOCK

    @pl.when(step == 0)
    def _():
        zero_ref[...] = jnp.zeros_like(zero_ref)
        zero_row = zero_ref.at[pl.ds(0, 1), :]

        def per_expert(e, carry):
            lo = pad_lo_ref[e]
            n = pad_n_ref[e]

            def start(r, c):
                pltpu.make_async_copy(zero_row, xs_ref.at[pl.ds(lo + r, 1), :], zsem).start()
                return c

            def wait(r, c):
                pltpu.make_async_copy(zero_row, xs_ref.at[pl.ds(0, 1), :], zsem).wait()
                return c

            lax.fori_loop(0, n, start, 0)
            lax.fori_loop(0, n, wait, 0)
            return carry

        lax.fori_loop(0, N_EXPERTS, per_expert, 0)

        def tail(b, carry):
            cp = pltpu.make_async_copy(zero_ref, xs_ref.at[pl.ds(b * MOE_BLOCK, MOE_BLOCK), :], zsem)
            cp.start()
            cp.wait()
            return carry

        lax.fori_loop(na_ref[0], n_blocks, tail, 0)

    for tok in range(tm):
        src = h_ref.at[pl.ds(tok, 1), :]
        for k_ in range(TOP_K):
            pltpu.make_async_copy(src, xs_ref.at[pl.ds(dest_ref[tok * TOP_K + k_], 1), :], sem).start(priority=k_ % 2)
    for _ in range(TOP_K):
        pltpu.make_async_copy(h_ref, xs_ref.at[pl.ds(0, tm), :], sem).wait()


def _dispatch(pad_lo, pad_n, nact, dest_flat, h2, n_rows, tm):
    t, d = h2.shape
    grid_spec = pltpu.PrefetchScalarGridSpec(
        num_scalar_prefetch=3,
        grid=(t // tm,),
        in_specs=[
            pl.BlockSpec((tm * TOP_K,), lambda i, *_: (i,), memory_space=pltpu.SMEM),
            pl.BlockSpec((tm, d), lambda i, *_: (i, 0)),
        ],
        out_specs=pl.BlockSpec(memory_space=pl.ANY),
        scratch_shapes=[pltpu.VMEM((MOE_BLOCK, d), F32), pltpu.SemaphoreType.DMA(()),
                        pltpu.SemaphoreType.DMA(())],
    )
    return pl.pallas_call(
        _dispatch_kernel,
        grid_spec=grid_spec,
        out_shape=jax.ShapeDtypeStruct((n_rows, d), F32),
        compiler_params=_params("arbitrary"),
        name="moe_dispatch",
    )(pad_lo, pad_n, nact, dest_flat, h2)


def _expert_kernel(be_ref, na_ref, x_ref, wgu_ref, bgu_ref, wdn_ref, bdn_ref, y_ref, wgu_s, wdn_s, *, d_exp):
    blk = pl.program_id(0)
    active = blk < na_ref[0]

    @pl.when(jnp.logical_not(active))
    def _():
        y_ref[...] = jnp.zeros_like(y_ref)

    @pl.when(active & ((blk == 0) | (be_ref[blk] != be_ref[jnp.maximum(blk - 1, 0)])))
    def _():
        wgu_s[...] = wgu_ref[0].astype(BF16)
        wdn_s[...] = wdn_ref[0].astype(BF16)

    @pl.when(active)
    def _():
        gu = _dot(x_ref[...].astype(BF16), wgu_s[...]) + bgu_ref[0]
        gate = jnp.minimum(gu[:, :d_exp], SWIGLU_LIMIT)
        up = jnp.clip(gu[:, d_exp:], -SWIGLU_LIMIT, SWIGLU_LIMIT)
        act = (up + 1.0) * (gate * _sigmoid(SWIGLU_ALPHA * gate))
        y_ref[...] = _dot(act.astype(BF16), wdn_s[...]) + bdn_ref[0]


def _experts(block_e, nact, xs, w_gu, b_gu, w_dn, b_dn):
    n_rows, d = xs.shape
    n_blocks = n_rows // MOE_BLOCK
    d_exp = w_dn.shape[1]
    kern = functools.partial(_expert_kernel, d_exp=d_exp)
    emap = lambda i, be, na: (be[i], 0, 0)
    grid_spec = pltpu.PrefetchScalarGridSpec(
        num_scalar_prefetch=2,
        grid=(n_blocks,),
        in_specs=[
            pl.BlockSpec((MOE_BLOCK, d), lambda i, be, na: (i, 0)),
            pl.BlockSpec((1, d, 2 * d_exp), emap),
            pl.BlockSpec((1, 1, 2 * d_exp), emap),
            pl.BlockSpec((1, d_exp, d), emap),
            pl.BlockSpec((1, 1, d), emap),
        ],
        out_specs=pl.BlockSpec((MOE_BLOCK, d), lambda i, be, na: (i, 0)),
        scratch_shapes=[pltpu.VMEM((d, 2 * d_exp), BF16), pltpu.VMEM((d_exp, d), BF16)],
    )
    return pl.pallas_call(
        kern,
        grid_spec=grid_spec,
        out_shape=jax.ShapeDtypeStruct((n_rows, d), F32),
        compiler_params=_params("arbitrary"),
        name="moe_experts",
    )(block_e, nact, xs, w_gu, b_gu, w_dn, b_dn)


def _combine_kernel(dest_ref, dest_nxt_ref, gw_ref, h_ref, g_ref, b_ref, y_ref, out_ref, buf_a, buf_b, sem):
    tm = buf_a.shape[1]
    step = pl.program_id(0)
    n_steps = pl.num_programs(0)

    def issue(idx_ref, half, buf, s):
        for tok in range(tm):
            for k_ in range(TOP_K):
                pltpu.make_async_copy(y_ref.at[pl.ds(idx_ref[(half * tm + tok) * TOP_K + k_], 1), :],
                                      buf.at[k_, pl.ds(tok, 1), :], sem.at[s]).start(priority=k_ % 2)

    def consume(half, buf, s):
        for k_ in range(TOP_K):
            pltpu.make_async_copy(y_ref.at[pl.ds(0, tm), :], buf.at[k_], sem.at[s]).wait()
        rows = slice(half * tm, (half + 1) * tm)
        gw = gw_ref[rows, :]
        ff = buf[0] * gw[:, 0:1]
        for k_ in range(1, TOP_K):
            ff = ff + buf[k_] * gw[:, k_:k_ + 1]
        out_ref[rows, :] = _layer_norm(DN_ALPHA * h_ref[rows, :] + ff, g_ref[...], b_ref[...])

    @pl.when(step == 0)
    def _():
        issue(dest_ref, 0, buf_a, 0)

    issue(dest_ref, 1, buf_b, 1)
    consume(0, buf_a, 0)

    @pl.when(step + 1 < n_steps)
    def _():
        issue(dest_nxt_ref, 0, buf_a, 0)

    consume(1, buf_b, 1)


def _combine(dest_flat, gw, h2, g, b, yb, tm):
    t, d = h2.shape
    n = t // (2 * tm)
    return pl.pallas_call(
        _combine_kernel,
        grid=(n,),
        in_specs=[
            pl.BlockSpec((2 * tm * TOP_K,), lambda i: (i,), memory_space=pltpu.SMEM),
            pl.BlockSpec((2 * tm * TOP_K,), lambda i: (jnp.minimum(i + 1, n - 1),), memory_space=pltpu.SMEM),
            pl.BlockSpec((2 * tm, LANES), lambda i: (i, 0)),
            pl.BlockSpec((2 * tm, d), lambda i: (i, 0)),
            pl.BlockSpec((1, d), lambda i: (0, 0)),
            pl.BlockSpec((1, d), lambda i: (0, 0)),
            pl.BlockSpec(memory_space=pl.ANY),
        ],
        out_specs=pl.BlockSpec((2 * tm, d), lambda i: (i, 0)),
        out_shape=jax.ShapeDtypeStruct((t, d), F32),
        scratch_shapes=[pltpu.VMEM((TOP_K, tm, d), F32), pltpu.VMEM((TOP_K, tm, d), F32),
                        pltpu.SemaphoreType.DMA((2,))],
        compiler_params=_params("arbitrary"),
        name="moe_combine_ln3",
    )(dest_flat, dest_flat, gw, h2, g, b, yb)


def _pick(n, pref):
    return pref if n % pref == 0 else n


def kernel(x, mem, ln_in_g, ln_in_b, w_in, ml_gate_bias, ml_norm_g, gdn_conv_w, gdn_a_log, gdn_dt_bias, gdn_norm_g, w_branch_ml, w_branch_gdn, w_mix_out, ln1_g, ln1_b, xa_wq, xa_wk, xa_wv, xa_wo, ln2_g, ln2_b, w_router, b_router, w_gu, b_gu, w_dn, b_dn, ln3_g, ln3_b):
    bsz, seq, d = x.shape
    mem_len = mem.shape[1]
    t = bsz * seq
    ml_dv = d // ML_HEADS
    ml_dqk = ml_dv // 2
    ml_qk_w, ml_v_w = ML_HEADS * ml_dqk, ML_HEADS * ml_dv
    n_qk = d // GDN_DK
    n_v = 2 * n_qk
    gdn_qk_w, gdn_v_w = n_qk * GDN_DK, n_v * GDN_DV
    conv_ch = 2 * gdn_qk_w + gdn_v_w
    splits = (ml_qk_w, ml_qk_w, ml_v_w, ml_v_w, 2 * ML_HEADS, conv_ch, gdn_v_w, n_v, n_v, d, d)
    names = ("mq", "mk", "mv", "mo", "mif", "gqkv", "gz", "ga", "gb", "gate_ml", "gate_gdn")
    starts = {}
    acc = 0
    for nm, sz in zip(names, splits):
        starts[nm] = (acc, sz)
        acc += sz
    row2 = lambda a: a.reshape(1, -1).astype(F32)

    h = x.reshape(t, d)
    for l in range(DEPTH):
        w = w_in[l]
        seg = lambda nm: w[:, starts[nm][0]:starts[nm][0] + starts[nm][1]]
        order = ("gqkv", "gz", "mv", "mo", "gate_ml", "gate_gdn", "mq", "mk")
        cols = {}
        off = 0
        for nm in order:
            assert off % starts[nm][1] == 0
            cols[nm] = off
            off += starts[nm][1]
        w_big = jnp.concatenate([seg(nm) for nm in order], axis=1).astype(BF16)
        n_small = 2 * ML_HEADS + 2 * n_v
        w_small = jnp.concatenate([seg("mif"), seg("ga"), seg("gb"),
                                   jnp.zeros((d, LANES - n_small), F32)], axis=1).astype(BF16)
        ga_off, gb_off = 2 * ML_HEADS, 2 * ML_HEADS + n_v

        if l == 0:
            h0, proj, small = _ln_proj(h, row2(ln_in_g), row2(ln_in_b), w_big, w_small,
                                       _pick(t, 1024), _pick(off, 1024))
        else:
            raise NotImplementedError("DEPTH > 1")
        n_ch = seq // CHUNK
        small_t = small[:, :n_small].reshape(bsz * n_ch, CHUNK, n_small).transpose(0, 2, 1)

        bias = ml_gate_bias[l].astype(F32)
        hm = _mlstm(proj, small_t, bias.reshape(-1, 1), row2(ml_norm_g[l]),
                    bsz, seq, cols, ml_dqk, ml_dv, ML_CHUNKS_PER_STEP)
        al = gdn_a_log[l].astype(F32)
        dt = gdn_dt_bias[l].astype(F32)
        og = _gdn(proj, small_t, gdn_conv_w[l].astype(F32), al.reshape(-1, 1), dt.reshape(-1, 1),
                  row2(gdn_norm_g[l]), bsz, seq, cols, n_qk, n_v, ga_off, gb_off, GDN_CHUNKS_PER_STEP)
        h1 = _merge(hm, og, proj, h0, w_branch_ml[l].astype(BF16), w_branch_gdn[l].astype(BF16),
                    w_mix_out[l].astype(BF16), row2(ln1_g[l]), row2(ln1_b[l]), cols, _pick(t, 512))

        kmem, vmem = _kv_proj(mem.reshape(bsz * mem_len, d), xa_wk[l].astype(BF16), xa_wv[l].astype(BF16),
                              _pick(bsz * mem_len, 512))
        w_r = jnp.concatenate([w_router[l], jnp.zeros((d, LANES - N_EXPERTS), F32)], axis=1).astype(BF16)
        b_r = jnp.concatenate([b_router[l].astype(F32), jnp.full((LANES - N_EXPERTS,), NEG_BIG, F32)]).reshape(1, -1)
        tm_x = _pick(seq, 512)
        h2, gw, route, cnt = _xattn(h1, kmem, vmem, xa_wq[l].astype(BF16), xa_wo[l].astype(BF16),
                                    row2(ln2_g[l]), row2(ln2_b[l]), w_r, b_r, bsz, seq, mem_len, tm_x)

        counts = cnt[0, :N_EXPERTS].astype(jnp.int32)
        padded = (counts + MOE_BLOCK - 1) // MOE_BLOCK * MOE_BLOCK
        pad_end = jnp.cumsum(padded)
        pad_start = pad_end - padded
        n_asg = t * TOP_K
        n_blocks = -(-n_asg // MOE_BLOCK) + N_EXPERTS
        n_rows = n_blocks * MOE_BLOCK
        top_e = route[:, :TOP_K]
        rank = route[:, TOP_K:2 * TOP_K]
        sel = top_e[:, :, None] == jnp.arange(N_EXPERTS, dtype=jnp.int32)[None, None, :]
        dest = (jnp.sum(jnp.where(sel, pad_start[None, None, :], 0), axis=-1) + rank).astype(jnp.int32)
        dest_flat = dest.reshape(n_asg)
        blk_row = jnp.arange(n_blocks, dtype=jnp.int32) * MOE_BLOCK
        block_e = jnp.minimum(jnp.sum(pad_end[None, :] <= blk_row[:, None], axis=1), N_EXPERTS - 1).astype(jnp.int32)
        nact = (pad_end[-1:] // MOE_BLOCK).astype(jnp.int32)
        pad_lo = (pad_start + counts).astype(jnp.int32)
        pad_n = (padded - counts).astype(jnp.int32)

        tm_d = _pick(t, 256)
        xs = _dispatch(pad_lo, pad_n, nact, dest_flat, h2, n_rows, _pick(t, 512))
        yb = _experts(block_e, nact, xs, w_gu[l], b_gu[l].astype(F32)[:, None, :],
                      w_dn[l], b_dn[l].astype(F32)[:, None, :])
        h = _combine(dest_flat, gw, h2, row2(ln3_g[l]), row2(ln3_b[l]), yb, tm_d)
    return h.reshape(bsz, seq, d)
```

```python
import functools

import jax
import jax.numpy as jnp
from jax import lax
from jax.experimental import pallas as pl
from jax.experimental.pallas import tpu as pltpu

F32 = jnp.float32
BF16 = jnp.bfloat16

CHUNK = 64
ML_HEADS = 4
GDN_DK = 128
GDN_DV = 128
CONV_K = 4
XA_HEADS = 4
N_EXPERTS = 32
TOP_K = 4
SWIGLU_LIMIT = 7.0
SWIGLU_ALPHA = 1.702
MOE_BLOCK = 512
DEPTH = 1
DN_ALPHA = (2 * DEPTH) ** 0.25
LN_EPS = 1e-5
RMS_EPS = 1e-6
LANES = 128
VMEM_LIMIT = 56 * 1024 * 1024
NEG_BIG = -1e30
ML_CHUNKS_PER_STEP = 8
GDN_CHUNKS_PER_STEP = 4
GDN_CONV_COLS = 512


def _params(*sem):
    return pltpu.CompilerParams(dimension_semantics=sem, vmem_limit_bytes=VMEM_LIMIT)


def _dot(a, b):
    return jnp.dot(a, b, preferred_element_type=F32)


def _dot_nt(a, b):
    return lax.dot_general(a, b, (((1,), (1,)), ((), ())), preferred_element_type=F32)


def _layer_norm(x, g, b):
    mu = jnp.mean(x, axis=-1, keepdims=True)
    xc = x - mu
    var = jnp.mean(xc * xc, axis=-1, keepdims=True)
    return xc * lax.rsqrt(var + LN_EPS) * g + b


def _sigmoid(x):
    return 1.0 / (1.0 + jnp.exp(-x))


def _log_sigmoid(x):
    return jnp.minimum(x, 0.0) - jnp.log(1.0 + jnp.exp(-jnp.abs(x)))


def _softplus(x):
    return jnp.maximum(x, 0.0) + jnp.log(1.0 + jnp.exp(-jnp.abs(x)))


def _split3(x):
    hi = x.astype(BF16)
    r = x - hi.astype(F32)
    mid = r.astype(BF16)
    lo = (r - mid.astype(F32)).astype(BF16)
    return hi, mid, lo


def _cumsum_lanes(x, upper):
    hi, mid, lo = _split3(x)
    return _dot(hi, upper) + _dot(mid, upper) + _dot(lo, upper)


def _ln_proj_kernel(x_ref, g_ref, b_ref, w_ref, ws_ref, h_ref, p_ref, s_ref, xn_ref):
    @pl.when(pl.program_id(1) == 0)
    def _():
        h = _layer_norm(x_ref[...], g_ref[...], b_ref[...])
        h_ref[...] = h
        hb = h.astype(BF16)
        xn_ref[...] = hb
        s_ref[...] = _dot(hb, ws_ref[...])

    p_ref[...] = _dot(xn_ref[...], w_ref[...]).astype(BF16)


def _ln_proj(x2, g, b, w_big, w_small, tm, tn):
    t, d = x2.shape
    nw = w_big.shape[1]
    return pl.pallas_call(
        _ln_proj_kernel,
        grid=(t // tm, nw // tn),
        in_specs=[
            pl.BlockSpec((tm, d), lambda i, j: (i, 0)),
            pl.BlockSpec((1, d), lambda i, j: (0, 0)),
            pl.BlockSpec((1, d), lambda i, j: (0, 0)),
            pl.BlockSpec((d, tn), lambda i, j: (0, j)),
            pl.BlockSpec((d, LANES), lambda i, j: (0, 0)),
        ],
        out_specs=[
            pl.BlockSpec((tm, d), lambda i, j: (i, 0)),
            pl.BlockSpec((tm, tn), lambda i, j: (i, j)),
            pl.BlockSpec((tm, LANES), lambda i, j: (i, 0)),
        ],
        out_shape=[
            jax.ShapeDtypeStruct((t, d), F32),
            jax.ShapeDtypeStruct((t, nw), BF16),
            jax.ShapeDtypeStruct((t, LANES), F32),
        ],
        scratch_shapes=[pltpu.VMEM((tm, d), BF16)],
        compiler_params=_params("arbitrary", "arbitrary"),
        name="ln_in_proj",
    )(x2, g, b, w_big, w_small)


def _mlstm_kernel(q_ref, k_ref, v_ref, o_ref, gr_ref, br_ref, ng_ref,
                  out_ref, c_ref, m_ref, sqk_ref, kt_ref, p_ref, ktw_ref, b3_ref, li3_ref, kw3_ref,
                  gs_ref, mc_ref, stb_ref, stm_ref, std_ref, num_ref, kv_ref, *, dqk, dv, n_ch):
    L = CHUNK
    H = ML_HEADS

    @pl.when(pl.program_id(1) == 0)
    def _():
        c_ref[...] = jnp.zeros_like(c_ref)
        m_ref[...] = jnp.zeros_like(m_ref)

    ii = lax.broadcasted_iota(jnp.int32, (L, L), 0)
    jj = lax.broadcasted_iota(jnp.int32, (L, L), 1)
    causal = jj <= ii
    diag = ii == jj
    upper = (ii <= jj).astype(BF16)
    scale = dqk ** -0.5
    heads = [(g, h) for g in range(n_ch) for h in range(H)]

    for g, h in heads:
        rows = slice(g * L, (g + 1) * L)
        kf = k_ref[rows, h * dqk:(h + 1) * dqk].astype(F32) * scale
        sqk_ref[g * H + h] = _dot_nt(q_ref[rows, h * dqk:(h + 1) * dqk], kf.astype(BF16))
        kt_ref[g * H + h] = kf.T

    pre = gr_ref[:, 0:2 * H, :].reshape(n_ch * 2 * H, L) + jnp.concatenate([br_ref[...]] * n_ch, axis=0)
    b_all = _cumsum_lanes(_log_sigmoid(pre), upper)
    gs_all = b_all[:, L - 1:L]
    a_all = gs_all - b_all + pltpu.roll(pre, H, axis=0)
    mc_all = jnp.max(a_all, axis=1, keepdims=True)
    kw_all = jnp.exp(a_all - mc_all)
    gs_ref[...] = jnp.broadcast_to(gs_all, gs_ref.shape)
    mc_ref[...] = jnp.broadcast_to(mc_all, mc_ref.shape)
    for g, h in heads:
        r = g * 2 * H + H + h
        b3_ref[g * H + h] = b_all[r:r + 1, :]
        li3_ref[g * H + h] = pre[r - H:r - H + 1, :]
        kw3_ref[g * H + h] = kw_all[r:r + 1, :]

    b_r = b3_ref[...]
    b_c = jnp.sum(jnp.where(diag, b_r, 0.0), axis=2, keepdims=True)
    dmat = jnp.where(causal, b_c - b_r + li3_ref[...], -jnp.inf)
    m_intra = jnp.max(dmat, axis=2, keepdims=True)
    p = jnp.exp(dmat - m_intra) * sqk_ref[...]
    p_ref[...] = p.astype(BF16)
    stb_ref[...] = jnp.broadcast_to(b_c, stb_ref.shape)
    stm_ref[...] = jnp.broadcast_to(m_intra, stm_ref.shape)
    std_ref[...] = jnp.broadcast_to(jnp.sum(p, axis=2, keepdims=True), std_ref.shape)
    ktw_ref[...] = (kt_ref[...] * kw3_ref[...]).astype(BF16)

    ones = jnp.ones((L, LANES), BF16)
    for g, h in heads:
        vh = v_ref[g * L:(g + 1) * L, h * dv:(h + 1) * dv]
        num_ref[g * H + h] = _dot(p_ref[g * H + h], vh)
        kv_ref[g * H + h] = _dot(ktw_ref[g * H + h], jnp.concatenate([vh, ones], axis=1))

    rep = dv // LANES
    wide = lambda s: jnp.concatenate([s] * rep, axis=1)
    for g, h in heads:
        i = g * H + h
        r = g * 2 * H + H + h
        rows = slice(g * L, (g + 1) * L)
        c_st = c_ref[h]
        m_st = m_ref[h:h + 1, :]
        qc = _dot(q_ref[rows, h * dqk:(h + 1) * dqk], c_st.astype(BF16))
        m_intra = stm_ref[i]
        inter_log = stb_ref[i] + m_st
        m_out = jnp.maximum(inter_log, m_intra)
        s_inter = jnp.exp(inter_log - m_out)
        s_intra = jnp.exp(m_intra - m_out)
        num = wide(s_inter) * qc[:, 0:dv] + wide(s_intra) * num_ref[i]
        den = s_inter * qc[:, dv:dv + LANES] + s_intra * std_ref[i]
        hh = num / wide(jnp.maximum(jnp.abs(den), jnp.exp(-m_out)))
        gs = gs_ref[r:r + 1, :]
        mc = mc_ref[r:r + 1, :]
        m_new = jnp.maximum(gs + m_st, mc)
        dec = jnp.exp(gs + m_st - m_new)
        s_new = jnp.exp(mc - m_new)
        c_ref[h] = (jnp.concatenate([dec] * (rep + 1), axis=1) * c_st
                    + jnp.concatenate([s_new] * (rep + 1), axis=1) * kv_ref[i])
        m_ref[h:h + 1, :] = m_new

        rms = lax.rsqrt(jnp.mean(hh * hh, axis=1, keepdims=True) + RMS_EPS)
        og = _sigmoid(o_ref[rows, h * dv:(h + 1) * dv].astype(F32))
        out_ref[rows, h * dv:(h + 1) * dv] = (hh * rms * ng_ref[:, h * dv:(h + 1) * dv] * og).astype(BF16)


def _mlstm(proj, small_t, bias_r, norm_g, bsz, seq, cols, dqk, dv, n_ch):
    n = seq // (CHUNK * n_ch)
    L = CHUNK
    R = n_ch * L
    H = ML_HEADS
    qk_w, v_w = H * dqk, H * dv
    t = bsz * seq
    nh = n_ch * H
    kern = functools.partial(_mlstm_kernel, dqk=dqk, dv=dv, n_ch=n_ch)
    row = lambda b, c: b * n + c
    return pl.pallas_call(
        kern,
        grid=(bsz, n),
        in_specs=[
            pl.BlockSpec((R, qk_w), lambda b, c: (row(b, c), cols["mq"] // qk_w)),
            pl.BlockSpec((R, qk_w), lambda b, c: (row(b, c), cols["mk"] // qk_w)),
            pl.BlockSpec((R, v_w), lambda b, c: (row(b, c), cols["mv"] // v_w)),
            pl.BlockSpec((R, v_w), lambda b, c: (row(b, c), cols["mo"] // v_w)),
            pl.BlockSpec((n_ch, small_t.shape[1], L), lambda b, c: (row(b, c), 0, 0)),
            pl.BlockSpec(bias_r.shape, lambda b, c: (0, 0)),
            pl.BlockSpec((1, v_w), lambda b, c: (0, 0)),
        ],
        out_specs=pl.BlockSpec((R, v_w), lambda b, c: (row(b, c), 0)),
        out_shape=jax.ShapeDtypeStruct((t, v_w), BF16),
        scratch_shapes=[
            pltpu.VMEM((H, dqk, dv + LANES), F32),
            pltpu.VMEM((8, LANES), F32),
            pltpu.VMEM((nh, L, L), F32),
            pltpu.VMEM((nh, dqk, L), F32),
            pltpu.VMEM((nh, L, L), BF16),
            pltpu.VMEM((nh, dqk, L), BF16),
            pltpu.VMEM((nh, 1, L), F32),
            pltpu.VMEM((nh, 1, L), F32),
            pltpu.VMEM((nh, 1, L), F32),
            pltpu.VMEM((n_ch * 2 * H, LANES), F32),
            pltpu.VMEM((n_ch * 2 * H, LANES), F32),
            pltpu.VMEM((nh, L, LANES), F32),
            pltpu.VMEM((nh, L, LANES), F32),
            pltpu.VMEM((nh, L, LANES), F32),
            pltpu.VMEM((nh, L, dv), F32),
            pltpu.VMEM((nh, dqk, dv + LANES), F32),
        ],
        compiler_params=_params("arbitrary", "arbitrary"),
        name="mlstm",
    )(proj, proj, proj, proj, small_t, bias_r, norm_g)


def _gdn_kernel(x_ref, z_ref, gr_ref, cw_ref, al_ref, dt_ref, ng_ref,
                out_ref, s_ref, xs_ref, c_ref, qn_ref, kn_ref, knt_ref, kk_ref, qk_ref, pw_ref, x_ref_, rhs_ref,
                attn_ref, qd_ref, kdt_ref, gam3_ref, beta3_ref, kdwb3_ref, gt_ref, sol_ref, vn_ref,
                *, n_qk, n_v, ga_off, gb_off, n_ch):
    L = CHUNK
    R = n_ch * L
    dk, dv = GDN_DK, GDN_DV
    qk_w = n_qk * dk
    rep = n_v // n_qk
    conv_ch = 2 * qk_w + n_v * dv

    @pl.when(pl.program_id(1) == 0)
    def _():
        s_ref[...] = jnp.zeros_like(s_ref)
        xs_ref[0:8, :] = jnp.zeros((8, xs_ref.shape[1]), F32)

    xs_ref[8:8 + R, :] = x_ref[...].astype(F32)
    for cb in range(0, conv_ch, GDN_CONV_COLS):
        cs = slice(cb, cb + GDN_CONV_COLS)
        conv = cw_ref[0:1, cs] * xs_ref[8 - (CONV_K - 1):8 - (CONV_K - 1) + R, cs]
        for j in range(1, CONV_K):
            off = 8 - (CONV_K - 1) + j
            conv = conv + cw_ref[j:j + 1, cs] * xs_ref[off:off + R, cs]
        c_ref[:, cs] = conv * _sigmoid(conv)
    xs_ref[0:8, :] = xs_ref[R:R + 8, :]

    ii = lax.broadcasted_iota(jnp.int32, (L, L), 0)
    jj = lax.broadcasted_iota(jnp.int32, (L, L), 1)
    incl = jj <= ii
    strict = jj < ii
    diag = ii == jj
    eye = diag.astype(F32)
    upper = (ii <= jj).astype(BF16)

    for g in range(n_ch):
        rows = slice(g * L, (g + 1) * L)
        for hk in range(n_qk):
            iq = g * n_qk + hk
            cq = c_ref[rows, hk * dk:(hk + 1) * dk]
            ck = c_ref[rows, qk_w + hk * dk:qk_w + (hk + 1) * dk]
            qn = cq * lax.rsqrt(jnp.sum(cq * cq, axis=1, keepdims=True) + RMS_EPS) * (dk ** -0.5)
            kn = ck * lax.rsqrt(jnp.sum(ck * ck, axis=1, keepdims=True) + RMS_EPS)
            qn_ref[iq] = qn
            kn_ref[iq] = kn
            knt_ref[iq] = kn.T
            kb = kn.astype(BF16)
            kk_ref[iq] = _dot_nt(kb, kb)
            qk_ref[iq] = _dot_nt(qn.astype(BF16), kb)

    for g in range(n_ch):
        gd = -jnp.exp(al_ref[...]) * _softplus(gr_ref[g, ga_off:ga_off + n_v, :] + dt_ref[...])
        gam = _cumsum_lanes(gd, upper)
        beta = _sigmoid(gr_ref[g, gb_off:gb_off + n_v, :])
        g_tot = gam[:, L - 1:L]
        kdwb = jnp.exp(g_tot - gam) * beta
        gt_ref[g * n_v:(g + 1) * n_v, :] = jnp.broadcast_to(jnp.exp(g_tot), (n_v, LANES))
        for h in range(n_v):
            gam3_ref[g * n_v + h] = gam[h:h + 1, :]
            beta3_ref[g * n_v + h] = beta[h:h + 1, :]
            kdwb3_ref[g * n_v + h] = kdwb[h:h + 1, :]

    for g in range(n_ch):
        sv = slice(g * n_v, (g + 1) * n_v)
        sq = slice(g * n_qk, (g + 1) * n_qk)
        rows = slice(g * L, (g + 1) * L)
        gam_r = gam3_ref[sv]
        beta_r = beta3_ref[sv]
        gam_c = jnp.sum(jnp.where(diag, gam_r, 0.0), axis=2, keepdims=True)
        decm = jnp.exp(jnp.where(incl, gam_c - gam_r, -jnp.inf))
        db = decm * beta_r
        a = jnp.where(strict, jnp.repeat(kk_ref[sq], rep, axis=0) * db, 0.0)
        pw_ref[sv] = a
        x_ref_[sv] = eye - a
        attn_ref[sv] = (jnp.repeat(qk_ref[sq], rep, axis=0) * db).astype(BF16)
        eg_c = jnp.exp(gam_c)
        rhs_ref[sv, :, dv:dv + dk] = (jnp.repeat(kn_ref[sq], rep, axis=0) * eg_c).astype(BF16)
        qd_ref[sv] = (jnp.repeat(qn_ref[sq], rep, axis=0) * eg_c).astype(BF16)
        kdt_ref[sv] = (jnp.repeat(knt_ref[sq], rep, axis=0) * kdwb3_ref[sv]).astype(BF16)
        for h in range(n_v):
            rhs_ref[g * n_v + h, :, 0:dv] = c_ref[rows, 2 * qk_w + h * dv:2 * qk_w + (h + 1) * dv].astype(BF16)

    n_sq = 5
    for lvl in range(n_sq + 1):
        for i in range(n_ch * n_v):
            pwb = pw_ref[i].astype(BF16)
            if lvl > 0:
                xv = x_ref_[i]
                x_ref_[i] = xv + _dot(xv.astype(BF16), pwb)
            if lvl < n_sq:
                pw_ref[i] = _dot(pwb, pwb)

    for i in range(n_ch * n_v):
        sol_ref[i] = _dot(x_ref_[i].astype(BF16), rhs_ref[i])

    for g in range(n_ch):
        rows = slice(g * L, (g + 1) * L)
        for h in range(n_v):
            sol = sol_ref[g * n_v + h]
            vn_ref[h] = (sol[:, 0:dv] - _dot(sol[:, dv:dv + dk].astype(BF16), s_ref[h].astype(BF16))).astype(BF16)
        for h in range(n_v):
            i = g * n_v + h
            s_st = s_ref[h]
            vnb = vn_ref[h]
            o = _dot(qd_ref[i], s_st.astype(BF16)) + _dot(attn_ref[i], vnb)
            s_ref[h] = gt_ref[i:i + 1, :] * s_st + _dot(kdt_ref[i], vnb)
            rms = lax.rsqrt(jnp.mean(o * o, axis=1, keepdims=True) + RMS_EPS)
            zz = z_ref[rows, h * dv:(h + 1) * dv].astype(F32)
            out_ref[rows, h * dv:(h + 1) * dv] = (o * rms * ng_ref[...] * (zz * _sigmoid(zz))).astype(BF16)


def _gdn(proj, small_t, conv_w, al, dt, norm_g, bsz, seq, cols, n_qk, n_v, ga_off, gb_off, n_ch):
    n = seq // (CHUNK * n_ch)
    L = CHUNK
    R = n_ch * L
    conv_ch = 2 * n_qk * GDN_DK + n_v * GDN_DV
    v_w = n_v * GDN_DV
    t = bsz * seq
    nq, nv = n_ch * n_qk, n_ch * n_v
    kern = functools.partial(_gdn_kernel, n_qk=n_qk, n_v=n_v, ga_off=ga_off, gb_off=gb_off, n_ch=n_ch)
    row = lambda b, c: b * n + c
    full = lambda a: pl.BlockSpec(a.shape, lambda b, c: (0,) * a.ndim)
    return pl.pallas_call(
        kern,
        grid=(bsz, n),
        in_specs=[
            pl.BlockSpec((R, conv_ch), lambda b, c: (row(b, c), cols["gqkv"] // conv_ch)),
            pl.BlockSpec((R, v_w), lambda b, c: (row(b, c), cols["gz"] // v_w)),
            pl.BlockSpec((n_ch, small_t.shape[1], L), lambda b, c: (row(b, c), 0, 0)),
            full(conv_w), full(al), full(dt), full(norm_g),
        ],
        out_specs=pl.BlockSpec((R, v_w), lambda b, c: (row(b, c), 0)),
        out_shape=jax.ShapeDtypeStruct((t, v_w), BF16),
        scratch_shapes=[
            pltpu.VMEM((n_v, GDN_DK, GDN_DV), F32),
            pltpu.VMEM((8 + R + 8, conv_ch), F32),
            pltpu.VMEM((R, conv_ch), F32),
            pltpu.VMEM((nq, L, GDN_DK), F32),
            pltpu.VMEM((nq, L, GDN_DK), F32),
            pltpu.VMEM((nq, GDN_DK, L), F32),
            pltpu.VMEM((nq, L, L), F32),
            pltpu.VMEM((nq, L, L), F32),
            pltpu.VMEM((nv, L, L), F32),
            pltpu.VMEM((nv, L, L), F32),
            pltpu.VMEM((nv, L, GDN_DV + GDN_DK), BF16),
            pltpu.VMEM((nv, L, L), BF16),
            pltpu.VMEM((nv, L, GDN_DK), BF16),
            pltpu.VMEM((nv, GDN_DK, L), BF16),
            pltpu.VMEM((nv, 1, L), F32),
            pltpu.VMEM((nv, 1, L), F32),
            pltpu.VMEM((nv, 1, L), F32),
            pltpu.VMEM((nv, LANES), F32),
            pltpu.VMEM((nv, L, GDN_DV + GDN_DK), F32),
            pltpu.VMEM((n_v, L, GDN_DV), BF16),
        ],
        compiler_params=_params("arbitrary", "arbitrary"),
        name="gdn",
    )(proj, proj, small_t, conv_w, al, dt, norm_g)


def _merge_kernel(hm_ref, og_ref, gm_ref, gg_ref, h0_ref, wbm_ref, wbg_ref, wo_ref, g_ref, b_ref, out_ref):
    y_ml = _dot(hm_ref[...], wbm_ref[...])
    y_gdn = _dot(og_ref[...], wbg_ref[...])
    merged = _sigmoid(gm_ref[...].astype(F32)) * y_ml + _sigmoid(gg_ref[...].astype(F32)) * y_gdn
    mix = _dot(merged.astype(BF16), wo_ref[...])
    out_ref[...] = _layer_norm(DN_ALPHA * h0_ref[...] + mix, g_ref[...], b_ref[...])


def _merge(hm, og, proj, h0, w_bm, w_bg, w_out, g, b, cols, tm):
    t, d = h0.shape
    full = lambda a: pl.BlockSpec(a.shape, lambda i: (0,) * a.ndim)
    return pl.pallas_call(
        _merge_kernel,
        grid=(t // tm,),
        in_specs=[
            pl.BlockSpec((tm, hm.shape[1]), lambda i: (i, 0)),
            pl.BlockSpec((tm, og.shape[1]), lambda i: (i, 0)),
            pl.BlockSpec((tm, d), lambda i: (i, cols["gate_ml"] // d)),
            pl.BlockSpec((tm, d), lambda i: (i, cols["gate_gdn"] // d)),
            pl.BlockSpec((tm, d), lambda i: (i, 0)),
            full(w_bm), full(w_bg), full(w_out), full(g), full(b),
        ],
        out_specs=pl.BlockSpec((tm, d), lambda i: (i, 0)),
        out_shape=jax.ShapeDtypeStruct((t, d), F32),
        compiler_params=_params("arbitrary"),
        name="merge_out_ln1",
    )(hm, og, proj, proj, h0, w_bm, w_bg, w_out, g, b)


def _kv_kernel(m_ref, wk_ref, wv_ref, k_ref, v_ref):
    mb = m_ref[...].astype(BF16)
    k_ref[...] = _dot(mb, wk_ref[...]).astype(BF16)
    v_ref[...] = _dot(mb, wv_ref[...]).astype(BF16)


def _kv_proj(mem2, wk, wv, tm):
    t, d = mem2.shape
    full = lambda a: pl.BlockSpec(a.shape, lambda i: (0,) * a.ndim)
    return pl.pallas_call(
        _kv_kernel,
        grid=(t // tm,),
        in_specs=[pl.BlockSpec((tm, d), lambda i: (i, 0)), full(wk), full(wv)],
        out_specs=[pl.BlockSpec((tm, d), lambda i: (i, 0))] * 2,
        out_shape=[jax.ShapeDtypeStruct((t, d), BF16)] * 2,
        compiler_params=_params("arbitrary"),
        name="mem_kv_proj",
    )(mem2, wk, wv)


def _xattn_kernel(h1_ref, k_ref, v_ref, wq_ref, wo_ref, g_ref, b_ref, wr_ref, br_ref,
                  h2_ref, gw_ref, route_ref, cnt_ref, carry_ref, *, dh):
    tm = h1_ref.shape[0]

    @pl.when((pl.program_id(0) == 0) & (pl.program_id(1) == 0))
    def _():
        carry_ref[...] = jnp.zeros_like(carry_ref)

    h1 = h1_ref[...]
    q = _dot(h1.astype(BF16), wq_ref[...])
    outs = []
    for hd in range(XA_HEADS):
        qh = q[:, hd * dh:(hd + 1) * dh].astype(BF16)
        kh = k_ref[:, hd * dh:(hd + 1) * dh]
        vh = v_ref[:, hd * dh:(hd + 1) * dh]
        sc = _dot_nt(qh, kh) * (dh ** -0.5)
        e = jnp.exp(sc - jnp.max(sc, axis=1, keepdims=True))
        p = e / jnp.sum(e, axis=1, keepdims=True)
        outs.append(_dot(p.astype(BF16), vh))
    o = jnp.concatenate(outs, axis=1)
    xa = _dot(o.astype(BF16), wo_ref[...])
    h2 = _layer_norm(DN_ALPHA * h1 + xa, g_ref[...], b_ref[...])
    h2_ref[...] = h2

    logits = _dot(h2.astype(BF16), wr_ref[...]) + br_ref[...]
    lane = lax.broadcasted_iota(jnp.int32, (tm, LANES), 1)
    lane_f = lane.astype(F32)
    work = logits
    vals, idxs = [], []
    for _ in range(TOP_K):
        m = jnp.max(work, axis=1, keepdims=True)
        idx = jnp.min(jnp.where(work == m, lane_f, float(LANES)), axis=1, keepdims=True)
        vals.append(m)
        idxs.append(idx)
        work = jnp.where(lane_f == idx, -jnp.inf, work)
    es = [jnp.exp(v - vals[0]) for v in vals]
    tot = es[0]
    for e_ in es[1:]:
        tot = tot + e_
    onehot = jnp.zeros((tm, LANES), F32)
    for idx in idxs:
        onehot = onehot + (lane_f == idx).astype(F32)
    ri = lax.broadcasted_iota(jnp.int32, (tm, tm), 0)
    ci = lax.broadcasted_iota(jnp.int32, (tm, tm), 1)
    tri = (ci < ri).astype(BF16)
    carry = carry_ref[0:1, :]
    ranks = carry + _dot(tri, onehot.astype(BF16))
    gw = jnp.zeros((tm, LANES), F32)
    route = jnp.zeros((tm, LANES), F32)
    for k_ in range(TOP_K):
        rk = jnp.sum(jnp.where(lane_f == idxs[k_], ranks, 0.0), axis=1, keepdims=True)
        gw = gw + jnp.where(lane == k_, es[k_] / tot, 0.0)
        route = route + jnp.where(lane == k_, idxs[k_], 0.0) + jnp.where(lane == TOP_K + k_, rk, 0.0)
    gw_ref[...] = gw
    route_ref[...] = route.astype(jnp.int32)
    carry = carry + jnp.sum(onehot, axis=0, keepdims=True)
    carry_ref[...] = jnp.broadcast_to(carry, carry_ref.shape)
    cnt_ref[...] = jnp.broadcast_to(carry, cnt_ref.shape)


def _xattn(h1, kmem, vmem, wq, wo, g, b, w_r, b_r, bsz, seq, mem_len, tm):
    t, d = h1.shape
    nt = seq // tm
    full = lambda a: pl.BlockSpec(a.shape, lambda i, j: (0,) * a.ndim)
    kern = functools.partial(_xattn_kernel, dh=d // XA_HEADS)
    return pl.pallas_call(
        kern,
        grid=(bsz, nt),
        in_specs=[
            pl.BlockSpec((tm, d), lambda i, j: (i * nt + j, 0)),
            pl.BlockSpec((mem_len, d), lambda i, j: (i, 0)),
            pl.BlockSpec((mem_len, d), lambda i, j: (i, 0)),
            full(wq), full(wo), full(g), full(b), full(w_r), full(b_r),
        ],
        out_specs=[
            pl.BlockSpec((tm, d), lambda i, j: (i * nt + j, 0)),
            pl.BlockSpec((tm, LANES), lambda i, j: (i * nt + j, 0)),
            pl.BlockSpec((tm, LANES), lambda i, j: (i * nt + j, 0)),
            pl.BlockSpec((8, LANES), lambda i, j: (0, 0)),
        ],
        out_shape=[
            jax.ShapeDtypeStruct((t, d), F32),
            jax.ShapeDtypeStruct((t, LANES), F32),
            jax.ShapeDtypeStruct((t, LANES), jnp.int32),
            jax.ShapeDtypeStruct((8, LANES), F32),
        ],
        scratch_shapes=[pltpu.VMEM((8, LANES), F32)],
        compiler_params=_params("arbitrary", "arbitrary"),
        name="xattn_ln2_router",
    )(h1, kmem, vmem, wq, wo, g, b, w_r, b_r)


def _dispatch_kernel(pad_lo_ref, pad_n_ref, na_ref, dest_ref, h_ref, xs_ref, zero_ref, sem, zsem):
    tm = h_ref.shape[0]
    step = pl.program_id(0)
    n_blocks = xs_ref.shape[0] // MOE_BLOCK

    @pl.when(step == 0)
    def _():
        zero_ref[...] = jnp.zeros_like(zero_ref)
        zero_row = zero_ref.at[pl.ds(0, 1), :]

        def per_expert(e, carry):
            lo = pad_lo_ref[e]
            n = pad_n_ref[e]

            def start(r, c):
                pltpu.make_async_copy(zero_row, xs_ref.at[pl.ds(lo + r, 1), :], zsem).start()
                return c

            def wait(r, c):
                pltpu.make_async_copy(zero_row, xs_ref.at[pl.ds(0, 1), :], zsem).wait()
                return c

            lax.fori_loop(0, n, start, 0)
            lax.fori_loop(0, n, wait, 0)
            return carry

        lax.fori_loop(0, N_EXPERTS, per_expert, 0)

        def tail(b, carry):
            cp = pltpu.make_async_copy(zero_ref, xs_ref.at[pl.ds(b * MOE_BLOCK, MOE_BLOCK), :], zsem)
            cp.start()
            cp.wait()
            return carry

        lax.fori_loop(na_ref[0], n_blocks, tail, 0)

    for tok in range(tm):
        src = h_ref.at[pl.ds(tok, 1), :]
        for k_ in range(TOP_K):
            pltpu.make_async_copy(src, xs_ref.at[pl.ds(dest_ref[tok * TOP_K + k_], 1), :], sem).start(priority=k_ % 2)
    for _ in range(TOP_K):
        pltpu.make_async_copy(h_ref, xs_ref.at[pl.ds(0, tm), :], sem).wait()


def _dispatch(pad_lo, pad_n, nact, dest_flat, h2, n_rows, tm):
    t, d = h2.shape
    grid_spec = pltpu.PrefetchScalarGridSpec(
        num_scalar_prefetch=3,
        grid=(t // tm,),
        in_specs=[
            pl.BlockSpec((tm * TOP_K,), lambda i, *_: (i,), memory_space=pltpu.SMEM),
            pl.BlockSpec((tm, d), lambda i, *_: (i, 0)),
        ],
        out_specs=pl.BlockSpec(memory_space=pl.ANY),
        scratch_shapes=[pltpu.VMEM((MOE_BLOCK, d), F32), pltpu.SemaphoreType.DMA(()),
                        pltpu.SemaphoreType.DMA(())],
    )
    return pl.pallas_call(
        _dispatch_kernel,
        grid_spec=grid_spec,
        out_shape=jax.ShapeDtypeStruct((n_rows, d), F32),
        compiler_params=_params("arbitrary"),
        name="moe_dispatch",
    )(pad_lo, pad_n, nact, dest_flat, h2)


def _expert_kernel(be_ref, na_ref, x_ref, wgu_ref, bgu_ref, wdn_ref, bdn_ref, y_ref, wgu_s, wdn_s, *, d_exp):
    blk = pl.program_id(0)
    active = blk < na_ref[0]

    @pl.when(jnp.logical_not(active))
    def _():
        y_ref[...] = jnp.zeros_like(y_ref)

    @pl.when(active & ((blk == 0) | (be_ref[blk] != be_ref[jnp.maximum(blk - 1, 0)])))
    def _():
        wgu_s[...] = wgu_ref[0].astype(BF16)
        wdn_s[...] = wdn_ref[0].astype(BF16)

    @pl.when(active)
    def _():
        gu = _dot(x_ref[...].astype(BF16), wgu_s[...]) + bgu_ref[0]
        gate = jnp.minimum(gu[:, :d_exp], SWIGLU_LIMIT)
        up = jnp.clip(gu[:, d_exp:], -SWIGLU_LIMIT, SWIGLU_LIMIT)
        act = (up + 1.0) * (gate * _sigmoid(SWIGLU_ALPHA * gate))
        y_ref[...] = _dot(act.astype(BF16), wdn_s[...]) + bdn_ref[0]


def _experts(block_e, nact, xs, w_gu, b_gu, w_dn, b_dn):
    n_rows, d = xs.shape
    n_blocks = n_rows // MOE_BLOCK
    d_exp = w_dn.shape[1]
    kern = functools.partial(_expert_kernel, d_exp=d_exp)
    emap = lambda i, be, na: (be[i], 0, 0)
    grid_spec = pltpu.PrefetchScalarGridSpec(
        num_scalar_prefetch=2,
        grid=(n_blocks,),
        in_specs=[
            pl.BlockSpec((MOE_BLOCK, d), lambda i, be, na: (i, 0)),
            pl.BlockSpec((1, d, 2 * d_exp), emap),
            pl.BlockSpec((1, 1, 2 * d_exp), emap),
            pl.BlockSpec((1, d_exp, d), emap),
            pl.BlockSpec((1, 1, d), emap),
        ],
        out_specs=pl.BlockSpec((MOE_BLOCK, d), lambda i, be, na: (i, 0)),
        scratch_shapes=[pltpu.VMEM((d, 2 * d_exp), BF16), pltpu.VMEM((d_exp, d), BF16)],
    )
    return pl.pallas_call(
        kern,
        grid_spec=grid_spec,
        out_shape=jax.ShapeDtypeStruct((n_rows, d), F32),
        compiler_params=_params("arbitrary"),
        name="moe_experts",
    )(block_e, nact, xs, w_gu, b_gu, w_dn, b_dn)


def _combine_kernel(dest_ref, dest_nxt_ref, gw_ref, h_ref, g_ref, b_ref, y_ref, out_ref, buf_a, buf_b, sem):
    tm = buf_a.shape[1]
    step = pl.program_id(0)
    n_steps = pl.num_programs(0)

    def issue(idx_ref, half, buf, s):
        for tok in range(tm):
            for k_ in range(TOP_K):
                pltpu.make_async_copy(y_ref.at[pl.ds(idx_ref[(half * tm + tok) * TOP_K + k_], 1), :],
                                      buf.at[k_, pl.ds(tok, 1), :], sem.at[s]).start(priority=k_ % 2)

    def consume(half, buf, s):
        for k_ in range(TOP_K):
            pltpu.make_async_copy(y_ref.at[pl.ds(0, tm), :], buf.at[k_], sem.at[s]).wait()
        rows = slice(half * tm, (half + 1) * tm)
        gw = gw_ref[rows, :]
        ff = buf[0] * gw[:, 0:1]
        for k_ in range(1, TOP_K):
            ff = ff + buf[k_] * gw[:, k_:k_ + 1]
        out_ref[rows, :] = _layer_norm(DN_ALPHA * h_ref[rows, :] + ff, g_ref[...], b_ref[...])

    @pl.when(step == 0)
    def _():
        issue(dest_ref, 0, buf_a, 0)

    issue(dest_ref, 1, buf_b, 1)
    consume(0, buf_a, 0)

    @pl.when(step + 1 < n_steps)
    def _():
        issue(dest_nxt_ref, 0, buf_a, 0)

    consume(1, buf_b, 1)


def _combine(dest_flat, gw, h2, g, b, yb, tm):
    t, d = h2.shape
    n = t // (2 * tm)
    return pl.pallas_call(
        _combine_kernel,
        grid=(n,),
        in_specs=[
            pl.BlockSpec((2 * tm * TOP_K,), lambda i: (i,), memory_space=pltpu.SMEM),
            pl.BlockSpec((2 * tm * TOP_K,), lambda i: (jnp.minimum(i + 1, n - 1),), memory_space=pltpu.SMEM),
            pl.BlockSpec((2 * tm, LANES), lambda i: (i, 0)),
            pl.BlockSpec((2 * tm, d), lambda i: (i, 0)),
            pl.BlockSpec((1, d), lambda i: (0, 0)),
            pl.BlockSpec((1, d), lambda i: (0, 0)),
            pl.BlockSpec(memory_space=pl.ANY),
        ],
        out_specs=pl.BlockSpec((2 * tm, d), lambda i: (i, 0)),
        out_shape=jax.ShapeDtypeStruct((t, d), F32),
        scratch_shapes=[pltpu.VMEM((TOP_K, tm, d), F32), pltpu.VMEM((TOP_K, tm, d), F32),
                        pltpu.SemaphoreType.DMA((2,))],
        compiler_params=_params("arbitrary"),
        name="moe_combine_ln3",
    )(dest_flat, dest_flat, gw, h2, g, b, yb)


def _pick(n, pref):
    return pref if n % pref == 0 else n


def kernel(x, mem, ln_in_g, ln_in_b, w_in, ml_gate_bias, ml_norm_g, gdn_conv_w, gdn_a_log, gdn_dt_bias, gdn_norm_g, w_branch_ml, w_branch_gdn, w_mix_out, ln1_g, ln1_b, xa_wq, xa_wk, xa_wv, xa_wo, ln2_g, ln2_b, w_router, b_router, w_gu, b_gu, w_dn, b_dn, ln3_g, ln3_b):
    bsz, seq, d = x.shape
    mem_len = mem.shape[1]
    t = bsz * seq
    ml_dv = d // ML_HEADS
    ml_dqk = ml_dv // 2
    ml_qk_w, ml_v_w = ML_HEADS * ml_dqk, ML_HEADS * ml_dv
    n_qk = d // GDN_DK
    n_v = 2 * n_qk
    gdn_qk_w, gdn_v_w = n_qk * GDN_DK, n_v * GDN_DV
    conv_ch = 2 * gdn_qk_w + gdn_v_w
    splits = (ml_qk_w, ml_qk_w, ml_v_w, ml_v_w, 2 * ML_HEADS, conv_ch, gdn_v_w, n_v, n_v, d, d)
    names = ("mq", "mk", "mv", "mo", "mif", "gqkv", "gz", "ga", "gb", "gate_ml", "gate_gdn")
    starts = {}
    acc = 0
    for nm, sz in zip(names, splits):
        starts[nm] = (acc, sz)
        acc += sz
    row2 = lambda a: a.reshape(1, -1).astype(F32)

    h = x.reshape(t, d)
    for l in range(DEPTH):
        w = w_in[l]
        seg = lambda nm: w[:, starts[nm][0]:starts[nm][0] + starts[nm][1]]
        order = ("gqkv", "gz", "mv", "mo", "gate_ml", "gate_gdn", "mq", "mk")
        cols = {}
        off = 0
        for nm in order:
            assert off % starts[nm][1] == 0
            cols[nm] = off
            off += starts[nm][1]
        w_big = jnp.concatenate([seg(nm) for nm in order], axis=1).astype(BF16)
        n_small = 2 * ML_HEADS + 2 * n_v
        w_small = jnp.concatenate([seg("mif"), seg("ga"), seg("gb"),
                                   jnp.zeros((d, LANES - n_small), F32)], axis=1).astype(BF16)
        ga_off, gb_off = 2 * ML_HEADS, 2 * ML_HEADS + n_v

        if l == 0:
            h0, proj, small = _ln_proj(h, row2(ln_in_g), row2(ln_in_b), w_big, w_small,
                                       _pick(t, 1024), _pick(off, 2816))
        else:
            raise NotImplementedError("DEPTH > 1")
        n_ch = seq // CHUNK
        small_t = small[:, :n_small].reshape(bsz * n_ch, CHUNK, n_small).transpose(0, 2, 1)

        bias = ml_gate_bias[l].astype(F32)
        hm = _mlstm(proj, small_t, bias.reshape(-1, 1), row2(ml_norm_g[l]),
                    bsz, seq, cols, ml_dqk, ml_dv, ML_CHUNKS_PER_STEP)
        al = gdn_a_log[l].astype(F32)
        dt = gdn_dt_bias[l].astype(F32)
        og = _gdn(proj, small_t, gdn_conv_w[l].astype(F32), al.reshape(-1, 1), dt.reshape(-1, 1),
                  row2(gdn_norm_g[l]), bsz, seq, cols, n_qk, n_v, ga_off, gb_off, GDN_CHUNKS_PER_STEP)
        h1 = _merge(hm, og, proj, h0, w_branch_ml[l].astype(BF16), w_branch_gdn[l].astype(BF16),
                    w_mix_out[l].astype(BF16), row2(ln1_g[l]), row2(ln1_b[l]), cols, _pick(t, 512))

        kmem, vmem = _kv_proj(mem.reshape(bsz * mem_len, d), xa_wk[l].astype(BF16), xa_wv[l].astype(BF16),
                              _pick(bsz * mem_len, 512))
        w_r = jnp.concatenate([w_router[l], jnp.zeros((d, LANES - N_EXPERTS), F32)], axis=1).astype(BF16)
        b_r = jnp.concatenate([b_router[l].astype(F32), jnp.full((LANES - N_EXPERTS,), NEG_BIG, F32)]).reshape(1, -1)
        tm_x = _pick(seq, 512)
        h2, gw, route, cnt = _xattn(h1, kmem, vmem, xa_wq[l].astype(BF16), xa_wo[l].astype(BF16),
                                    row2(ln2_g[l]), row2(ln2_b[l]), w_r, b_r, bsz, seq, mem_len, tm_x)

        counts = cnt[0, :N_EXPERTS].astype(jnp.int32)
        padded = (counts + MOE_BLOCK - 1) // MOE_BLOCK * MOE_BLOCK
        pad_end = jnp.cumsum(padded)
        pad_start = pad_end - padded
        n_asg = t * TOP_K
        n_blocks = -(-n_asg // MOE_BLOCK) + N_EXPERTS
        n_rows = n_blocks * MOE_BLOCK
        top_e = route[:, :TOP_K]
        rank = route[:, TOP_K:2 * TOP_K]
        sel = top_e[:, :, None] == jnp.arange(N_EXPERTS, dtype=jnp.int32)[None, None, :]
        dest = (jnp.sum(jnp.where(sel, pad_start[None, None, :], 0), axis=-1) + rank).astype(jnp.int32)
        dest_flat = dest.reshape(n_asg)
        blk_row = jnp.arange(n_blocks, dtype=jnp.int32) * MOE_BLOCK
        block_e = jnp.minimum(jnp.sum(pad_end[None, :] <= blk_row[:, None], axis=1), N_EXPERTS - 1).astype(jnp.int32)
        nact = (pad_end[-1:] // MOE_BLOCK).astype(jnp.int32)
        pad_lo = (pad_start + counts).astype(jnp.int32)
        pad_n = (padded - counts).astype(jnp.int32)

        tm_d = _pick(t, 256)
        xs = _dispatch(pad_lo, pad_n, nact, dest_flat, h2, n_rows, _pick(t, 512))
        yb = _experts(block_e, nact, xs, w_gu[l], b_gu[l].astype(F32)[:, None, :],
                      w_dn[l], b_dn[l].astype(F32)[:, None, :])
        h = _combine(dest_flat, gw, h2, row2(ln3_g[l]), row2(ln3_b[l]), yb, tm_d)
    return h.reshape(bsz, seq, d)
```

```python
import functools

import jax
import jax.numpy as jnp
from jax import lax
from jax.experimental import pallas as pl
from jax.experimental.pallas import tpu as pltpu

F32 = jnp.float32
BF16 = jnp.bfloat16

CHUNK = 64
ML_HEADS = 4
GDN_DK = 128
GDN_DV = 128
CONV_K = 4
XA_HEADS = 4
N_EXPERTS = 32
TOP_K = 4
SWIGLU_LIMIT = 7.0
SWIGLU_ALPHA = 1.702
MOE_BLOCK = 512
DEPTH = 1
DN_ALPHA = (2 * DEPTH) ** 0.25
LN_EPS = 1e-5
RMS_EPS = 1e-6
LANES = 128
SUBLANES = 8
VMEM_LIMIT = 56 * 1024 * 1024
NEG_BIG = -1e30
ML_CHUNKS_PER_STEP = 8
GDN_CHUNKS_PER_STEP = 4
GDN_CONV_COLS = 512


def _params(*sem):
    return pltpu.CompilerParams(dimension_semantics=sem, vmem_limit_bytes=VMEM_LIMIT)


def _dot(a, b):
    return jnp.dot(a, b, preferred_element_type=F32)


def _dot_nt(a, b):
    return lax.dot_general(a, b, (((1,), (1,)), ((), ())), preferred_element_type=F32)


def _layer_norm(x, g, b):
    mu = jnp.mean(x, axis=-1, keepdims=True)
    xc = x - mu
    var = jnp.mean(xc * xc, axis=-1, keepdims=True)
    return xc * lax.rsqrt(var + LN_EPS) * g + b


def _sigmoid(x):
    return 1.0 / (1.0 + jnp.exp(-x))


def _log_sigmoid(x):
    return jnp.minimum(x, 0.0) - jnp.log(1.0 + jnp.exp(-jnp.abs(x)))


def _softplus(x):
    return jnp.maximum(x, 0.0) + jnp.log(1.0 + jnp.exp(-jnp.abs(x)))


def _split3(x):
    hi = x.astype(BF16)
    r = x - hi.astype(F32)
    mid = r.astype(BF16)
    lo = (r - mid.astype(F32)).astype(BF16)
    return hi, mid, lo


def _cumsum_lanes(x, upper):
    hi, mid, lo = _split3(x)
    return _dot(hi, upper) + _dot(mid, upper) + _dot(lo, upper)


def _ln_proj_kernel(x_ref, g_ref, b_ref, w_ref, ws_ref, h_ref, p_ref, s_ref, xn_ref):
    @pl.when(pl.program_id(1) == 0)
    def _():
        h = _layer_norm(x_ref[...], g_ref[...], b_ref[...])
        h_ref[...] = h
        hb = h.astype(BF16)
        xn_ref[...] = hb
        s_ref[...] = _dot(hb, ws_ref[...])

    p_ref[...] = _dot(xn_ref[...], w_ref[...]).astype(BF16)


def _ln_proj(x2, g, b, w_big, w_small, tm, tn):
    t, d = x2.shape
    nw = w_big.shape[1]
    return pl.pallas_call(
        _ln_proj_kernel,
        grid=(t // tm, nw // tn),
        in_specs=[
            pl.BlockSpec((tm, d), lambda i, j: (i, 0)),
            pl.BlockSpec((1, d), lambda i, j: (0, 0)),
            pl.BlockSpec((1, d), lambda i, j: (0, 0)),
            pl.BlockSpec((d, tn), lambda i, j: (0, j)),
            pl.BlockSpec((d, LANES), lambda i, j: (0, 0)),
        ],
        out_specs=[
            pl.BlockSpec((tm, d), lambda i, j: (i, 0)),
            pl.BlockSpec((tm, tn), lambda i, j: (i, j)),
            pl.BlockSpec((tm, LANES), lambda i, j: (i, 0)),
        ],
        out_shape=[
            jax.ShapeDtypeStruct((t, d), F32),
            jax.ShapeDtypeStruct((t, nw), BF16),
            jax.ShapeDtypeStruct((t, LANES), F32),
        ],
        scratch_shapes=[pltpu.VMEM((tm, d), BF16)],
        compiler_params=_params("arbitrary", "arbitrary"),
        name="ln_in_proj",
    )(x2, g, b, w_big, w_small)


def _mlstm_kernel(q_ref, k_ref, v_ref, o_ref, gr_ref, br_ref, ng_ref,
                  out_ref, c_ref, m_ref, sqk_ref, kt_ref, p_ref, ktw_ref, b3_ref, li3_ref, kw3_ref,
                  gs_ref, mc_ref, stb_ref, stm_ref, std_ref, num_ref, kv_ref, *, dqk, dv, n_ch):
    L = CHUNK
    H = ML_HEADS

    @pl.when(pl.program_id(1) == 0)
    def _():
        c_ref[...] = jnp.zeros_like(c_ref)
        m_ref[...] = jnp.zeros_like(m_ref)

    ii = lax.broadcasted_iota(jnp.int32, (L, L), 0)
    jj = lax.broadcasted_iota(jnp.int32, (L, L), 1)
    causal = jj <= ii
    diag = ii == jj
    upper = (ii <= jj).astype(BF16)
    scale = dqk ** -0.5
    heads = [(g, h) for g in range(n_ch) for h in range(H)]

    for g, h in heads:
        rows = slice(g * L, (g + 1) * L)
        kf = k_ref[rows, h * dqk:(h + 1) * dqk].astype(F32) * scale
        sqk_ref[g * H + h] = _dot_nt(q_ref[rows, h * dqk:(h + 1) * dqk], kf.astype(BF16))
        kt_ref[g * H + h] = kf.T

    pre = gr_ref[:, 0:2 * H, :].reshape(n_ch * 2 * H, L) + jnp.concatenate([br_ref[...]] * n_ch, axis=0)
    b_all = _cumsum_lanes(_log_sigmoid(pre), upper)
    gs_all = b_all[:, L - 1:L]
    a_all = gs_all - b_all + pltpu.roll(pre, H, axis=0)
    mc_all = jnp.max(a_all, axis=1, keepdims=True)
    kw_all = jnp.exp(a_all - mc_all)
    gs_ref[...] = jnp.broadcast_to(gs_all, gs_ref.shape)
    mc_ref[...] = jnp.broadcast_to(mc_all, mc_ref.shape)
    for g, h in heads:
        r = g * 2 * H + H + h
        b3_ref[g * H + h] = b_all[r:r + 1, :]
        li3_ref[g * H + h] = pre[r - H:r - H + 1, :]
        kw3_ref[g * H + h] = kw_all[r:r + 1, :]

    b_r = b3_ref[...]
    b_c = jnp.sum(jnp.where(diag, b_r, 0.0), axis=2, keepdims=True)
    dmat = jnp.where(causal, b_c - b_r + li3_ref[...], -jnp.inf)
    m_intra = jnp.max(dmat, axis=2, keepdims=True)
    p = jnp.exp(dmat - m_intra) * sqk_ref[...]
    p_ref[...] = p.astype(BF16)
    stb_ref[...] = jnp.broadcast_to(b_c, stb_ref.shape)
    stm_ref[...] = jnp.broadcast_to(m_intra, stm_ref.shape)
    std_ref[...] = jnp.broadcast_to(jnp.sum(p, axis=2, keepdims=True), std_ref.shape)
    ktw_ref[...] = (kt_ref[...] * kw3_ref[...]).astype(BF16)

    ones = jnp.ones((L, LANES), BF16)
    for g, h in heads:
        vh = v_ref[g * L:(g + 1) * L, h * dv:(h + 1) * dv]
        num_ref[g * H + h] = _dot(p_ref[g * H + h], vh)
        kv_ref[g * H + h] = _dot(ktw_ref[g * H + h], jnp.concatenate([vh, ones], axis=1))

    rep = dv // LANES
    wide = lambda s: jnp.concatenate([s] * rep, axis=1)
    for g, h in heads:
        i = g * H + h
        r = g * 2 * H + H + h
        rows = slice(g * L, (g + 1) * L)
        c_st = c_ref[h]
        m_st = m_ref[h:h + 1, :]
        qc = _dot(q_ref[rows, h * dqk:(h + 1) * dqk], c_st.astype(BF16))
        m_intra = stm_ref[i]
        inter_log = stb_ref[i] + m_st
        m_out = jnp.maximum(inter_log, m_intra)
        s_inter = jnp.exp(inter_log - m_out)
        s_intra = jnp.exp(m_intra - m_out)
        num = wide(s_inter) * qc[:, 0:dv] + wide(s_intra) * num_ref[i]
        den = s_inter * qc[:, dv:dv + LANES] + s_intra * std_ref[i]
        hh = num / wide(jnp.maximum(jnp.abs(den), jnp.exp(-m_out)))
        gs = gs_ref[r:r + 1, :]
        mc = mc_ref[r:r + 1, :]
        m_new = jnp.maximum(gs + m_st, mc)
        dec = jnp.exp(gs + m_st - m_new)
        s_new = jnp.exp(mc - m_new)
        c_ref[h] = (jnp.concatenate([dec] * (rep + 1), axis=1) * c_st
                    + jnp.concatenate([s_new] * (rep + 1), axis=1) * kv_ref[i])
        m_ref[h:h + 1, :] = m_new

        rms = lax.rsqrt(jnp.mean(hh * hh, axis=1, keepdims=True) + RMS_EPS)
        og = _sigmoid(o_ref[rows, h * dv:(h + 1) * dv].astype(F32))
        out_ref[rows, h * dv:(h + 1) * dv] = (hh * rms * ng_ref[:, h * dv:(h + 1) * dv] * og).astype(BF16)


def _mlstm(proj, small_t, bias_r, norm_g, bsz, seq, cols, dqk, dv, n_ch):
    n = seq // (CHUNK * n_ch)
    L = CHUNK
    R = n_ch * L
    H = ML_HEADS
    qk_w, v_w = H * dqk, H * dv
    t = bsz * seq
    nh = n_ch * H
    kern = functools.partial(_mlstm_kernel, dqk=dqk, dv=dv, n_ch=n_ch)
    row = lambda b, c: b * n + c
    return pl.pallas_call(
        kern,
        grid=(bsz, n),
        in_specs=[
            pl.BlockSpec((R, qk_w), lambda b, c: (row(b, c), cols["mq"] // qk_w)),
            pl.BlockSpec((R, qk_w), lambda b, c: (row(b, c), cols["mk"] // qk_w)),
            pl.BlockSpec((R, v_w), lambda b, c: (row(b, c), cols["mv"] // v_w)),
            pl.BlockSpec((R, v_w), lambda b, c: (row(b, c), cols["mo"] // v_w)),
            pl.BlockSpec((n_ch, small_t.shape[1], L), lambda b, c: (row(b, c), 0, 0)),
            pl.BlockSpec(bias_r.shape, lambda b, c: (0, 0)),
            pl.BlockSpec((1, v_w), lambda b, c: (0, 0)),
        ],
        out_specs=pl.BlockSpec((R, v_w), lambda b, c: (row(b, c), 0)),
        out_shape=jax.ShapeDtypeStruct((t, v_w), BF16),
        scratch_shapes=[
            pltpu.VMEM((H, dqk, dv + LANES), F32),
            pltpu.VMEM((8, LANES), F32),
            pltpu.VMEM((nh, L, L), F32),
            pltpu.VMEM((nh, dqk, L), F32),
            pltpu.VMEM((nh, L, L), BF16),
            pltpu.VMEM((nh, dqk, L), BF16),
            pltpu.VMEM((nh, 1, L), F32),
            pltpu.VMEM((nh, 1, L), F32),
            pltpu.VMEM((nh, 1, L), F32),
            pltpu.VMEM((n_ch * 2 * H, LANES), F32),
            pltpu.VMEM((n_ch * 2 * H, LANES), F32),
            pltpu.VMEM((nh, L, LANES), F32),
            pltpu.VMEM((nh, L, LANES), F32),
            pltpu.VMEM((nh, L, LANES), F32),
            pltpu.VMEM((nh, L, dv), F32),
            pltpu.VMEM((nh, dqk, dv + LANES), F32),
        ],
        compiler_params=_params("arbitrary", "arbitrary"),
        name="mlstm",
    )(proj, proj, proj, proj, small_t, bias_r, norm_g)


def _gdn_kernel(x_ref, z_ref, gr_ref, cw_ref, al_ref, dt_ref, ng_ref,
                out_ref, s_ref, xs_ref, c_ref, qn_ref, kn_ref, knt_ref, kk_ref, qk_ref, pw_ref, x_ref_, rhs_ref,
                attn_ref, qd_ref, kdt_ref, gam3_ref, beta3_ref, kdwb3_ref, gt_ref, sol_ref, vn_ref,
                *, n_qk, n_v, ga_off, gb_off, n_ch):
    L = CHUNK
    R = n_ch * L
    dk, dv = GDN_DK, GDN_DV
    qk_w = n_qk * dk
    rep = n_v // n_qk
    conv_ch = 2 * qk_w + n_v * dv

    @pl.when(pl.program_id(1) == 0)
    def _():
        s_ref[...] = jnp.zeros_like(s_ref)
        xs_ref[0:8, :] = jnp.zeros((8, xs_ref.shape[1]), F32)

    xs_ref[8:8 + R, :] = x_ref[...].astype(F32)
    for cb in range(0, conv_ch, GDN_CONV_COLS):
        cs = slice(cb, cb + GDN_CONV_COLS)
        conv = cw_ref[0:1, cs] * xs_ref[8 - (CONV_K - 1):8 - (CONV_K - 1) + R, cs]
        for j in range(1, CONV_K):
            off = 8 - (CONV_K - 1) + j
            conv = conv + cw_ref[j:j + 1, cs] * xs_ref[off:off + R, cs]
        c_ref[:, cs] = conv * _sigmoid(conv)
    xs_ref[0:8, :] = xs_ref[R:R + 8, :]

    ii = lax.broadcasted_iota(jnp.int32, (L, L), 0)
    jj = lax.broadcasted_iota(jnp.int32, (L, L), 1)
    incl = jj <= ii
    strict = jj < ii
    diag = ii == jj
    eye = diag.astype(F32)
    upper = (ii <= jj).astype(BF16)

    for g in range(n_ch):
        rows = slice(g * L, (g + 1) * L)
        for hk in range(n_qk):
            iq = g * n_qk + hk
            cq = c_ref[rows, hk * dk:(hk + 1) * dk]
            ck = c_ref[rows, qk_w + hk * dk:qk_w + (hk + 1) * dk]
            qn = cq * lax.rsqrt(jnp.sum(cq * cq, axis=1, keepdims=True) + RMS_EPS) * (dk ** -0.5)
            kn = ck * lax.rsqrt(jnp.sum(ck * ck, axis=1, keepdims=True) + RMS_EPS)
            qn_ref[iq] = qn
            kn_ref[iq] = kn
            knt_ref[iq] = kn.T
            kb = kn.astype(BF16)
            kk_ref[iq] = _dot_nt(kb, kb)
            qk_ref[iq] = _dot_nt(qn.astype(BF16), kb)

    for g in range(n_ch):
        gd = -jnp.exp(al_ref[...]) * _softplus(gr_ref[g, ga_off:ga_off + n_v, :] + dt_ref[...])
        gam = _cumsum_lanes(gd, upper)
        beta = _sigmoid(gr_ref[g, gb_off:gb_off + n_v, :])
        g_tot = gam[:, L - 1:L]
        kdwb = jnp.exp(g_tot - gam) * beta
        gt_ref[g * n_v:(g + 1) * n_v, :] = jnp.broadcast_to(jnp.exp(g_tot), (n_v, LANES))
        for h in range(n_v):
            gam3_ref[g * n_v + h] = gam[h:h + 1, :]
            beta3_ref[g * n_v + h] = beta[h:h + 1, :]
            kdwb3_ref[g * n_v + h] = kdwb[h:h + 1, :]

    for g in range(n_ch):
        sv = slice(g * n_v, (g + 1) * n_v)
        sq = slice(g * n_qk, (g + 1) * n_qk)
        rows = slice(g * L, (g + 1) * L)
        gam_r = gam3_ref[sv]
        beta_r = beta3_ref[sv]
        gam_c = jnp.sum(jnp.where(diag, gam_r, 0.0), axis=2, keepdims=True)
        decm = jnp.exp(jnp.where(incl, gam_c - gam_r, -jnp.inf))
        db = decm * beta_r
        a = jnp.where(strict, jnp.repeat(kk_ref[sq], rep, axis=0) * db, 0.0)
        pw_ref[sv] = a
        x_ref_[sv] = eye - a
        attn_ref[sv] = (jnp.repeat(qk_ref[sq], rep, axis=0) * db).astype(BF16)
        eg_c = jnp.exp(gam_c)
        rhs_ref[sv, :, dv:dv + dk] = (jnp.repeat(kn_ref[sq], rep, axis=0) * eg_c).astype(BF16)
        qd_ref[sv] = (jnp.repeat(qn_ref[sq], rep, axis=0) * eg_c).astype(BF16)
        kdt_ref[sv] = (jnp.repeat(knt_ref[sq], rep, axis=0) * kdwb3_ref[sv]).astype(BF16)
        for h in range(n_v):
            rhs_ref[g * n_v + h, :, 0:dv] = c_ref[rows, 2 * qk_w + h * dv:2 * qk_w + (h + 1) * dv].astype(BF16)

    n_sq = 5
    for lvl in range(n_sq + 1):
        for i in range(n_ch * n_v):
            pwb = pw_ref[i].astype(BF16)
            if lvl > 0:
                xv = x_ref_[i]
                x_ref_[i] = xv + _dot(xv.astype(BF16), pwb)
            if lvl < n_sq:
                pw_ref[i] = _dot(pwb, pwb)

    for i in range(n_ch * n_v):
        sol_ref[i] = _dot(x_ref_[i].astype(BF16), rhs_ref[i])

    for g in range(n_ch):
        rows = slice(g * L, (g + 1) * L)
        for h in range(n_v):
            sol = sol_ref[g * n_v + h]
            vn_ref[h] = (sol[:, 0:dv] - _dot(sol[:, dv:dv + dk].astype(BF16), s_ref[h].astype(BF16))).astype(BF16)
        for h in range(n_v):
            i = g * n_v + h
            s_st = s_ref[h]
            vnb = vn_ref[h]
            o = _dot(qd_ref[i], s_st.astype(BF16)) + _dot(attn_ref[i], vnb)
            s_ref[h] = gt_ref[i:i + 1, :] * s_st + _dot(kdt_ref[i], vnb)
            rms = lax.rsqrt(jnp.mean(o * o, axis=1, keepdims=True) + RMS_EPS)
            zz = z_ref[rows, h * dv:(h + 1) * dv].astype(F32)
            out_ref[rows, h * dv:(h + 1) * dv] = (o * rms * ng_ref[...] * (zz * _sigmoid(zz))).astype(BF16)


def _gdn(proj, small_t, conv_w, al, dt, norm_g, bsz, seq, cols, n_qk, n_v, ga_off, gb_off, n_ch):
    n = seq // (CHUNK * n_ch)
    L = CHUNK
    R = n_ch * L
    conv_ch = 2 * n_qk * GDN_DK + n_v * GDN_DV
    v_w = n_v * GDN_DV
    t = bsz * seq
    nq, nv = n_ch * n_qk, n_ch * n_v
    kern = functools.partial(_gdn_kernel, n_qk=n_qk, n_v=n_v, ga_off=ga_off, gb_off=gb_off, n_ch=n_ch)
    row = lambda b, c: b * n + c
    full = lambda a: pl.BlockSpec(a.shape, lambda b, c: (0,) * a.ndim)
    return pl.pallas_call(
        kern,
        grid=(bsz, n),
        in_specs=[
            pl.BlockSpec((R, conv_ch), lambda b, c: (row(b, c), cols["gqkv"] // conv_ch)),
            pl.BlockSpec((R, v_w), lambda b, c: (row(b, c), cols["gz"] // v_w)),
            pl.BlockSpec((n_ch, small_t.shape[1], L), lambda b, c: (row(b, c), 0, 0)),
            full(conv_w), full(al), full(dt), full(norm_g),
        ],
        out_specs=pl.BlockSpec((R, v_w), lambda b, c: (row(b, c), 0)),
        out_shape=jax.ShapeDtypeStruct((t, v_w), BF16),
        scratch_shapes=[
            pltpu.VMEM((n_v, GDN_DK, GDN_DV), F32),
            pltpu.VMEM((8 + R + 8, conv_ch), F32),
            pltpu.VMEM((R, conv_ch), F32),
            pltpu.VMEM((nq, L, GDN_DK), F32),
            pltpu.VMEM((nq, L, GDN_DK), F32),
            pltpu.VMEM((nq, GDN_DK, L), F32),
            pltpu.VMEM((nq, L, L), F32),
            pltpu.VMEM((nq, L, L), F32),
            pltpu.VMEM((nv, L, L), F32),
            pltpu.VMEM((nv, L, L), F32),
            pltpu.VMEM((nv, L, GDN_DV + GDN_DK), BF16),
            pltpu.VMEM((nv, L, L), BF16),
            pltpu.VMEM((nv, L, GDN_DK), BF16),
            pltpu.VMEM((nv, GDN_DK, L), BF16),
            pltpu.VMEM((nv, 1, L), F32),
            pltpu.VMEM((nv, 1, L), F32),
            pltpu.VMEM((nv, 1, L), F32),
            pltpu.VMEM((nv, LANES), F32),
            pltpu.VMEM((nv, L, GDN_DV + GDN_DK), F32),
            pltpu.VMEM((n_v, L, GDN_DV), BF16),
        ],
        compiler_params=_params("arbitrary", "arbitrary"),
        name="gdn",
    )(proj, proj, small_t, conv_w, al, dt, norm_g)


def _merge_kernel(hm_ref, og_ref, gm_ref, gg_ref, h0_ref, wbm_ref, wbg_ref, wo_ref, g_ref, b_ref, out_ref):
    y_ml = _dot(hm_ref[...], wbm_ref[...])
    y_gdn = _dot(og_ref[...], wbg_ref[...])
    merged = _sigmoid(gm_ref[...].astype(F32)) * y_ml + _sigmoid(gg_ref[...].astype(F32)) * y_gdn
    mix = _dot(merged.astype(BF16), wo_ref[...])
    out_ref[...] = _layer_norm(DN_ALPHA * h0_ref[...] + mix, g_ref[...], b_ref[...])


def _merge(hm, og, proj, h0, w_bm, w_bg, w_out, g, b, cols, tm):
    t, d = h0.shape
    full = lambda a: pl.BlockSpec(a.shape, lambda i: (0,) * a.ndim)
    return pl.pallas_call(
        _merge_kernel,
        grid=(t // tm,),
        in_specs=[
            pl.BlockSpec((tm, hm.shape[1]), lambda i: (i, 0)),
            pl.BlockSpec((tm, og.shape[1]), lambda i: (i, 0)),
            pl.BlockSpec((tm, d), lambda i: (i, cols["gate_ml"] // d)),
            pl.BlockSpec((tm, d), lambda i: (i, cols["gate_gdn"] // d)),
            pl.BlockSpec((tm, d), lambda i: (i, 0)),
            full(w_bm), full(w_bg), full(w_out), full(g), full(b),
        ],
        out_specs=pl.BlockSpec((tm, d), lambda i: (i, 0)),
        out_shape=jax.ShapeDtypeStruct((t, d), F32),
        compiler_params=_params("arbitrary"),
        name="merge_out_ln1",
    )(hm, og, proj, proj, h0, w_bm, w_bg, w_out, g, b)


def _kv_kernel(m_ref, wk_ref, wv_ref, k_ref, v_ref):
    mb = m_ref[...].astype(BF16)
    k_ref[...] = _dot(mb, wk_ref[...]).astype(BF16)
    v_ref[...] = _dot(mb, wv_ref[...]).astype(BF16)


def _kv_proj(mem2, wk, wv, tm):
    t, d = mem2.shape
    full = lambda a: pl.BlockSpec(a.shape, lambda i: (0,) * a.ndim)
    return pl.pallas_call(
        _kv_kernel,
        grid=(t // tm,),
        in_specs=[pl.BlockSpec((tm, d), lambda i: (i, 0)), full(wk), full(wv)],
        out_specs=[pl.BlockSpec((tm, d), lambda i: (i, 0))] * 2,
        out_shape=[jax.ShapeDtypeStruct((t, d), BF16)] * 2,
        compiler_params=_params("arbitrary"),
        name="mem_kv_proj",
    )(mem2, wk, wv)


def _xattn_kernel(h1_ref, k_ref, v_ref, wq_ref, wo_ref, g_ref, b_ref, wr_ref, br_ref,
                  h2_ref, gw_ref, route_ref, cnt_ref, carry_ref, *, dh):
    tm = h1_ref.shape[0]

    @pl.when((pl.program_id(0) == 0) & (pl.program_id(1) == 0))
    def _():
        carry_ref[...] = jnp.zeros_like(carry_ref)

    h1 = h1_ref[...]
    q = _dot(h1.astype(BF16), wq_ref[...])
    outs = []
    for hd in range(XA_HEADS):
        qh = q[:, hd * dh:(hd + 1) * dh].astype(BF16)
        kh = k_ref[:, hd * dh:(hd + 1) * dh]
        vh = v_ref[:, hd * dh:(hd + 1) * dh]
        sc = _dot_nt(qh, kh) * (dh ** -0.5)
        e = jnp.exp(sc - jnp.max(sc, axis=1, keepdims=True))
        p = e / jnp.sum(e, axis=1, keepdims=True)
        outs.append(_dot(p.astype(BF16), vh))
    o = jnp.concatenate(outs, axis=1)
    xa = _dot(o.astype(BF16), wo_ref[...])
    h2 = _layer_norm(DN_ALPHA * h1 + xa, g_ref[...], b_ref[...])
    h2_ref[...] = h2

    logits = _dot(h2.astype(BF16), wr_ref[...]) + br_ref[...]
    lane = lax.broadcasted_iota(jnp.int32, (tm, LANES), 1)
    lane_f = lane.astype(F32)
    work = logits
    vals, idxs = [], []
    for _ in range(TOP_K):
        m = jnp.max(work, axis=1, keepdims=True)
        idx = jnp.min(jnp.where(work == m, lane_f, float(LANES)), axis=1, keepdims=True)
        vals.append(m)
        idxs.append(idx)
        work = jnp.where(lane_f == idx, -jnp.inf, work)
    es = [jnp.exp(v - vals[0]) for v in vals]
    tot = es[0]
    for e_ in es[1:]:
        tot = tot + e_
    onehot = jnp.zeros((tm, LANES), F32)
    for idx in idxs:
        onehot = onehot + (lane_f == idx).astype(F32)
    ri = lax.broadcasted_iota(jnp.int32, (tm, tm), 0)
    ci = lax.broadcasted_iota(jnp.int32, (tm, tm), 1)
    tri = (ci < ri).astype(BF16)
    carry = carry_ref[0:1, :]
    ranks = carry + _dot(tri, onehot.astype(BF16))
    gw = jnp.zeros((tm, LANES), F32)
    route = jnp.zeros((tm, LANES), F32)
    for k_ in range(TOP_K):
        rk = jnp.sum(jnp.where(lane_f == idxs[k_], ranks, 0.0), axis=1, keepdims=True)
        gw = gw + jnp.where(lane == k_, es[k_] / tot, 0.0)
        route = route + jnp.where(lane == k_, idxs[k_], 0.0) + jnp.where(lane == TOP_K + k_, rk, 0.0)
    gw_ref[...] = gw
    route_ref[...] = route.astype(jnp.int32)
    carry = carry + jnp.sum(onehot, axis=0, keepdims=True)
    carry_ref[...] = jnp.broadcast_to(carry, carry_ref.shape)
    cnt_ref[...] = jnp.broadcast_to(carry, cnt_ref.shape)


def _xattn(h1, kmem, vmem, wq, wo, g, b, w_r, b_r, bsz, seq, mem_len, tm):
    t, d = h1.shape
    nt = seq // tm
    full = lambda a: pl.BlockSpec(a.shape, lambda i, j: (0,) * a.ndim)
    kern = functools.partial(_xattn_kernel, dh=d // XA_HEADS)
    return pl.pallas_call(
        kern,
        grid=(bsz, nt),
        in_specs=[
            pl.BlockSpec((tm, d), lambda i, j: (i * nt + j, 0)),
            pl.BlockSpec((mem_len, d), lambda i, j: (i, 0)),
            pl.BlockSpec((mem_len, d), lambda i, j: (i, 0)),
            full(wq), full(wo), full(g), full(b), full(w_r), full(b_r),
        ],
        out_specs=[
            pl.BlockSpec((tm, d), lambda i, j: (i * nt + j, 0)),
            pl.BlockSpec((tm, LANES), lambda i, j: (i * nt + j, 0)),
            pl.BlockSpec((tm, LANES), lambda i, j: (i * nt + j, 0)),
            pl.BlockSpec((8, LANES), lambda i, j: (0, 0)),
        ],
        out_shape=[
            jax.ShapeDtypeStruct((t, d), F32),
            jax.ShapeDtypeStruct((t, LANES), F32),
            jax.ShapeDtypeStruct((t, LANES), jnp.int32),
            jax.ShapeDtypeStruct((8, LANES), F32),
        ],
        scratch_shapes=[pltpu.VMEM((8, LANES), F32)],
        compiler_params=_params("arbitrary", "arbitrary"),
        name="xattn_ln2_router",
    )(h1, kmem, vmem, wq, wo, g, b, w_r, b_r)


def _pack_halves(x):
    half = x.shape[1] // 2
    bits = lambda v: lax.bitcast_convert_type(v.astype(BF16).astype(F32), jnp.uint32)
    return (bits(x[:, :half]) & jnp.uint32(0xFFFF0000)) | (bits(x[:, half:]) >> 16)


def _unpack_halves(w):
    first = lax.bitcast_convert_type(w & jnp.uint32(0xFFFF0000), F32)
    second = lax.bitcast_convert_type(w << 16, F32)
    return first, second


def _dispatch_kernel(pad_lo_ref, pad_n_ref, na_ref, dest_ref, hf_ref, xs_ref, zero_ref, h_ref, sem, zsem):
    tm = hf_ref.shape[0]
    step = pl.program_id(0)
    n_blocks = xs_ref.shape[0] // MOE_BLOCK
    h_ref[...] = _pack_halves(hf_ref[...])

    @pl.when(step == 0)
    def _():
        zero_ref[...] = jnp.zeros_like(zero_ref)

        def per_expert(e, carry):
            n = pad_n_ref[e]
            pos = pad_lo_ref[e] + n

            def zero_rows(start, p):
                cp = pltpu.make_async_copy(zero_ref.at[pl.ds(0, p), :], xs_ref.at[pl.ds(start, p), :], zsem)
                cp.start()
                cp.wait()

            p = MOE_BLOCK // 2
            while p >= SUBLANES:

                @pl.when((n & p) != 0)
                def _(p=p, pos=pos):
                    zero_rows(pl.multiple_of(pos - p, SUBLANES), p)

                pos = pos - (n & p)
                p //= 2
            for r in range(SUBLANES - 1):

                @pl.when(r < (n & (SUBLANES - 1)))
                def _(r=r, pos=pos):
                    zero_rows(pos - 1 - r, 1)

            return carry

        lax.fori_loop(0, N_EXPERTS, per_expert, 0)

        def tail(b, carry):
            half = MOE_BLOCK // 2
            cp = pltpu.make_async_copy(zero_ref, xs_ref.at[pl.ds(b * half, half), :], zsem)
            cp.start()
            cp.wait()
            return carry

        lax.fori_loop(2 * na_ref[0], 2 * n_blocks, tail, 0)

    for tok in range(tm):
        src = h_ref.at[pl.ds(tok, 1), :]
        for k_ in range(TOP_K):
            pltpu.make_async_copy(src, xs_ref.at[pl.ds(dest_ref[tok * TOP_K + k_], 1), :], sem).start(priority=k_ % 2)
    for _ in range(TOP_K):
        pltpu.make_async_copy(h_ref, xs_ref.at[pl.ds(0, tm), :], sem).wait()


def _dispatch(pad_lo, pad_n, nact, dest_flat, h2, n_rows, tm):
    t, d = h2.shape
    grid_spec = pltpu.PrefetchScalarGridSpec(
        num_scalar_prefetch=3,
        grid=(t // tm,),
        in_specs=[
            pl.BlockSpec((tm * TOP_K,), lambda i, *_: (i,), memory_space=pltpu.SMEM),
            pl.BlockSpec((tm, d), lambda i, *_: (i, 0)),
        ],
        out_specs=pl.BlockSpec(memory_space=pl.ANY),
        scratch_shapes=[pltpu.VMEM((MOE_BLOCK // 2, d // 2), jnp.uint32), pltpu.VMEM((tm, d // 2), jnp.uint32),
                        pltpu.SemaphoreType.DMA(()), pltpu.SemaphoreType.DMA(())],
    )
    return pl.pallas_call(
        _dispatch_kernel,
        grid_spec=grid_spec,
        out_shape=jax.ShapeDtypeStruct((n_rows, d // 2), jnp.uint32),
        compiler_params=_params("arbitrary"),
        name="moe_dispatch",
    )(pad_lo, pad_n, nact, dest_flat, h2)


def _expert_kernel(be_ref, na_ref, x_ref, wgu_ref, bgu_ref, wdn_ref, bdn_ref, y_ref, wgu_s, wdn_s, *, d_exp):
    blk = pl.program_id(0)
    active = blk < na_ref[0]

    @pl.when(jnp.logical_not(active))
    def _():
        y_ref[...] = jnp.zeros_like(y_ref)

    @pl.when(active & ((blk == 0) | (be_ref[blk] != be_ref[jnp.maximum(blk - 1, 0)])))
    def _():
        wgu_s[...] = wgu_ref[0].astype(BF16)
        wdn_s[...] = wdn_ref[0].astype(BF16)

    @pl.when(active)
    def _():
        x_lo, x_hi = _unpack_halves(x_ref[...])
        gu = _dot(jnp.concatenate([x_lo, x_hi], axis=1).astype(BF16), wgu_s[...]) + bgu_ref[0]
        gate = jnp.minimum(gu[:, :d_exp], SWIGLU_LIMIT)
        up = jnp.clip(gu[:, d_exp:], -SWIGLU_LIMIT, SWIGLU_LIMIT)
        act = (up + 1.0) * (gate * _sigmoid(SWIGLU_ALPHA * gate))
        y_ref[...] = _pack_halves(_dot(act.astype(BF16), wdn_s[...]) + bdn_ref[0])


def _experts(block_e, nact, xs, w_gu, b_gu, w_dn, b_dn):
    n_rows = xs.shape[0]
    d = w_gu.shape[1]
    n_blocks = n_rows // MOE_BLOCK
    d_exp = w_dn.shape[1]
    kern = functools.partial(_expert_kernel, d_exp=d_exp)
    emap = lambda i, be, na: (be[i], 0, 0)
    grid_spec = pltpu.PrefetchScalarGridSpec(
        num_scalar_prefetch=2,
        grid=(n_blocks,),
        in_specs=[
            pl.BlockSpec((MOE_BLOCK, d // 2), lambda i, be, na: (i, 0)),
            pl.BlockSpec((1, d, 2 * d_exp), emap),
            pl.BlockSpec((1, 1, 2 * d_exp), emap),
            pl.BlockSpec((1, d_exp, d), emap),
            pl.BlockSpec((1, 1, d), emap),
        ],
        out_specs=pl.BlockSpec((MOE_BLOCK, d // 2), lambda i, be, na: (i, 0)),
        scratch_shapes=[pltpu.VMEM((d, 2 * d_exp), BF16), pltpu.VMEM((d_exp, d), BF16)],
    )
    return pl.pallas_call(
        kern,
        grid_spec=grid_spec,
        out_shape=jax.ShapeDtypeStruct((n_rows, d // 2), jnp.uint32),
        compiler_params=_params("arbitrary"),
        name="moe_experts",
    )(block_e, nact, xs, w_gu, b_gu, w_dn, b_dn)


def _combine_kernel(dest_ref, dest_nxt_ref, gw_ref, h_ref, g_ref, b_ref, y_ref, out_ref, buf_a, buf_b, sem):
    tm = buf_a.shape[1]
    step = pl.program_id(0)
    n_steps = pl.num_programs(0)

    def issue(idx_ref, half, buf, s):
        for tok in range(tm):
            for k_ in range(TOP_K):
                pltpu.make_async_copy(y_ref.at[pl.ds(idx_ref[(half * tm + tok) * TOP_K + k_], 1), :],
                                      buf.at[k_, pl.ds(tok, 1), :], sem.at[s]).start(priority=k_ % 2)

    def consume(half, buf, s):
        for k_ in range(TOP_K):
            pltpu.make_async_copy(y_ref.at[pl.ds(0, tm), :], buf.at[k_], sem.at[s]).wait()
        rows = slice(half * tm, (half + 1) * tm)
        gw = gw_ref[rows, :]
        lo, hi = _unpack_halves(buf[0])
        ff_lo, ff_hi = lo * gw[:, 0:1], hi * gw[:, 0:1]
        for k_ in range(1, TOP_K):
            lo, hi = _unpack_halves(buf[k_])
            ff_lo, ff_hi = ff_lo + lo * gw[:, k_:k_ + 1], ff_hi + hi * gw[:, k_:k_ + 1]
        ff = jnp.concatenate([ff_lo, ff_hi], axis=1)
        out_ref[rows, :] = _layer_norm(DN_ALPHA * h_ref[rows, :] + ff, g_ref[...], b_ref[...])

    @pl.when(step == 0)
    def _():
        issue(dest_ref, 0, buf_a, 0)

    issue(dest_ref, 1, buf_b, 1)
    consume(0, buf_a, 0)

    @pl.when(step + 1 < n_steps)
    def _():
        issue(dest_nxt_ref, 0, buf_a, 0)

    consume(1, buf_b, 1)


def _combine(dest_flat, gw, h2, g, b, yb, tm):
    t, d = h2.shape
    n = t // (2 * tm)
    return pl.pallas_call(
        _combine_kernel,
        grid=(n,),
        in_specs=[
            pl.BlockSpec((2 * tm * TOP_K,), lambda i: (i,), memory_space=pltpu.SMEM),
            pl.BlockSpec((2 * tm * TOP_K,), lambda i: (jnp.minimum(i + 1, n - 1),), memory_space=pltpu.SMEM),
            pl.BlockSpec((2 * tm, LANES), lambda i: (i, 0)),
            pl.BlockSpec((2 * tm, d), lambda i: (i, 0)),
            pl.BlockSpec((1, d), lambda i: (0, 0)),
            pl.BlockSpec((1, d), lambda i: (0, 0)),
            pl.BlockSpec(memory_space=pl.ANY),
        ],
        out_specs=pl.BlockSpec((2 * tm, d), lambda i: (i, 0)),
        out_shape=jax.ShapeDtypeStruct((t, d), F32),
        scratch_shapes=[pltpu.VMEM((TOP_K, tm, d // 2), jnp.uint32), pltpu.VMEM((TOP_K, tm, d // 2), jnp.uint32),
                        pltpu.SemaphoreType.DMA((2,))],
        compiler_params=_params("arbitrary"),
        name="moe_combine_ln3",
    )(dest_flat, dest_flat, gw, h2, g, b, yb)


def _pick(n, pref):
    return pref if n % pref == 0 else n


def kernel(x, mem, ln_in_g, ln_in_b, w_in, ml_gate_bias, ml_norm_g, gdn_conv_w, gdn_a_log, gdn_dt_bias, gdn_norm_g, w_branch_ml, w_branch_gdn, w_mix_out, ln1_g, ln1_b, xa_wq, xa_wk, xa_wv, xa_wo, ln2_g, ln2_b, w_router, b_router, w_gu, b_gu, w_dn, b_dn, ln3_g, ln3_b):
    bsz, seq, d = x.shape
    mem_len = mem.shape[1]
    t = bsz * seq
    ml_dv = d // ML_HEADS
    ml_dqk = ml_dv // 2
    ml_qk_w, ml_v_w = ML_HEADS * ml_dqk, ML_HEADS * ml_dv
    n_qk = d // GDN_DK
    n_v = 2 * n_qk
    gdn_qk_w, gdn_v_w = n_qk * GDN_DK, n_v * GDN_DV
    conv_ch = 2 * gdn_qk_w + gdn_v_w
    splits = (ml_qk_w, ml_qk_w, ml_v_w, ml_v_w, 2 * ML_HEADS, conv_ch, gdn_v_w, n_v, n_v, d, d)
    names = ("mq", "mk", "mv", "mo", "mif", "gqkv", "gz", "ga", "gb", "gate_ml", "gate_gdn")
    starts = {}
    acc = 0
    for nm, sz in zip(names, splits):
        starts[nm] = (acc, sz)
        acc += sz
    row2 = lambda a: a.reshape(1, -1).astype(F32)

    h = x.reshape(t, d)
    for l in range(DEPTH):
        w = w_in[l]
        seg = lambda nm: w[:, starts[nm][0]:starts[nm][0] + starts[nm][1]]
        order = ("gqkv", "gz", "mv", "mo", "gate_ml", "gate_gdn", "mq", "mk")
        cols = {}
        off = 0
        for nm in order:
            assert off % starts[nm][1] == 0
            cols[nm] = off
            off += starts[nm][1]
        w_big = jnp.concatenate([seg(nm) for nm in order], axis=1).astype(BF16)
        n_small = 2 * ML_HEADS + 2 * n_v
        w_small = jnp.concatenate([seg("mif"), seg("ga"), seg("gb"),
                                   jnp.zeros((d, LANES - n_small), F32)], axis=1).astype(BF16)
        ga_off, gb_off = 2 * ML_HEADS, 2 * ML_HEADS + n_v

        if l == 0:
            h0, proj, small = _ln_proj(h, row2(ln_in_g), row2(ln_in_b), w_big, w_small,
                                       _pick(t, 1024), _pick(off, 2816))
        else:
            raise NotImplementedError("DEPTH > 1")
        n_ch = seq // CHUNK
        small_t = small[:, :n_small].reshape(bsz * n_ch, CHUNK, n_small).transpose(0, 2, 1)

        bias = ml_gate_bias[l].astype(F32)
        hm = _mlstm(proj, small_t, bias.reshape(-1, 1), row2(ml_norm_g[l]),
                    bsz, seq, cols, ml_dqk, ml_dv, ML_CHUNKS_PER_STEP)
        al = gdn_a_log[l].astype(F32)
        dt = gdn_dt_bias[l].astype(F32)
        og = _gdn(proj, small_t, gdn_conv_w[l].astype(F32), al.reshape(-1, 1), dt.reshape(-1, 1),
                  row2(gdn_norm_g[l]), bsz, seq, cols, n_qk, n_v, ga_off, gb_off, GDN_CHUNKS_PER_STEP)
        h1 = _merge(hm, og, proj, h0, w_branch_ml[l].astype(BF16), w_branch_gdn[l].astype(BF16),
                    w_mix_out[l].astype(BF16), row2(ln1_g[l]), row2(ln1_b[l]), cols, _pick(t, 512))

        kmem, vmem = _kv_proj(mem.reshape(bsz * mem_len, d), xa_wk[l].astype(BF16), xa_wv[l].astype(BF16),
                              _pick(bsz * mem_len, 512))
        w_r = jnp.concatenate([w_router[l], jnp.zeros((d, LANES - N_EXPERTS), F32)], axis=1).astype(BF16)
        b_r = jnp.concatenate([b_router[l].astype(F32), jnp.full((LANES - N_EXPERTS,), NEG_BIG, F32)]).reshape(1, -1)
        tm_x = _pick(seq, 512)
        h2, gw, route, cnt = _xattn(h1, kmem, vmem, xa_wq[l].astype(BF16), xa_wo[l].astype(BF16),
                                    row2(ln2_g[l]), row2(ln2_b[l]), w_r, b_r, bsz, seq, mem_len, tm_x)

        counts = cnt[0, :N_EXPERTS].astype(jnp.int32)
        padded = (counts + MOE_BLOCK - 1) // MOE_BLOCK * MOE_BLOCK
        pad_end = jnp.cumsum(padded)
        pad_start = pad_end - padded
        n_asg = t * TOP_K
        n_blocks = -(-n_asg // MOE_BLOCK) + N_EXPERTS
        n_rows = n_blocks * MOE_BLOCK
        top_e = route[:, :TOP_K]
        rank = route[:, TOP_K:2 * TOP_K]
        sel = top_e[:, :, None] == jnp.arange(N_EXPERTS, dtype=jnp.int32)[None, None, :]
        dest = (jnp.sum(jnp.where(sel, pad_start[None, None, :], 0), axis=-1) + rank).astype(jnp.int32)
        dest_flat = dest.reshape(n_asg)
        blk_row = jnp.arange(n_blocks, dtype=jnp.int32) * MOE_BLOCK
        block_e = jnp.minimum(jnp.sum(pad_end[None, :] <= blk_row[:, None], axis=1), N_EXPERTS - 1).astype(jnp.int32)
        nact = (pad_end[-1:] // MOE_BLOCK).astype(jnp.int32)
        pad_lo = (pad_start + counts).astype(jnp.int32)
        pad_n = (padded - counts).astype(jnp.int32)

        tm_d = _pick(t, 256)
        xs = _dispatch(pad_lo, pad_n, nact, dest_flat, h2, n_rows, _pick(t, 512))
        yb = _experts(block_e, nact, xs, w_gu[l], b_gu[l].astype(F32)[:, None, :],
                      w_dn[l], b_dn[l].astype(F32)[:, None, :])
        h = _combine(dest_flat, gw, h2, row2(ln3_g[l]), row2(ln3_b[l]), yb, tm_d)
    return h.reshape(bsz, seq, d)
```

```python
import functools

import jax
import jax.numpy as jnp
from jax import lax
from jax.experimental import pallas as pl
from jax.experimental.pallas import tpu as pltpu

F32 = jnp.float32
BF16 = jnp.bfloat16

CHUNK = 64
ML_HEADS = 4
GDN_DK = 128
GDN_DV = 128
CONV_K = 4
XA_HEADS = 4
N_EXPERTS = 32
TOP_K = 4
SWIGLU_LIMIT = 7.0
SWIGLU_ALPHA = 1.702
MOE_BLOCK = 512
DEPTH = 1
DN_ALPHA = (2 * DEPTH) ** 0.25
LN_EPS = 1e-5
RMS_EPS = 1e-6
LANES = 128
SUBLANES = 8
VMEM_LIMIT = 56 * 1024 * 1024
NEG_BIG = -1e30
ML_CHUNKS_PER_STEP = 8
GDN_CHUNKS_PER_STEP = 4
GDN_CONV_COLS = 512


def _params(*sem):
    return pltpu.CompilerParams(dimension_semantics=sem, vmem_limit_bytes=VMEM_LIMIT)


def _dot(a, b):
    return jnp.dot(a, b, preferred_element_type=F32)


def _dot_nt(a, b):
    return lax.dot_general(a, b, (((1,), (1,)), ((), ())), preferred_element_type=F32)


def _layer_norm(x, g, b):
    mu = jnp.mean(x, axis=-1, keepdims=True)
    xc = x - mu
    var = jnp.mean(xc * xc, axis=-1, keepdims=True)
    return xc * lax.rsqrt(var + LN_EPS) * g + b


def _sigmoid(x):
    return 1.0 / (1.0 + jnp.exp(-x))


def _log_sigmoid(x):
    return jnp.minimum(x, 0.0) - jnp.log(1.0 + jnp.exp(-jnp.abs(x)))


def _softplus(x):
    return jnp.maximum(x, 0.0) + jnp.log(1.0 + jnp.exp(-jnp.abs(x)))


def _split3(x):
    hi = x.astype(BF16)
    r = x - hi.astype(F32)
    mid = r.astype(BF16)
    lo = (r - mid.astype(F32)).astype(BF16)
    return hi, mid, lo


def _cumsum_lanes(x, upper):
    hi, mid, lo = _split3(x)
    return _dot(hi, upper) + _dot(mid, upper) + _dot(lo, upper)


def _ln_proj_kernel(x_ref, g_ref, b_ref, w_ref, ws_ref, h_ref, p_ref, s_ref, xn_ref):
    @pl.when(pl.program_id(1) == 0)
    def _():
        h = _layer_norm(x_ref[...], g_ref[...], b_ref[...])
        h_ref[...] = h
        hb = h.astype(BF16)
        xn_ref[...] = hb
        s_ref[...] = _dot(hb, ws_ref[...])

    p_ref[...] = _dot(xn_ref[...], w_ref[...]).astype(BF16)


def _ln_proj(x2, g, b, w_big, w_small, tm, tn):
    t, d = x2.shape
    nw = w_big.shape[1]
    return pl.pallas_call(
        _ln_proj_kernel,
        grid=(t // tm, nw // tn),
        in_specs=[
            pl.BlockSpec((tm, d), lambda i, j: (i, 0)),
            pl.BlockSpec((1, d), lambda i, j: (0, 0)),
            pl.BlockSpec((1, d), lambda i, j: (0, 0)),
            pl.BlockSpec((d, tn), lambda i, j: (0, j)),
            pl.BlockSpec((d, LANES), lambda i, j: (0, 0)),
        ],
        out_specs=[
            pl.BlockSpec((tm, d), lambda i, j: (i, 0)),
            pl.BlockSpec((tm, tn), lambda i, j: (i, j)),
            pl.BlockSpec((tm, LANES), lambda i, j: (i, 0)),
        ],
        out_shape=[
            jax.ShapeDtypeStruct((t, d), F32),
            jax.ShapeDtypeStruct((t, nw), BF16),
            jax.ShapeDtypeStruct((t, LANES), F32),
        ],
        scratch_shapes=[pltpu.VMEM((tm, d), BF16)],
        compiler_params=_params("arbitrary", "arbitrary"),
        name="ln_in_proj",
    )(x2, g, b, w_big, w_small)


def _mlstm_kernel(q_ref, k_ref, v_ref, o_ref, gr_ref, br_ref, ng_ref,
                  out_ref, c_ref, m_ref, sqk_ref, kt_ref, p_ref, ktw_ref, b3_ref, li3_ref, kw3_ref,
                  gs_ref, mc_ref, stb_ref, stm_ref, std_ref, num_ref, kv_ref, *, dqk, dv, n_ch):
    L = CHUNK
    H = ML_HEADS

    @pl.when(pl.program_id(1) == 0)
    def _():
        c_ref[...] = jnp.zeros_like(c_ref)
        m_ref[...] = jnp.zeros_like(m_ref)

    ii = lax.broadcasted_iota(jnp.int32, (L, L), 0)
    jj = lax.broadcasted_iota(jnp.int32, (L, L), 1)
    causal = jj <= ii
    diag = ii == jj
    upper = (ii <= jj).astype(BF16)
    scale = dqk ** -0.5
    heads = [(g, h) for g in range(n_ch) for h in range(H)]

    for g, h in heads:
        rows = slice(g * L, (g + 1) * L)
        kf = k_ref[rows, h * dqk:(h + 1) * dqk].astype(F32) * scale
        sqk_ref[g * H + h] = _dot_nt(q_ref[rows, h * dqk:(h + 1) * dqk], kf.astype(BF16))
        kt_ref[g * H + h] = kf.T

    pre = gr_ref[:, 0:2 * H, :].reshape(n_ch * 2 * H, L) + jnp.concatenate([br_ref[...]] * n_ch, axis=0)
    b_all = _cumsum_lanes(_log_sigmoid(pre), upper)
    gs_all = b_all[:, L - 1:L]
    a_all = gs_all - b_all + pltpu.roll(pre, H, axis=0)
    mc_all = jnp.max(a_all, axis=1, keepdims=True)
    kw_all = jnp.exp(a_all - mc_all)
    gs_ref[...] = jnp.broadcast_to(gs_all, gs_ref.shape)
    mc_ref[...] = jnp.broadcast_to(mc_all, mc_ref.shape)
    for g, h in heads:
        r = g * 2 * H + H + h
        b3_ref[g * H + h] = b_all[r:r + 1, :]
        li3_ref[g * H + h] = pre[r - H:r - H + 1, :]
        kw3_ref[g * H + h] = kw_all[r:r + 1, :]

    b_r = b3_ref[...]
    b_c = jnp.sum(jnp.where(diag, b_r, 0.0), axis=2, keepdims=True)
    dmat = jnp.where(causal, b_c - b_r + li3_ref[...], -jnp.inf)
    m_intra = jnp.max(dmat, axis=2, keepdims=True)
    p = jnp.exp(dmat - m_intra) * sqk_ref[...]
    p_ref[...] = p.astype(BF16)
    stb_ref[...] = jnp.broadcast_to(b_c, stb_ref.shape)
    stm_ref[...] = jnp.broadcast_to(m_intra, stm_ref.shape)
    std_ref[...] = jnp.broadcast_to(jnp.sum(p, axis=2, keepdims=True), std_ref.shape)
    ktw_ref[...] = (kt_ref[...] * kw3_ref[...]).astype(BF16)

    ones = jnp.ones((L, LANES), BF16)
    for g, h in heads:
        vh = v_ref[g * L:(g + 1) * L, h * dv:(h + 1) * dv]
        num_ref[g * H + h] = _dot(p_ref[g * H + h], vh)
        kv_ref[g * H + h] = _dot(ktw_ref[g * H + h], jnp.concatenate([vh, ones], axis=1))

    rep = dv // LANES
    wide = lambda s: jnp.concatenate([s] * rep, axis=1)
    for g, h in heads:
        i = g * H + h
        r = g * 2 * H + H + h
        rows = slice(g * L, (g + 1) * L)
        c_st = c_ref[h]
        m_st = m_ref[h:h + 1, :]
        qc = _dot(q_ref[rows, h * dqk:(h + 1) * dqk], c_st.astype(BF16))
        m_intra = stm_ref[i]
        inter_log = stb_ref[i] + m_st
        m_out = jnp.maximum(inter_log, m_intra)
        s_inter = jnp.exp(inter_log - m_out)
        s_intra = jnp.exp(m_intra - m_out)
        num = wide(s_inter) * qc[:, 0:dv] + wide(s_intra) * num_ref[i]
        den = s_inter * qc[:, dv:dv + LANES] + s_intra * std_ref[i]
        hh = num / wide(jnp.maximum(jnp.abs(den), jnp.exp(-m_out)))
        gs = gs_ref[r:r + 1, :]
        mc = mc_ref[r:r + 1, :]
        m_new = jnp.maximum(gs + m_st, mc)
        dec = jnp.exp(gs + m_st - m_new)
        s_new = jnp.exp(mc - m_new)
        c_ref[h] = (jnp.concatenate([dec] * (rep + 1), axis=1) * c_st
                    + jnp.concatenate([s_new] * (rep + 1), axis=1) * kv_ref[i])
        m_ref[h:h + 1, :] = m_new

        rms = lax.rsqrt(jnp.mean(hh * hh, axis=1, keepdims=True) + RMS_EPS)
        og = _sigmoid(o_ref[rows, h * dv:(h + 1) * dv].astype(F32))
        out_ref[rows, h * dv:(h + 1) * dv] = (hh * rms * ng_ref[:, h * dv:(h + 1) * dv] * og).astype(BF16)


def _mlstm(proj, small_t, bias_r, norm_g, bsz, seq, cols, dqk, dv, n_ch):
    n = seq // (CHUNK * n_ch)
    L = CHUNK
    R = n_ch * L
    H = ML_HEADS
    qk_w, v_w = H * dqk, H * dv
    t = bsz * seq
    nh = n_ch * H
    kern = functools.partial(_mlstm_kernel, dqk=dqk, dv=dv, n_ch=n_ch)
    row = lambda b, c: b * n + c
    return pl.pallas_call(
        kern,
        grid=(bsz, n),
        in_specs=[
            pl.BlockSpec((R, qk_w), lambda b, c: (row(b, c), cols["mq"] // qk_w)),
            pl.BlockSpec((R, qk_w), lambda b, c: (row(b, c), cols["mk"] // qk_w)),
            pl.BlockSpec((R, v_w), lambda b, c: (row(b, c), cols["mv"] // v_w)),
            pl.BlockSpec((R, v_w), lambda b, c: (row(b, c), cols["mo"] // v_w)),
            pl.BlockSpec((n_ch, small_t.shape[1], L), lambda b, c: (row(b, c), 0, 0)),
            pl.BlockSpec(bias_r.shape, lambda b, c: (0, 0)),
            pl.BlockSpec((1, v_w), lambda b, c: (0, 0)),
        ],
        out_specs=pl.BlockSpec((R, v_w), lambda b, c: (row(b, c), 0)),
        out_shape=jax.ShapeDtypeStruct((t, v_w), BF16),
        scratch_shapes=[
            pltpu.VMEM((H, dqk, dv + LANES), F32),
            pltpu.VMEM((8, LANES), F32),
            pltpu.VMEM((nh, L, L), F32),
            pltpu.VMEM((nh, dqk, L), F32),
            pltpu.VMEM((nh, L, L), BF16),
            pltpu.VMEM((nh, dqk, L), BF16),
            pltpu.VMEM((nh, 1, L), F32),
            pltpu.VMEM((nh, 1, L), F32),
            pltpu.VMEM((nh, 1, L), F32),
            pltpu.VMEM((n_ch * 2 * H, LANES), F32),
            pltpu.VMEM((n_ch * 2 * H, LANES), F32),
            pltpu.VMEM((nh, L, LANES), F32),
            pltpu.VMEM((nh, L, LANES), F32),
            pltpu.VMEM((nh, L, LANES), F32),
            pltpu.VMEM((nh, L, dv), F32),
            pltpu.VMEM((nh, dqk, dv + LANES), F32),
        ],
        compiler_params=_params("arbitrary", "arbitrary"),
        name="mlstm",
    )(proj, proj, proj, proj, small_t, bias_r, norm_g)


def _gdn_kernel(x_ref, z_ref, gr_ref, cw_ref, al_ref, dt_ref, ng_ref,
                out_ref, s_ref, xs_ref, c_ref, qn_ref, kn_ref, knt_ref, kk_ref, qk_ref, pw_ref, x_ref_, rhs_ref,
                attn_ref, qd_ref, kdt_ref, gam3_ref, beta3_ref, kdwb3_ref, gt_ref, sol_ref, vn_ref,
                *, n_qk, n_v, ga_off, gb_off, n_ch):
    L = CHUNK
    R = n_ch * L
    dk, dv = GDN_DK, GDN_DV
    qk_w = n_qk * dk
    rep = n_v // n_qk
    conv_ch = 2 * qk_w + n_v * dv

    @pl.when(pl.program_id(1) == 0)
    def _():
        s_ref[...] = jnp.zeros_like(s_ref)
        xs_ref[0:8, :] = jnp.zeros((8, xs_ref.shape[1]), F32)

    xs_ref[8:8 + R, :] = x_ref[...].astype(F32)
    for cb in range(0, conv_ch, GDN_CONV_COLS):
        cs = slice(cb, cb + GDN_CONV_COLS)
        xa = xs_ref[0:8 + R, cs]
        conv = cw_ref[CONV_K - 1:CONV_K, cs] * xa[8:8 + R]
        for sh in range(1, CONV_K):
            conv = conv + cw_ref[CONV_K - 1 - sh:CONV_K - sh, cs] * pltpu.roll(xa, sh, axis=0)[8:8 + R]
        c_ref[:, cs] = conv * _sigmoid(conv)
    xs_ref[0:8, :] = xs_ref[R:R + 8, :]

    ii = lax.broadcasted_iota(jnp.int32, (L, L), 0)
    jj = lax.broadcasted_iota(jnp.int32, (L, L), 1)
    incl = jj <= ii
    strict = jj < ii
    diag = ii == jj
    eye = diag.astype(F32)
    upper = (ii <= jj).astype(BF16)

    for g in range(n_ch):
        rows = slice(g * L, (g + 1) * L)
        for hk in range(n_qk):
            iq = g * n_qk + hk
            cq = c_ref[rows, hk * dk:(hk + 1) * dk]
            ck = c_ref[rows, qk_w + hk * dk:qk_w + (hk + 1) * dk]
            qn = cq * lax.rsqrt(jnp.sum(cq * cq, axis=1, keepdims=True) + RMS_EPS) * (dk ** -0.5)
            kn = ck * lax.rsqrt(jnp.sum(ck * ck, axis=1, keepdims=True) + RMS_EPS)
            qn_ref[iq] = qn
            kn_ref[iq] = kn
            knt_ref[iq] = kn.T
            kb = kn.astype(BF16)
            kk_ref[iq] = _dot_nt(kb, kb)
            qk_ref[iq] = _dot_nt(qn.astype(BF16), kb)

    for g in range(n_ch):
        gd = -jnp.exp(al_ref[...]) * _softplus(gr_ref[g, ga_off:ga_off + n_v, :] + dt_ref[...])
        gam = _cumsum_lanes(gd, upper)
        beta = _sigmoid(gr_ref[g, gb_off:gb_off + n_v, :])
        g_tot = gam[:, L - 1:L]
        kdwb = jnp.exp(g_tot - gam) * beta
        gt_ref[g * n_v:(g + 1) * n_v, :] = jnp.broadcast_to(jnp.exp(g_tot), (n_v, LANES))
        for h in range(n_v):
            gam3_ref[g * n_v + h] = gam[h:h + 1, :]
            beta3_ref[g * n_v + h] = beta[h:h + 1, :]
            kdwb3_ref[g * n_v + h] = kdwb[h:h + 1, :]

    for g in range(n_ch):
        sv = slice(g * n_v, (g + 1) * n_v)
        sq = slice(g * n_qk, (g + 1) * n_qk)
        rows = slice(g * L, (g + 1) * L)
        gam_r = gam3_ref[sv]
        beta_r = beta3_ref[sv]
        gam_c = jnp.sum(jnp.where(diag, gam_r, 0.0), axis=2, keepdims=True)
        decm = jnp.exp(jnp.where(incl, gam_c - gam_r, -jnp.inf))
        db = decm * beta_r
        a = jnp.where(strict, jnp.repeat(kk_ref[sq], rep, axis=0) * db, 0.0)
        pw_ref[sv] = a
        x_ref_[sv] = eye - a
        attn_ref[sv] = (jnp.repeat(qk_ref[sq], rep, axis=0) * db).astype(BF16)
        eg_c = jnp.exp(gam_c)
        rhs_ref[sv, :, dv:dv + dk] = (jnp.repeat(kn_ref[sq], rep, axis=0) * eg_c).astype(BF16)
        qd_ref[sv] = (jnp.repeat(qn_ref[sq], rep, axis=0) * eg_c).astype(BF16)
        kdt_ref[sv] = (jnp.repeat(knt_ref[sq], rep, axis=0) * kdwb3_ref[sv]).astype(BF16)
        for h in range(n_v):
            rhs_ref[g * n_v + h, :, 0:dv] = c_ref[rows, 2 * qk_w + h * dv:2 * qk_w + (h + 1) * dv].astype(BF16)

    n_sq = 5
    for lvl in range(n_sq + 1):
        for i in range(n_ch * n_v):
            pwb = pw_ref[i].astype(BF16)
            if lvl > 0:
                xv = x_ref_[i]
                x_ref_[i] = xv + _dot(xv.astype(BF16), pwb)
            if lvl < n_sq:
                pw_ref[i] = _dot(pwb, pwb)

    for i in range(n_ch * n_v):
        sol_ref[i] = _dot(x_ref_[i].astype(BF16), rhs_ref[i])

    for g in range(n_ch):
        rows = slice(g * L, (g + 1) * L)
        for h in range(n_v):
            sol = sol_ref[g * n_v + h]
            vn_ref[h] = (sol[:, 0:dv] - _dot(sol[:, dv:dv + dk].astype(BF16), s_ref[h].astype(BF16))).astype(BF16)
        for h in range(n_v):
            i = g * n_v + h
            s_st = s_ref[h]
            vnb = vn_ref[h]
            o = _dot(qd_ref[i], s_st.astype(BF16)) + _dot(attn_ref[i], vnb)
            s_ref[h] = gt_ref[i:i + 1, :] * s_st + _dot(kdt_ref[i], vnb)
            rms = lax.rsqrt(jnp.mean(o * o, axis=1, keepdims=True) + RMS_EPS)
            zz = z_ref[rows, h * dv:(h + 1) * dv].astype(F32)
            out_ref[rows, h * dv:(h + 1) * dv] = (o * rms * ng_ref[...] * (zz * _sigmoid(zz))).astype(BF16)


def _gdn(proj, small_t, conv_w, al, dt, norm_g, bsz, seq, cols, n_qk, n_v, ga_off, gb_off, n_ch):
    n = seq // (CHUNK * n_ch)
    L = CHUNK
    R = n_ch * L
    conv_ch = 2 * n_qk * GDN_DK + n_v * GDN_DV
    v_w = n_v * GDN_DV
    t = bsz * seq
    nq, nv = n_ch * n_qk, n_ch * n_v
    kern = functools.partial(_gdn_kernel, n_qk=n_qk, n_v=n_v, ga_off=ga_off, gb_off=gb_off, n_ch=n_ch)
    row = lambda b, c: b * n + c
    full = lambda a: pl.BlockSpec(a.shape, lambda b, c: (0,) * a.ndim)
    return pl.pallas_call(
        kern,
        grid=(bsz, n),
        in_specs=[
            pl.BlockSpec((R, conv_ch), lambda b, c: (row(b, c), cols["gqkv"] // conv_ch)),
            pl.BlockSpec((R, v_w), lambda b, c: (row(b, c), cols["gz"] // v_w)),
            pl.BlockSpec((n_ch, small_t.shape[1], L), lambda b, c: (row(b, c), 0, 0)),
            full(conv_w), full(al), full(dt), full(norm_g),
        ],
        out_specs=pl.BlockSpec((R, v_w), lambda b, c: (row(b, c), 0)),
        out_shape=jax.ShapeDtypeStruct((t, v_w), BF16),
        scratch_shapes=[
            pltpu.VMEM((n_v, GDN_DK, GDN_DV), F32),
            pltpu.VMEM((8 + R + 8, conv_ch), F32),
            pltpu.VMEM((R, conv_ch), F32),
            pltpu.VMEM((nq, L, GDN_DK), F32),
            pltpu.VMEM((nq, L, GDN_DK), F32),
            pltpu.VMEM((nq, GDN_DK, L), F32),
            pltpu.VMEM((nq, L, L), F32),
            pltpu.VMEM((nq, L, L), F32),
            pltpu.VMEM((nv, L, L), F32),
            pltpu.VMEM((nv, L, L), F32),
            pltpu.VMEM((nv, L, GDN_DV + GDN_DK), BF16),
            pltpu.VMEM((nv, L, L), BF16),
            pltpu.VMEM((nv, L, GDN_DK), BF16),
            pltpu.VMEM((nv, GDN_DK, L), BF16),
            pltpu.VMEM((nv, 1, L), F32),
            pltpu.VMEM((nv, 1, L), F32),
            pltpu.VMEM((nv, 1, L), F32),
            pltpu.VMEM((nv, LANES), F32),
            pltpu.VMEM((nv, L, GDN_DV + GDN_DK), F32),
            pltpu.VMEM((n_v, L, GDN_DV), BF16),
        ],
        compiler_params=_params("arbitrary", "arbitrary"),
        name="gdn",
    )(proj, proj, small_t, conv_w, al, dt, norm_g)


def _merge_kernel(hm_ref, og_ref, gm_ref, gg_ref, h0_ref, wbm_ref, wbg_ref, wo_ref, g_ref, b_ref, out_ref):
    y_ml = _dot(hm_ref[...], wbm_ref[...])
    y_gdn = _dot(og_ref[...], wbg_ref[...])
    merged = _sigmoid(gm_ref[...].astype(F32)) * y_ml + _sigmoid(gg_ref[...].astype(F32)) * y_gdn
    mix = _dot(merged.astype(BF16), wo_ref[...])
    out_ref[...] = _layer_norm(DN_ALPHA * h0_ref[...] + mix, g_ref[...], b_ref[...])


def _merge(hm, og, proj, h0, w_bm, w_bg, w_out, g, b, cols, tm):
    t, d = h0.shape
    full = lambda a: pl.BlockSpec(a.shape, lambda i: (0,) * a.ndim)
    return pl.pallas_call(
        _merge_kernel,
        grid=(t // tm,),
        in_specs=[
            pl.BlockSpec((tm, hm.shape[1]), lambda i: (i, 0)),
            pl.BlockSpec((tm, og.shape[1]), lambda i: (i, 0)),
            pl.BlockSpec((tm, d), lambda i: (i, cols["gate_ml"] // d)),
            pl.BlockSpec((tm, d), lambda i: (i, cols["gate_gdn"] // d)),
            pl.BlockSpec((tm, d), lambda i: (i, 0)),
            full(w_bm), full(w_bg), full(w_out), full(g), full(b),
        ],
        out_specs=pl.BlockSpec((tm, d), lambda i: (i, 0)),
        out_shape=jax.ShapeDtypeStruct((t, d), F32),
        compiler_params=_params("arbitrary"),
        name="merge_out_ln1",
    )(hm, og, proj, proj, h0, w_bm, w_bg, w_out, g, b)


def _kv_kernel(m_ref, wk_ref, wv_ref, k_ref, v_ref):
    mb = m_ref[...].astype(BF16)
    k_ref[...] = _dot(mb, wk_ref[...]).astype(BF16)
    v_ref[...] = _dot(mb, wv_ref[...]).astype(BF16)


def _kv_proj(mem2, wk, wv, tm):
    t, d = mem2.shape
    full = lambda a: pl.BlockSpec(a.shape, lambda i: (0,) * a.ndim)
    return pl.pallas_call(
        _kv_kernel,
        grid=(t // tm,),
        in_specs=[pl.BlockSpec((tm, d), lambda i: (i, 0)), full(wk), full(wv)],
        out_specs=[pl.BlockSpec((tm, d), lambda i: (i, 0))] * 2,
        out_shape=[jax.ShapeDtypeStruct((t, d), BF16)] * 2,
        compiler_params=_params("arbitrary"),
        name="mem_kv_proj",
    )(mem2, wk, wv)


def _xattn_kernel(h1_ref, k_ref, v_ref, wq_ref, wo_ref, g_ref, b_ref, wr_ref, br_ref,
                  h2_ref, gw_ref, route_ref, cnt_ref, carry_ref, *, dh):
    tm = h1_ref.shape[0]

    @pl.when((pl.program_id(0) == 0) & (pl.program_id(1) == 0))
    def _():
        carry_ref[...] = jnp.zeros_like(carry_ref)

    h1 = h1_ref[...]
    q = _dot(h1.astype(BF16), wq_ref[...])
    outs = []
    for hd in range(XA_HEADS):
        qh = q[:, hd * dh:(hd + 1) * dh].astype(BF16)
        kh = k_ref[:, hd * dh:(hd + 1) * dh]
        vh = v_ref[:, hd * dh:(hd + 1) * dh]
        sc = _dot_nt(qh, kh) * (dh ** -0.5)
        e = jnp.exp(sc - jnp.max(sc, axis=1, keepdims=True))
        p = e / jnp.sum(e, axis=1, keepdims=True)
        outs.append(_dot(p.astype(BF16), vh))
    o = jnp.concatenate(outs, axis=1)
    xa = _dot(o.astype(BF16), wo_ref[...])
    h2 = _layer_norm(DN_ALPHA * h1 + xa, g_ref[...], b_ref[...])
    h2_ref[...] = h2

    logits = _dot(h2.astype(BF16), wr_ref[...]) + br_ref[...]
    lane = lax.broadcasted_iota(jnp.int32, (tm, LANES), 1)
    lane_f = lane.astype(F32)
    work = logits
    vals, idxs = [], []
    for _ in range(TOP_K):
        m = jnp.max(work, axis=1, keepdims=True)
        idx = jnp.min(jnp.where(work == m, lane_f, float(LANES)), axis=1, keepdims=True)
        vals.append(m)
        idxs.append(idx)
        work = jnp.where(lane_f == idx, -jnp.inf, work)
    es = [jnp.exp(v - vals[0]) for v in vals]
    tot = es[0]
    for e_ in es[1:]:
        tot = tot + e_
    onehot = jnp.zeros((tm, LANES), F32)
    for idx in idxs:
        onehot = onehot + (lane_f == idx).astype(F32)
    ri = lax.broadcasted_iota(jnp.int32, (tm, tm), 0)
    ci = lax.broadcasted_iota(jnp.int32, (tm, tm), 1)
    tri = (ci < ri).astype(BF16)
    carry = carry_ref[0:1, :]
    ranks = carry + _dot(tri, onehot.astype(BF16))
    gw = jnp.zeros((tm, LANES), F32)
    route = jnp.zeros((tm, LANES), F32)
    for k_ in range(TOP_K):
        rk = jnp.sum(jnp.where(lane_f == idxs[k_], ranks, 0.0), axis=1, keepdims=True)
        gw = gw + jnp.where(lane == k_, es[k_] / tot, 0.0)
        route = route + jnp.where(lane == k_, idxs[k_], 0.0) + jnp.where(lane == TOP_K + k_, rk, 0.0)
    gw_ref[...] = gw
    route_ref[...] = route.astype(jnp.int32)
    carry = carry + jnp.sum(onehot, axis=0, keepdims=True)
    carry_ref[...] = jnp.broadcast_to(carry, carry_ref.shape)
    cnt_ref[...] = jnp.broadcast_to(carry, cnt_ref.shape)


def _xattn(h1, kmem, vmem, wq, wo, g, b, w_r, b_r, bsz, seq, mem_len, tm):
    t, d = h1.shape
    nt = seq // tm
    full = lambda a: pl.BlockSpec(a.shape, lambda i, j: (0,) * a.ndim)
    kern = functools.partial(_xattn_kernel, dh=d // XA_HEADS)
    return pl.pallas_call(
        kern,
        grid=(bsz, nt),
        in_specs=[
            pl.BlockSpec((tm, d), lambda i, j: (i * nt + j, 0)),
            pl.BlockSpec((mem_len, d), lambda i, j: (i, 0)),
            pl.BlockSpec((mem_len, d), lambda i, j: (i, 0)),
            full(wq), full(wo), full(g), full(b), full(w_r), full(b_r),
        ],
        out_specs=[
            pl.BlockSpec((tm, d), lambda i, j: (i * nt + j, 0)),
            pl.BlockSpec((tm, LANES), lambda i, j: (i * nt + j, 0)),
            pl.BlockSpec((tm, LANES), lambda i, j: (i * nt + j, 0)),
            pl.BlockSpec((8, LANES), lambda i, j: (0, 0)),
        ],
        out_shape=[
            jax.ShapeDtypeStruct((t, d), F32),
            jax.ShapeDtypeStruct((t, LANES), F32),
            jax.ShapeDtypeStruct((t, LANES), jnp.int32),
            jax.ShapeDtypeStruct((8, LANES), F32),
        ],
        scratch_shapes=[pltpu.VMEM((8, LANES), F32)],
        compiler_params=_params("arbitrary", "arbitrary"),
        name="xattn_ln2_router",
    )(h1, kmem, vmem, wq, wo, g, b, w_r, b_r)


def _dispatch_kernel(pad_lo_ref, pad_n_ref, na_ref, dest_ref, h_ref, xs_ref, zero_ref, sem, zsem):
    tm = h_ref.shape[0]
    step = pl.program_id(0)
    n_blocks = xs_ref.shape[0] // MOE_BLOCK

    @pl.when(step == 0)
    def _():
        zero_ref[...] = jnp.zeros_like(zero_ref)

        def per_expert(e, carry):
            n = pad_n_ref[e]
            pos = pad_lo_ref[e] + n

            def zero_rows(start, p):
                cp = pltpu.make_async_copy(zero_ref.at[pl.ds(0, p), :], xs_ref.at[pl.ds(start, p), :], zsem)
                cp.start()
                cp.wait()

            p = MOE_BLOCK // 2
            while p >= SUBLANES:

                @pl.when((n & p) != 0)
                def _(p=p, pos=pos):
                    zero_rows(pl.multiple_of(pos - p, SUBLANES), p)

                pos = pos - (n & p)
                p //= 2
            for r in range(SUBLANES - 1):

                @pl.when(r < (n & (SUBLANES - 1)))
                def _(r=r, pos=pos):
                    zero_rows(pos - 1 - r, 1)

            return carry

        lax.fori_loop(0, N_EXPERTS, per_expert, 0)

        def tail(b, carry):
            half = MOE_BLOCK // 2
            cp = pltpu.make_async_copy(zero_ref, xs_ref.at[pl.ds(b * half, half), :], zsem)
            cp.start()
            cp.wait()
            return carry

        lax.fori_loop(2 * na_ref[0], 2 * n_blocks, tail, 0)

    for tok in range(tm):
        src = h_ref.at[pl.ds(tok, 1), :]
        for k_ in range(TOP_K):
            pltpu.make_async_copy(src, xs_ref.at[pl.ds(dest_ref[tok * TOP_K + k_], 1), :], sem).start(priority=k_ % 2)
    for _ in range(TOP_K):
        pltpu.make_async_copy(h_ref, xs_ref.at[pl.ds(0, tm), :], sem).wait()


def _dispatch(pad_lo, pad_n, nact, dest_flat, h2, n_rows, tm):
    t, d = h2.shape
    grid_spec = pltpu.PrefetchScalarGridSpec(
        num_scalar_prefetch=3,
        grid=(t // tm,),
        in_specs=[
            pl.BlockSpec((tm * TOP_K,), lambda i, *_: (i,), memory_space=pltpu.SMEM),
            pl.BlockSpec((tm, d), lambda i, *_: (i, 0)),
        ],
        out_specs=pl.BlockSpec(memory_space=pl.ANY),
        scratch_shapes=[pltpu.VMEM((MOE_BLOCK // 2, d), F32), pltpu.SemaphoreType.DMA(()),
                        pltpu.SemaphoreType.DMA(())],
    )
    return pl.pallas_call(
        _dispatch_kernel,
        grid_spec=grid_spec,
        out_shape=jax.ShapeDtypeStruct((n_rows, d), F32),
        compiler_params=_params("arbitrary"),
        name="moe_dispatch",
    )(pad_lo, pad_n, nact, dest_flat, h2)


def _expert_kernel(be_ref, na_ref, x_ref, wgu_ref, bgu_ref, wdn_ref, bdn_ref, y_ref, wgu_s, wdn_s, *, d_exp):
    blk = pl.program_id(0)
    active = blk < na_ref[0]

    @pl.when(jnp.logical_not(active))
    def _():
        y_ref[...] = jnp.zeros_like(y_ref)

    @pl.when(active & ((blk == 0) | (be_ref[blk] != be_ref[jnp.maximum(blk - 1, 0)])))
    def _():
        wgu_s[...] = wgu_ref[0].astype(BF16)
        wdn_s[...] = wdn_ref[0].astype(BF16)

    @pl.when(active)
    def _():
        gu = _dot(x_ref[...].astype(BF16), wgu_s[...]) + bgu_ref[0]
        gate = jnp.minimum(gu[:, :d_exp], SWIGLU_LIMIT)
        up = jnp.clip(gu[:, d_exp:], -SWIGLU_LIMIT, SWIGLU_LIMIT)
        act = (up + 1.0) * (gate * _sigmoid(SWIGLU_ALPHA * gate))
        y_ref[...] = _dot(act.astype(BF16), wdn_s[...]) + bdn_ref[0]


def _experts(block_e, nact, xs, w_gu, b_gu, w_dn, b_dn):
    n_rows, d = xs.shape
    n_blocks = n_rows // MOE_BLOCK
    d_exp = w_dn.shape[1]
    kern = functools.partial(_expert_kernel, d_exp=d_exp)
    emap = lambda i, be, na: (be[i], 0, 0)
    grid_spec = pltpu.PrefetchScalarGridSpec(
        num_scalar_prefetch=2,
        grid=(n_blocks,),
        in_specs=[
            pl.BlockSpec((MOE_BLOCK, d), lambda i, be, na: (i, 0)),
            pl.BlockSpec((1, d, 2 * d_exp), emap),
            pl.BlockSpec((1, 1, 2 * d_exp), emap),
            pl.BlockSpec((1, d_exp, d), emap),
            pl.BlockSpec((1, 1, d), emap),
        ],
        out_specs=pl.BlockSpec((MOE_BLOCK, d), lambda i, be, na: (i, 0)),
        scratch_shapes=[pltpu.VMEM((d, 2 * d_exp), BF16), pltpu.VMEM((d_exp, d), BF16)],
    )
    return pl.pallas_call(
        kern,
        grid_spec=grid_spec,
        out_shape=jax.ShapeDtypeStruct((n_rows, d), F32),
        compiler_params=_params("arbitrary"),
        name="moe_experts",
    )(block_e, nact, xs, w_gu, b_gu, w_dn, b_dn)


def _combine_kernel(dest_ref, dest_nxt_ref, gw_ref, h_ref, g_ref, b_ref, y_ref, out_ref, buf_a, buf_b, sem):
    tm = buf_a.shape[1]
    step = pl.program_id(0)
    n_steps = pl.num_programs(0)

    def issue(idx_ref, half, buf, s):
        for tok in range(tm):
            for k_ in range(TOP_K):
                pltpu.make_async_copy(y_ref.at[pl.ds(idx_ref[(half * tm + tok) * TOP_K + k_], 1), :],
                                      buf.at[k_, pl.ds(tok, 1), :], sem.at[s]).start(priority=k_ % 2)

    def consume(half, buf, s):
        for k_ in range(TOP_K):
            pltpu.make_async_copy(y_ref.at[pl.ds(0, tm), :], buf.at[k_], sem.at[s]).wait()
        rows = slice(half * tm, (half + 1) * tm)
        gw = gw_ref[rows, :]
        ff = buf[0] * gw[:, 0:1]
        for k_ in range(1, TOP_K):
            ff = ff + buf[k_] * gw[:, k_:k_ + 1]
        out_ref[rows, :] = _layer_norm(DN_ALPHA * h_ref[rows, :] + ff, g_ref[...], b_ref[...])

    @pl.when(step == 0)
    def _():
        issue(dest_ref, 0, buf_a, 0)

    issue(dest_ref, 1, buf_b, 1)
    consume(0, buf_a, 0)

    @pl.when(step + 1 < n_steps)
    def _():
        issue(dest_nxt_ref, 0, buf_a, 0)

    consume(1, buf_b, 1)


def _combine(dest_flat, gw, h2, g, b, yb, tm):
    t, d = h2.shape
    n = t // (2 * tm)
    return pl.pallas_call(
        _combine_kernel,
        grid=(n,),
        in_specs=[
            pl.BlockSpec((2 * tm * TOP_K,), lambda i: (i,), memory_space=pltpu.SMEM),
            pl.BlockSpec((2 * tm * TOP_K,), lambda i: (jnp.minimum(i + 1, n - 1),), memory_space=pltpu.SMEM),
            pl.BlockSpec((2 * tm, LANES), lambda i: (i, 0)),
            pl.BlockSpec((2 * tm, d), lambda i: (i, 0)),
            pl.BlockSpec((1, d), lambda i: (0, 0)),
            pl.BlockSpec((1, d), lambda i: (0, 0)),
            pl.BlockSpec(memory_space=pl.ANY),
        ],
        out_specs=pl.BlockSpec((2 * tm, d), lambda i: (i, 0)),
        out_shape=jax.ShapeDtypeStruct((t, d), F32),
        scratch_shapes=[pltpu.VMEM((TOP_K, tm, d), F32), pltpu.VMEM((TOP_K, tm, d), F32),
                        pltpu.SemaphoreType.DMA((2,))],
        compiler_params=_params("arbitrary"),
        name="moe_combine_ln3",
    )(dest_flat, dest_flat, gw, h2, g, b, yb)


def _pick(n, pref):
    return pref if n % pref == 0 else n


def kernel(x, mem, ln_in_g, ln_in_b, w_in, ml_gate_bias, ml_norm_g, gdn_conv_w, gdn_a_log, gdn_dt_bias, gdn_norm_g, w_branch_ml, w_branch_gdn, w_mix_out, ln1_g, ln1_b, xa_wq, xa_wk, xa_wv, xa_wo, ln2_g, ln2_b, w_router, b_router, w_gu, b_gu, w_dn, b_dn, ln3_g, ln3_b):
    bsz, seq, d = x.shape
    mem_len = mem.shape[1]
    t = bsz * seq
    ml_dv = d // ML_HEADS
    ml_dqk = ml_dv // 2
    ml_qk_w, ml_v_w = ML_HEADS * ml_dqk, ML_HEADS * ml_dv
    n_qk = d // GDN_DK
    n_v = 2 * n_qk
    gdn_qk_w, gdn_v_w = n_qk * GDN_DK, n_v * GDN_DV
    conv_ch = 2 * gdn_qk_w + gdn_v_w
    splits = (ml_qk_w, ml_qk_w, ml_v_w, ml_v_w, 2 * ML_HEADS, conv_ch, gdn_v_w, n_v, n_v, d, d)
    names = ("mq", "mk", "mv", "mo", "mif", "gqkv", "gz", "ga", "gb", "gate_ml", "gate_gdn")
    starts = {}
    acc = 0
    for nm, sz in zip(names, splits):
        starts[nm] = (acc, sz)
        acc += sz
    row2 = lambda a: a.reshape(1, -1).astype(F32)

    h = x.reshape(t, d)
    for l in range(DEPTH):
        w = w_in[l]
        seg = lambda nm: w[:, starts[nm][0]:starts[nm][0] + starts[nm][1]]
        order = ("gqkv", "gz", "mv", "mo", "gate_ml", "gate_gdn", "mq", "mk")
        cols = {}
        off = 0
        for nm in order:
            assert off % starts[nm][1] == 0
            cols[nm] = off
            off += starts[nm][1]
        w_big = jnp.concatenate([seg(nm) for nm in order], axis=1).astype(BF16)
        n_small = 2 * ML_HEADS + 2 * n_v
        w_small = jnp.concatenate([seg("mif"), seg("ga"), seg("gb"),
                                   jnp.zeros((d, LANES - n_small), F32)], axis=1).astype(BF16)
        ga_off, gb_off = 2 * ML_HEADS, 2 * ML_HEADS + n_v

        if l == 0:
            h0, proj, small = _ln_proj(h, row2(ln_in_g), row2(ln_in_b), w_big, w_small,
                                       _pick(t, 1024), _pick(off, 2816))
        else:
            raise NotImplementedError("DEPTH > 1")
        n_ch = seq // CHUNK
        small_t = small[:, :n_small].reshape(bsz * n_ch, CHUNK, n_small).transpose(0, 2, 1)

        bias = ml_gate_bias[l].astype(F32)
        hm = _mlstm(proj, small_t, bias.reshape(-1, 1), row2(ml_norm_g[l]),
                    bsz, seq, cols, ml_dqk, ml_dv, ML_CHUNKS_PER_STEP)
        al = gdn_a_log[l].astype(F32)
        dt = gdn_dt_bias[l].astype(F32)
        og = _gdn(proj, small_t, gdn_conv_w[l].astype(F32), al.reshape(-1, 1), dt.reshape(-1, 1),
                  row2(gdn_norm_g[l]), bsz, seq, cols, n_qk, n_v, ga_off, gb_off, GDN_CHUNKS_PER_STEP)
        h1 = _merge(hm, og, proj, h0, w_branch_ml[l].astype(BF16), w_branch_gdn[l].astype(BF16),
                    w_mix_out[l].astype(BF16), row2(ln1_g[l]), row2(ln1_b[l]), cols, _pick(t, 512))

        kmem, vmem = _kv_proj(mem.reshape(bsz * mem_len, d), xa_wk[l].astype(BF16), xa_wv[l].astype(BF16),
                              _pick(bsz * mem_len, 512))
        w_r = jnp.concatenate([w_router[l], jnp.zeros((d, LANES - N_EXPERTS), F32)], axis=1).astype(BF16)
        b_r = jnp.concatenate([b_router[l].astype(F32), jnp.full((LANES - N_EXPERTS,), NEG_BIG, F32)]).reshape(1, -1)
        tm_x = _pick(seq, 512)
        h2, gw, route, cnt = _xattn(h1, kmem, vmem, xa_wq[l].astype(BF16), xa_wo[l].astype(BF16),
                                    row2(ln2_g[l]), row2(ln2_b[l]), w_r, b_r, bsz, seq, mem_len, tm_x)

        counts = cnt[0, :N_EXPERTS].astype(jnp.int32)
        padded = (counts + MOE_BLOCK - 1) // MOE_BLOCK * MOE_BLOCK
        pad_end = jnp.cumsum(padded)
        pad_start = pad_end - padded
        n_asg = t * TOP_K
        n_blocks = -(-n_asg // MOE_BLOCK) + N_EXPERTS
        n_rows = n_blocks * MOE_BLOCK
        top_e = route[:, :TOP_K]
        rank = route[:, TOP_K:2 * TOP_K]
        sel = top_e[:, :, None] == jnp.arange(N_EXPERTS, dtype=jnp.int32)[None, None, :]
        dest = (jnp.sum(jnp.where(sel, pad_start[None, None, :], 0), axis=-1) + rank).astype(jnp.int32)
        dest_flat = dest.reshape(n_asg)
        blk_row = jnp.arange(n_blocks, dtype=jnp.int32) * MOE_BLOCK
        block_e = jnp.minimum(jnp.sum(pad_end[None, :] <= blk_row[:, None], axis=1), N_EXPERTS - 1).astype(jnp.int32)
        nact = (pad_end[-1:] // MOE_BLOCK).astype(jnp.int32)
        pad_lo = (pad_start + counts).astype(jnp.int32)
        pad_n = (padded - counts).astype(jnp.int32)

        tm_d = _pick(t, 256)
        xs = _dispatch(pad_lo, pad_n, nact, dest_flat, h2, n_rows, _pick(t, 512))
        yb = _experts(block_e, nact, xs, w_gu[l], b_gu[l].astype(F32)[:, None, :],
                      w_dn[l], b_dn[l].astype(F32)[:, None, :])
        h = _combine(dest_flat, gw, h2, row2(ln3_g[l]), row2(ln3_b[l]), yb, tm_d)
    return h.reshape(bsz, seq, d)
```

```python
import functools

import jax
import jax.numpy as jnp
from jax import lax
from jax.experimental import pallas as pl
from jax.experimental.pallas import tpu as pltpu

F32 = jnp.float32
BF16 = jnp.bfloat16

CHUNK = 64
ML_HEADS = 4
GDN_DK = 128
GDN_DV = 128
CONV_K = 4
XA_HEADS = 4
N_EXPERTS = 32
TOP_K = 4
SWIGLU_LIMIT = 7.0
SWIGLU_ALPHA = 1.702
MOE_BLOCK = 512
DEPTH = 1
DN_ALPHA = (2 * DEPTH) ** 0.25
LN_EPS = 1e-5
RMS_EPS = 1e-6
LANES = 128
SUBLANES = 8
VMEM_LIMIT = 56 * 1024 * 1024
NEG_BIG = -1e30
ML_CHUNKS_PER_STEP = 8
GDN_CHUNKS_PER_STEP = 4
GDN_CONV_COLS = 512


def _params(*sem):
    return pltpu.CompilerParams(dimension_semantics=sem, vmem_limit_bytes=VMEM_LIMIT)


def _dot(a, b):
    return jnp.dot(a, b, preferred_element_type=F32)


def _dot_nt(a, b):
    return lax.dot_general(a, b, (((1,), (1,)), ((), ())), preferred_element_type=F32)


def _layer_norm(x, g, b):
    mu = jnp.mean(x, axis=-1, keepdims=True)
    xc = x - mu
    var = jnp.mean(xc * xc, axis=-1, keepdims=True)
    return xc * lax.rsqrt(var + LN_EPS) * g + b


def _sigmoid(x):
    return 1.0 / (1.0 + jnp.exp(-x))


def _log_sigmoid(x):
    return jnp.minimum(x, 0.0) - jnp.log(1.0 + jnp.exp(-jnp.abs(x)))


def _softplus(x):
    return jnp.maximum(x, 0.0) + jnp.log(1.0 + jnp.exp(-jnp.abs(x)))


def _split3(x):
    hi = x.astype(BF16)
    r = x - hi.astype(F32)
    mid = r.astype(BF16)
    lo = (r - mid.astype(F32)).astype(BF16)
    return hi, mid, lo


def _cumsum_lanes(x, upper):
    hi, mid, lo = _split3(x)
    return _dot(hi, upper) + _dot(mid, upper) + _dot(lo, upper)


def _ln_proj_kernel(x_ref, g_ref, b_ref, w_ref, ws_ref, h_ref, p_ref, s_ref, xn_ref):
    @pl.when(pl.program_id(1) == 0)
    def _():
        h = _layer_norm(x_ref[...], g_ref[...], b_ref[...])
        h_ref[...] = h
        hb = h.astype(BF16)
        xn_ref[...] = hb
        s = _dot(hb, ws_ref[...])
        for c in range(s_ref.shape[0]):
            s_ref[c] = s[c * CHUNK:(c + 1) * CHUNK, :].T

    p_ref[...] = _dot(xn_ref[...], w_ref[...]).astype(BF16)


def _ln_proj(x2, g, b, w_big, w_small, tm, tn):
    t, d = x2.shape
    nw = w_big.shape[1]
    return pl.pallas_call(
        _ln_proj_kernel,
        grid=(t // tm, nw // tn),
        in_specs=[
            pl.BlockSpec((tm, d), lambda i, j: (i, 0)),
            pl.BlockSpec((1, d), lambda i, j: (0, 0)),
            pl.BlockSpec((1, d), lambda i, j: (0, 0)),
            pl.BlockSpec((d, tn), lambda i, j: (0, j)),
            pl.BlockSpec((d, LANES), lambda i, j: (0, 0)),
        ],
        out_specs=[
            pl.BlockSpec((tm, d), lambda i, j: (i, 0)),
            pl.BlockSpec((tm, tn), lambda i, j: (i, j)),
            pl.BlockSpec((tm // CHUNK, LANES, CHUNK), lambda i, j: (i, 0, 0)),
        ],
        out_shape=[
            jax.ShapeDtypeStruct((t, d), F32),
            jax.ShapeDtypeStruct((t, nw), BF16),
            jax.ShapeDtypeStruct((t // CHUNK, LANES, CHUNK), F32),
        ],
        scratch_shapes=[pltpu.VMEM((tm, d), BF16)],
        compiler_params=_params("arbitrary", "arbitrary"),
        name="ln_in_proj",
    )(x2, g, b, w_big, w_small)


def _mlstm_kernel(q_ref, k_ref, v_ref, o_ref, gr_ref, br_ref, ng_ref,
                  out_ref, c_ref, m_ref, sqk_ref, kt_ref, p_ref, ktw_ref, b3_ref, li3_ref, kw3_ref,
                  gs_ref, mc_ref, stb_ref, stm_ref, std_ref, num_ref, kv_ref, *, dqk, dv, n_ch):
    L = CHUNK
    H = ML_HEADS

    @pl.when(pl.program_id(1) == 0)
    def _():
        c_ref[...] = jnp.zeros_like(c_ref)
        m_ref[...] = jnp.zeros_like(m_ref)

    ii = lax.broadcasted_iota(jnp.int32, (L, L), 0)
    jj = lax.broadcasted_iota(jnp.int32, (L, L), 1)
    causal = jj <= ii
    diag = ii == jj
    upper = (ii <= jj).astype(BF16)
    scale = dqk ** -0.5
    heads = [(g, h) for g in range(n_ch) for h in range(H)]

    for g, h in heads:
        rows = slice(g * L, (g + 1) * L)
        kf = k_ref[rows, h * dqk:(h + 1) * dqk].astype(F32) * scale
        sqk_ref[g * H + h] = _dot_nt(q_ref[rows, h * dqk:(h + 1) * dqk], kf.astype(BF16))
        kt_ref[g * H + h] = kf.T

    pre = gr_ref[:, 0:2 * H, :].reshape(n_ch * 2 * H, L) + jnp.concatenate([br_ref[...]] * n_ch, axis=0)
    b_all = _cumsum_lanes(_log_sigmoid(pre), upper)
    gs_all = b_all[:, L - 1:L]
    a_all = gs_all - b_all + pltpu.roll(pre, H, axis=0)
    mc_all = jnp.max(a_all, axis=1, keepdims=True)
    kw_all = jnp.exp(a_all - mc_all)
    gs_ref[...] = jnp.broadcast_to(gs_all, gs_ref.shape)
    mc_ref[...] = jnp.broadcast_to(mc_all, mc_ref.shape)
    for g, h in heads:
        r = g * 2 * H + H + h
        b3_ref[g * H + h] = b_all[r:r + 1, :]
        li3_ref[g * H + h] = pre[r - H:r - H + 1, :]
        kw3_ref[g * H + h] = kw_all[r:r + 1, :]

    b_r = b3_ref[...]
    b_c = jnp.sum(jnp.where(diag, b_r, 0.0), axis=2, keepdims=True)
    dmat = jnp.where(causal, b_c - b_r + li3_ref[...], -jnp.inf)
    m_intra = jnp.max(dmat, axis=2, keepdims=True)
    p = jnp.exp(dmat - m_intra) * sqk_ref[...]
    p_ref[...] = p.astype(BF16)
    stb_ref[...] = jnp.broadcast_to(b_c, stb_ref.shape)
    stm_ref[...] = jnp.broadcast_to(m_intra, stm_ref.shape)
    std_ref[...] = jnp.broadcast_to(jnp.sum(p, axis=2, keepdims=True), std_ref.shape)
    ktw_ref[...] = (kt_ref[...] * kw3_ref[...]).astype(BF16)

    ones = jnp.ones((L, LANES), BF16)
    for g, h in heads:
        vh = v_ref[g * L:(g + 1) * L, h * dv:(h + 1) * dv]
        num_ref[g * H + h] = _dot(p_ref[g * H + h], vh)
        kv_ref[g * H + h] = _dot(ktw_ref[g * H + h], jnp.concatenate([vh, ones], axis=1))

    rep = dv // LANES
    wide = lambda s: jnp.concatenate([s] * rep, axis=1)
    for g, h in heads:
        i = g * H + h
        r = g * 2 * H + H + h
        rows = slice(g * L, (g + 1) * L)
        c_st = c_ref[h]
        m_st = m_ref[h:h + 1, :]
        qc = _dot(q_ref[rows, h * dqk:(h + 1) * dqk], c_st.astype(BF16))
        m_intra = stm_ref[i]
        inter_log = stb_ref[i] + m_st
        m_out = jnp.maximum(inter_log, m_intra)
        s_inter = jnp.exp(inter_log - m_out)
        s_intra = jnp.exp(m_intra - m_out)
        num = wide(s_inter) * qc[:, 0:dv] + wide(s_intra) * num_ref[i]
        den = s_inter * qc[:, dv:dv + LANES] + s_intra * std_ref[i]
        hh = num / wide(jnp.maximum(jnp.abs(den), jnp.exp(-m_out)))
        gs = gs_ref[r:r + 1, :]
        mc = mc_ref[r:r + 1, :]
        m_new = jnp.maximum(gs + m_st, mc)
        dec = jnp.exp(gs + m_st - m_new)
        s_new = jnp.exp(mc - m_new)
        c_ref[h] = (jnp.concatenate([dec] * (rep + 1), axis=1) * c_st
                    + jnp.concatenate([s_new] * (rep + 1), axis=1) * kv_ref[i])
        m_ref[h:h + 1, :] = m_new

        rms = lax.rsqrt(jnp.mean(hh * hh, axis=1, keepdims=True) + RMS_EPS)
        og = _sigmoid(o_ref[rows, h * dv:(h + 1) * dv].astype(F32))
        out_ref[rows, h * dv:(h + 1) * dv] = (hh * rms * ng_ref[:, h * dv:(h + 1) * dv] * og).astype(BF16)


def _mlstm(proj, small_t, bias_r, norm_g, bsz, seq, cols, dqk, dv, n_ch):
    n = seq // (CHUNK * n_ch)
    L = CHUNK
    R = n_ch * L
    H = ML_HEADS
    qk_w, v_w = H * dqk, H * dv
    t = bsz * seq
    nh = n_ch * H
    kern = functools.partial(_mlstm_kernel, dqk=dqk, dv=dv, n_ch=n_ch)
    row = lambda b, c: b * n + c
    return pl.pallas_call(
        kern,
        grid=(bsz, n),
        in_specs=[
            pl.BlockSpec((R, qk_w), lambda b, c: (row(b, c), cols["mq"] // qk_w)),
            pl.BlockSpec((R, qk_w), lambda b, c: (row(b, c), cols["mk"] // qk_w)),
            pl.BlockSpec((R, v_w), lambda b, c: (row(b, c), cols["mv"] // v_w)),
            pl.BlockSpec((R, v_w), lambda b, c: (row(b, c), cols["mo"] // v_w)),
            pl.BlockSpec((n_ch, small_t.shape[1], L), lambda b, c: (row(b, c), 0, 0)),
            pl.BlockSpec(bias_r.shape, lambda b, c: (0, 0)),
            pl.BlockSpec((1, v_w), lambda b, c: (0, 0)),
        ],
        out_specs=pl.BlockSpec((R, v_w), lambda b, c: (row(b, c), 0)),
        out_shape=jax.ShapeDtypeStruct((t, v_w), BF16),
        scratch_shapes=[
            pltpu.VMEM((H, dqk, dv + LANES), F32),
            pltpu.VMEM((8, LANES), F32),
            pltpu.VMEM((nh, L, L), F32),
            pltpu.VMEM((nh, dqk, L), F32),
            pltpu.VMEM((nh, L, L), BF16),
            pltpu.VMEM((nh, dqk, L), BF16),
            pltpu.VMEM((nh, 1, L), F32),
            pltpu.VMEM((nh, 1, L), F32),
            pltpu.VMEM((nh, 1, L), F32),
            pltpu.VMEM((n_ch * 2 * H, LANES), F32),
            pltpu.VMEM((n_ch * 2 * H, LANES), F32),
            pltpu.VMEM((nh, L, LANES), F32),
            pltpu.VMEM((nh, L, LANES), F32),
            pltpu.VMEM((nh, L, LANES), F32),
            pltpu.VMEM((nh, L, dv), F32),
            pltpu.VMEM((nh, dqk, dv + LANES), F32),
        ],
        compiler_params=_params("arbitrary", "arbitrary"),
        name="mlstm",
    )(proj, proj, proj, proj, small_t, bias_r, norm_g)


def _gdn_kernel(x_ref, z_ref, gr_ref, cw_ref, al_ref, dt_ref, ng_ref,
                out_ref, s_ref, xs_ref, c_ref, qn_ref, kn_ref, knt_ref, kk_ref, qk_ref, pw_ref, x_ref_, rhs_ref,
                attn_ref, qd_ref, kdt_ref, gam3_ref, beta3_ref, kdwb3_ref, gt_ref, sol_ref, vn_ref,
                *, n_qk, n_v, ga_off, gb_off, n_ch):
    L = CHUNK
    R = n_ch * L
    dk, dv = GDN_DK, GDN_DV
    qk_w = n_qk * dk
    rep = n_v // n_qk
    conv_ch = 2 * qk_w + n_v * dv

    @pl.when(pl.program_id(1) == 0)
    def _():
        s_ref[...] = jnp.zeros_like(s_ref)
        xs_ref[0:8, :] = jnp.zeros((8, xs_ref.shape[1]), F32)

    xs_ref[8:8 + R, :] = x_ref[...].astype(F32)
    for cb in range(0, conv_ch, GDN_CONV_COLS):
        cs = slice(cb, cb + GDN_CONV_COLS)
        xa = xs_ref[0:8 + R, cs]
        conv = cw_ref[CONV_K - 1:CONV_K, cs] * xa[8:8 + R]
        for sh in range(1, CONV_K):
            conv = conv + cw_ref[CONV_K - 1 - sh:CONV_K - sh, cs] * pltpu.roll(xa, sh, axis=0)[8:8 + R]
        c_ref[:, cs] = conv * _sigmoid(conv)
    xs_ref[0:8, :] = xs_ref[R:R + 8, :]

    ii = lax.broadcasted_iota(jnp.int32, (L, L), 0)
    jj = lax.broadcasted_iota(jnp.int32, (L, L), 1)
    incl = jj <= ii
    strict = jj < ii
    diag = ii == jj
    eye = diag.astype(F32)
    upper = (ii <= jj).astype(BF16)

    for g in range(n_ch):
        rows = slice(g * L, (g + 1) * L)
        for hk in range(n_qk):
            iq = g * n_qk + hk
            cq = c_ref[rows, hk * dk:(hk + 1) * dk]
            ck = c_ref[rows, qk_w + hk * dk:qk_w + (hk + 1) * dk]
            qn = cq * lax.rsqrt(jnp.sum(cq * cq, axis=1, keepdims=True) + RMS_EPS) * (dk ** -0.5)
            kn = ck * lax.rsqrt(jnp.sum(ck * ck, axis=1, keepdims=True) + RMS_EPS)
            qn_ref[iq] = qn
            kn_ref[iq] = kn
            knt_ref[iq] = kn.T
            kb = kn.astype(BF16)
            kk_ref[iq] = _dot_nt(kb, kb)
            qk_ref[iq] = _dot_nt(qn.astype(BF16), kb)

    for g in range(n_ch):
        gd = -jnp.exp(al_ref[...]) * _softplus(gr_ref[g, ga_off:ga_off + n_v, :] + dt_ref[...])
        gam = _cumsum_lanes(gd, upper)
        beta = _sigmoid(gr_ref[g, gb_off:gb_off + n_v, :])
        g_tot = gam[:, L - 1:L]
        kdwb = jnp.exp(g_tot - gam) * beta
        gt_ref[g * n_v:(g + 1) * n_v, :] = jnp.broadcast_to(jnp.exp(g_tot), (n_v, LANES))
        for h in range(n_v):
            gam3_ref[g * n_v + h] = gam[h:h + 1, :]
            beta3_ref[g * n_v + h] = beta[h:h + 1, :]
            kdwb3_ref[g * n_v + h] = kdwb[h:h + 1, :]

    for g in range(n_ch):
        sv = slice(g * n_v, (g + 1) * n_v)
        sq = slice(g * n_qk, (g + 1) * n_qk)
        rows = slice(g * L, (g + 1) * L)
        gam_r = gam3_ref[sv]
        beta_r = beta3_ref[sv]
        gam_c = jnp.sum(jnp.where(diag, gam_r, 0.0), axis=2, keepdims=True)
        decm = jnp.exp(jnp.where(incl, gam_c - gam_r, -jnp.inf))
        db = decm * beta_r
        a = jnp.where(strict, jnp.repeat(kk_ref[sq], rep, axis=0) * db, 0.0)
        pw_ref[sv] = a
        x_ref_[sv] = eye - a
        attn_ref[sv] = (jnp.repeat(qk_ref[sq], rep, axis=0) * db).astype(BF16)
        eg_c = jnp.exp(gam_c)
        rhs_ref[sv, :, dv:dv + dk] = (jnp.repeat(kn_ref[sq], rep, axis=0) * eg_c).astype(BF16)
        qd_ref[sv] = (jnp.repeat(qn_ref[sq], rep, axis=0) * eg_c).astype(BF16)
        kdt_ref[sv] = (jnp.repeat(knt_ref[sq], rep, axis=0) * kdwb3_ref[sv]).astype(BF16)
        for h in range(n_v):
            rhs_ref[g * n_v + h, :, 0:dv] = c_ref[rows, 2 * qk_w + h * dv:2 * qk_w + (h + 1) * dv].astype(BF16)

    n_sq = 5
    for lvl in range(n_sq + 1):
        for i in range(n_ch * n_v):
            pwb = pw_ref[i].astype(BF16)
            if lvl > 0:
                xv = x_ref_[i]
                x_ref_[i] = xv + _dot(xv.astype(BF16), pwb)
            if lvl < n_sq:
                pw_ref[i] = _dot(pwb, pwb)

    for i in range(n_ch * n_v):
        sol_ref[i] = _dot(x_ref_[i].astype(BF16), rhs_ref[i])

    for g in range(n_ch):
        rows = slice(g * L, (g + 1) * L)
        for h in range(n_v):
            sol = sol_ref[g * n_v + h]
            vn_ref[h] = (sol[:, 0:dv] - _dot(sol[:, dv:dv + dk].astype(BF16), s_ref[h].astype(BF16))).astype(BF16)
        for h in range(n_v):
            i = g * n_v + h
            s_st = s_ref[h]
            vnb = vn_ref[h]
            o = _dot(qd_ref[i], s_st.astype(BF16)) + _dot(attn_ref[i], vnb)
            s_ref[h] = gt_ref[i:i + 1, :] * s_st + _dot(kdt_ref[i], vnb)
            rms = lax.rsqrt(jnp.mean(o * o, axis=1, keepdims=True) + RMS_EPS)
            zz = z_ref[rows, h * dv:(h + 1) * dv].astype(F32)
            out_ref[rows, h * dv:(h + 1) * dv] = (o * rms * ng_ref[...] * (zz * _sigmoid(zz))).astype(BF16)


def _gdn(proj, small_t, conv_w, al, dt, norm_g, bsz, seq, cols, n_qk, n_v, ga_off, gb_off, n_ch):
    n = seq // (CHUNK * n_ch)
    L = CHUNK
    R = n_ch * L
    conv_ch = 2 * n_qk * GDN_DK + n_v * GDN_DV
    v_w = n_v * GDN_DV
    t = bsz * seq
    nq, nv = n_ch * n_qk, n_ch * n_v
    kern = functools.partial(_gdn_kernel, n_qk=n_qk, n_v=n_v, ga_off=ga_off, gb_off=gb_off, n_ch=n_ch)
    row = lambda b, c: b * n + c
    full = lambda a: pl.BlockSpec(a.shape, lambda b, c: (0,) * a.ndim)
    return pl.pallas_call(
        kern,
        grid=(bsz, n),
        in_specs=[
            pl.BlockSpec((R, conv_ch), lambda b, c: (row(b, c), cols["gqkv"] // conv_ch)),
            pl.BlockSpec((R, v_w), lambda b, c: (row(b, c), cols["gz"] // v_w)),
            pl.BlockSpec((n_ch, small_t.shape[1], L), lambda b, c: (row(b, c), 0, 0)),
            full(conv_w), full(al), full(dt), full(norm_g),
        ],
        out_specs=pl.BlockSpec((R, v_w), lambda b, c: (row(b, c), 0)),
        out_shape=jax.ShapeDtypeStruct((t, v_w), BF16),
        scratch_shapes=[
            pltpu.VMEM((n_v, GDN_DK, GDN_DV), F32),
            pltpu.VMEM((8 + R + 8, conv_ch), F32),
            pltpu.VMEM((R, conv_ch), F32),
            pltpu.VMEM((nq, L, GDN_DK), F32),
            pltpu.VMEM((nq, L, GDN_DK), F32),
            pltpu.VMEM((nq, GDN_DK, L), F32),
            pltpu.VMEM((nq, L, L), F32),
            pltpu.VMEM((nq, L, L), F32),
            pltpu.VMEM((nv, L, L), F32),
            pltpu.VMEM((nv, L, L), F32),
            pltpu.VMEM((nv, L, GDN_DV + GDN_DK), BF16),
            pltpu.VMEM((nv, L, L), BF16),
            pltpu.VMEM((nv, L, GDN_DK), BF16),
            pltpu.VMEM((nv, GDN_DK, L), BF16),
            pltpu.VMEM((nv, 1, L), F32),
            pltpu.VMEM((nv, 1, L), F32),
            pltpu.VMEM((nv, 1, L), F32),
            pltpu.VMEM((nv, LANES), F32),
            pltpu.VMEM((nv, L, GDN_DV + GDN_DK), F32),
            pltpu.VMEM((n_v, L, GDN_DV), BF16),
        ],
        compiler_params=_params("arbitrary", "arbitrary"),
        name="gdn",
    )(proj, proj, small_t, conv_w, al, dt, norm_g)


def _merge_kernel(hm_ref, og_ref, gm_ref, gg_ref, h0_ref, wbm_ref, wbg_ref, wo_ref, g_ref, b_ref, out_ref):
    y_ml = _dot(hm_ref[...], wbm_ref[...])
    y_gdn = _dot(og_ref[...], wbg_ref[...])
    merged = _sigmoid(gm_ref[...].astype(F32)) * y_ml + _sigmoid(gg_ref[...].astype(F32)) * y_gdn
    mix = _dot(merged.astype(BF16), wo_ref[...])
    out_ref[...] = _layer_norm(DN_ALPHA * h0_ref[...] + mix, g_ref[...], b_ref[...])


def _merge(hm, og, proj, h0, w_bm, w_bg, w_out, g, b, cols, tm):
    t, d = h0.shape
    full = lambda a: pl.BlockSpec(a.shape, lambda i: (0,) * a.ndim)
    return pl.pallas_call(
        _merge_kernel,
        grid=(t // tm,),
        in_specs=[
            pl.BlockSpec((tm, hm.shape[1]), lambda i: (i, 0)),
            pl.BlockSpec((tm, og.shape[1]), lambda i: (i, 0)),
            pl.BlockSpec((tm, d), lambda i: (i, cols["gate_ml"] // d)),
            pl.BlockSpec((tm, d), lambda i: (i, cols["gate_gdn"] // d)),
            pl.BlockSpec((tm, d), lambda i: (i, 0)),
            full(w_bm), full(w_bg), full(w_out), full(g), full(b),
        ],
        out_specs=pl.BlockSpec((tm, d), lambda i: (i, 0)),
        out_shape=jax.ShapeDtypeStruct((t, d), F32),
        compiler_params=_params("arbitrary"),
        name="merge_out_ln1",
    )(hm, og, proj, proj, h0, w_bm, w_bg, w_out, g, b)


def _kv_kernel(m_ref, wk_ref, wv_ref, k_ref, v_ref):
    mb = m_ref[...].astype(BF16)
    k_ref[...] = _dot(mb, wk_ref[...]).astype(BF16)
    v_ref[...] = _dot(mb, wv_ref[...]).astype(BF16)


def _kv_proj(mem2, wk, wv, tm):
    t, d = mem2.shape
    full = lambda a: pl.BlockSpec(a.shape, lambda i: (0,) * a.ndim)
    return pl.pallas_call(
        _kv_kernel,
        grid=(t // tm,),
        in_specs=[pl.BlockSpec((tm, d), lambda i: (i, 0)), full(wk), full(wv)],
        out_specs=[pl.BlockSpec((tm, d), lambda i: (i, 0))] * 2,
        out_shape=[jax.ShapeDtypeStruct((t, d), BF16)] * 2,
        compiler_params=_params("arbitrary"),
        name="mem_kv_proj",
    )(mem2, wk, wv)


def _xattn_kernel(h1_ref, k_ref, v_ref, wq_ref, wo_ref, g_ref, b_ref, wr_ref, br_ref,
                  h2_ref, gw_ref, route_ref, cnt_ref, carry_ref, *, dh):
    tm = h1_ref.shape[0]

    @pl.when((pl.program_id(0) == 0) & (pl.program_id(1) == 0))
    def _():
        carry_ref[...] = jnp.zeros_like(carry_ref)

    h1 = h1_ref[...]
    q = _dot(h1.astype(BF16), wq_ref[...])
    outs = []
    for hd in range(XA_HEADS):
        qh = q[:, hd * dh:(hd + 1) * dh].astype(BF16)
        kh = k_ref[:, hd * dh:(hd + 1) * dh]
        vh = v_ref[:, hd * dh:(hd + 1) * dh]
        sc = _dot_nt(qh, kh) * (dh ** -0.5)
        e = jnp.exp(sc - jnp.max(sc, axis=1, keepdims=True))
        p = e / jnp.sum(e, axis=1, keepdims=True)
        outs.append(_dot(p.astype(BF16), vh))
    o = jnp.concatenate(outs, axis=1)
    xa = _dot(o.astype(BF16), wo_ref[...])
    h2 = _layer_norm(DN_ALPHA * h1 + xa, g_ref[...], b_ref[...])
    h2_ref[...] = h2

    logits = _dot(h2.astype(BF16), wr_ref[...]) + br_ref[...]
    lane = lax.broadcasted_iota(jnp.int32, (tm, LANES), 1)
    lane_f = lane.astype(F32)
    work = logits
    vals, idxs = [], []
    for _ in range(TOP_K):
        m = jnp.max(work, axis=1, keepdims=True)
        idx = jnp.min(jnp.where(work == m, lane_f, float(LANES)), axis=1, keepdims=True)
        vals.append(m)
        idxs.append(idx)
        work = jnp.where(lane_f == idx, -jnp.inf, work)
    es = [jnp.exp(v - vals[0]) for v in vals]
    tot = es[0]
    for e_ in es[1:]:
        tot = tot + e_
    onehot = jnp.zeros((tm, LANES), F32)
    for idx in idxs:
        onehot = onehot + (lane_f == idx).astype(F32)
    ri = lax.broadcasted_iota(jnp.int32, (tm, tm), 0)
    ci = lax.broadcasted_iota(jnp.int32, (tm, tm), 1)
    tri = (ci < ri).astype(BF16)
    carry = carry_ref[0:1, :]
    ranks = carry + _dot(tri, onehot.astype(BF16))
    gw = jnp.zeros((tm, LANES), F32)
    route = jnp.zeros((tm, LANES), F32)
    for k_ in range(TOP_K):
        rk = jnp.sum(jnp.where(lane_f == idxs[k_], ranks, 0.0), axis=1, keepdims=True)
        gw = gw + jnp.where(lane == k_, es[k_] / tot, 0.0)
        route = route + jnp.where(lane == k_, idxs[k_], 0.0) + jnp.where(lane == TOP_K + k_, rk, 0.0)
    gw_ref[...] = gw
    route_ref[...] = route.astype(jnp.int32)
    carry = carry + jnp.sum(onehot, axis=0, keepdims=True)
    carry_ref[...] = jnp.broadcast_to(carry, carry_ref.shape)
    cnt_ref[...] = jnp.broadcast_to(carry, cnt_ref.shape)


def _xattn(h1, kmem, vmem, wq, wo, g, b, w_r, b_r, bsz, seq, mem_len, tm):
    t, d = h1.shape
    nt = seq // tm
    full = lambda a: pl.BlockSpec(a.shape, lambda i, j: (0,) * a.ndim)
    kern = functools.partial(_xattn_kernel, dh=d // XA_HEADS)
    return pl.pallas_call(
        kern,
        grid=(bsz, nt),
        in_specs=[
            pl.BlockSpec((tm, d), lambda i, j: (i * nt + j, 0)),
            pl.BlockSpec((mem_len, d), lambda i, j: (i, 0)),
            pl.BlockSpec((mem_len, d), lambda i, j: (i, 0)),
            full(wq), full(wo), full(g), full(b), full(w_r), full(b_r),
        ],
        out_specs=[
            pl.BlockSpec((tm, d), lambda i, j: (i * nt + j, 0)),
            pl.BlockSpec((tm, LANES), lambda i, j: (i * nt + j, 0)),
            pl.BlockSpec((tm, LANES), lambda i, j: (i * nt + j, 0)),
            pl.BlockSpec((8, LANES), lambda i, j: (0, 0)),
        ],
        out_shape=[
            jax.ShapeDtypeStruct((t, d), F32),
            jax.ShapeDtypeStruct((t, LANES), F32),
            jax.ShapeDtypeStruct((t, LANES), jnp.int32),
            jax.ShapeDtypeStruct((8, LANES), F32),
        ],
        scratch_shapes=[pltpu.VMEM((8, LANES), F32)],
        compiler_params=_params("arbitrary", "arbitrary"),
        name="xattn_ln2_router",
    )(h1, kmem, vmem, wq, wo, g, b, w_r, b_r)


def _dispatch_kernel(pad_lo_ref, pad_n_ref, na_ref, dest_ref, h_ref, xs_ref, zero_ref, sem, zsem):
    tm = h_ref.shape[0]
    step = pl.program_id(0)
    n_blocks = xs_ref.shape[0] // MOE_BLOCK

    @pl.when(step == 0)
    def _():
        zero_ref[...] = jnp.zeros_like(zero_ref)

        def per_expert(e, carry):
            n = pad_n_ref[e]
            pos = pad_lo_ref[e] + n

            def pieces(wait):
                def piece(start, p):
                    cp = pltpu.make_async_copy(zero_ref.at[pl.ds(0, p), :], xs_ref.at[pl.ds(start, p), :], zsem)
                    cp.wait() if wait else cp.start()

                at = pos
                p = MOE_BLOCK // 2
                while p >= SUBLANES:

                    @pl.when((n & p) != 0)
                    def _(p=p, at=at):
                        piece(pl.multiple_of(at - p, SUBLANES), p)

                    at = at - (n & p)
                    p //= 2
                for r in range(SUBLANES - 1):

                    @pl.when(r < (n & (SUBLANES - 1)))
                    def _(r=r, at=at):
                        piece(at - 1 - r, 1)

            pieces(wait=False)
            pieces(wait=True)
            return carry

        lax.fori_loop(0, N_EXPERTS, per_expert, 0)

        def tail(b, carry):
            half = MOE_BLOCK // 2
            cp = pltpu.make_async_copy(zero_ref, xs_ref.at[pl.ds(b * half, half), :], zsem)
            cp.start()
            cp.wait()
            return carry

        lax.fori_loop(2 * na_ref[0], 2 * n_blocks, tail, 0)

    for tok in range(tm):
        src = h_ref.at[pl.ds(tok, 1), :]
        for k_ in range(TOP_K):
            pltpu.make_async_copy(src, xs_ref.at[pl.ds(dest_ref[tok * TOP_K + k_], 1), :], sem).start(priority=k_ % 2)
    for _ in range(TOP_K):
        pltpu.make_async_copy(h_ref, xs_ref.at[pl.ds(0, tm), :], sem).wait()


def _dispatch(pad_lo, pad_n, nact, dest_flat, h2, n_rows, tm):
    t, d = h2.shape
    grid_spec = pltpu.PrefetchScalarGridSpec(
        num_scalar_prefetch=3,
        grid=(t // tm,),
        in_specs=[
            pl.BlockSpec((tm * TOP_K,), lambda i, *_: (i,), memory_space=pltpu.SMEM),
            pl.BlockSpec((tm, d), lambda i, *_: (i, 0)),
        ],
        out_specs=pl.BlockSpec(memory_space=pl.ANY),
        scratch_shapes=[pltpu.VMEM((MOE_BLOCK // 2, d), F32), pltpu.SemaphoreType.DMA(()),
                        pltpu.SemaphoreType.DMA(())],
    )
    return pl.pallas_call(
        _dispatch_kernel,
        grid_spec=grid_spec,
        out_shape=jax.ShapeDtypeStruct((n_rows, d), F32),
        compiler_params=_params("arbitrary"),
        name="moe_dispatch",
    )(pad_lo, pad_n, nact, dest_flat, h2)


def _expert_kernel(be_ref, na_ref, x_ref, wgu_ref, bgu_ref, wdn_ref, bdn_ref, y_ref, wgu_s, wdn_s, *, d_exp):
    blk = pl.program_id(0)
    active = blk < na_ref[0]

    @pl.when(jnp.logical_not(active))
    def _():
        y_ref[...] = jnp.zeros_like(y_ref)

    @pl.when(active & ((blk == 0) | (be_ref[blk] != be_ref[jnp.maximum(blk - 1, 0)])))
    def _():
        wgu_s[...] = wgu_ref[0].astype(BF16)
        wdn_s[...] = wdn_ref[0].astype(BF16)

    @pl.when(active)
    def _():
        gu = _dot(x_ref[...].astype(BF16), wgu_s[...]) + bgu_ref[0]
        gate = jnp.minimum(gu[:, :d_exp], SWIGLU_LIMIT)
        up = jnp.clip(gu[:, d_exp:], -SWIGLU_LIMIT, SWIGLU_LIMIT)
        act = (up + 1.0) * (gate * _sigmoid(SWIGLU_ALPHA * gate))
        y_ref[...] = _dot(act.astype(BF16), wdn_s[...]) + bdn_ref[0]


def _experts(block_e, nact, xs, w_gu, b_gu, w_dn, b_dn):
    n_rows, d = xs.shape
    n_blocks = n_rows // MOE_BLOCK
    d_exp = w_dn.shape[1]
    kern = functools.partial(_expert_kernel, d_exp=d_exp)
    emap = lambda i, be, na: (be[i], 0, 0)
    grid_spec = pltpu.PrefetchScalarGridSpec(
        num_scalar_prefetch=2,
        grid=(n_blocks,),
        in_specs=[
            pl.BlockSpec((MOE_BLOCK, d), lambda i, be, na: (i, 0)),
            pl.BlockSpec((1, d, 2 * d_exp), emap),
            pl.BlockSpec((1, 1, 2 * d_exp), emap),
            pl.BlockSpec((1, d_exp, d), emap),
            pl.BlockSpec((1, 1, d), emap),
        ],
        out_specs=pl.BlockSpec((MOE_BLOCK, d), lambda i, be, na: (i, 0)),
        scratch_shapes=[pltpu.VMEM((d, 2 * d_exp), BF16), pltpu.VMEM((d_exp, d), BF16)],
    )
    return pl.pallas_call(
        kern,
        grid_spec=grid_spec,
        out_shape=jax.ShapeDtypeStruct((n_rows, d), F32),
        compiler_params=_params("arbitrary"),
        name="moe_experts",
    )(block_e, nact, xs, w_gu, b_gu, w_dn, b_dn)


def _combine_kernel(dest_ref, dest_nxt_ref, gw_ref, h_ref, g_ref, b_ref, y_ref, out_ref, buf_a, buf_b, sem):
    tm = buf_a.shape[1]
    step = pl.program_id(0)
    n_steps = pl.num_programs(0)

    def issue(idx_ref, half, buf, s):
        for tok in range(tm):
            for k_ in range(TOP_K):
                pltpu.make_async_copy(y_ref.at[pl.ds(idx_ref[(half * tm + tok) * TOP_K + k_], 1), :],
                                      buf.at[k_, pl.ds(tok, 1), :], sem.at[s]).start(priority=k_ % 2)

    def consume(half, buf, s):
        for k_ in range(TOP_K):
            pltpu.make_async_copy(y_ref.at[pl.ds(0, tm), :], buf.at[k_], sem.at[s]).wait()
        rows = slice(half * tm, (half + 1) * tm)
        gw = gw_ref[rows, :]
        ff = buf[0] * gw[:, 0:1]
        for k_ in range(1, TOP_K):
            ff = ff + buf[k_] * gw[:, k_:k_ + 1]
        out_ref[rows, :] = _layer_norm(DN_ALPHA * h_ref[rows, :] + ff, g_ref[...], b_ref[...])

    @pl.when(step == 0)
    def _():
        issue(dest_ref, 0, buf_a, 0)

    issue(dest_ref, 1, buf_b, 1)
    consume(0, buf_a, 0)

    @pl.when(step + 1 < n_steps)
    def _():
        issue(dest_nxt_ref, 0, buf_a, 0)

    consume(1, buf_b, 1)


def _combine(dest_flat, gw, h2, g, b, yb, tm):
    t, d = h2.shape
    n = t // (2 * tm)
    return pl.pallas_call(
        _combine_kernel,
        grid=(n,),
        in_specs=[
            pl.BlockSpec((2 * tm * TOP_K,), lambda i: (i,), memory_space=pltpu.SMEM),
            pl.BlockSpec((2 * tm * TOP_K,), lambda i: (jnp.minimum(i + 1, n - 1),), memory_space=pltpu.SMEM),
            pl.BlockSpec((2 * tm, LANES), lambda i: (i, 0)),
            pl.BlockSpec((2 * tm, d), lambda i: (i, 0)),
            pl.BlockSpec((1, d), lambda i: (0, 0)),
            pl.BlockSpec((1, d), lambda i: (0, 0)),
            pl.BlockSpec(memory_space=pl.ANY),
        ],
        out_specs=pl.BlockSpec((2 * tm, d), lambda i: (i, 0)),
        out_shape=jax.ShapeDtypeStruct((t, d), F32),
        scratch_shapes=[pltpu.VMEM((TOP_K, tm, d), F32), pltpu.VMEM((TOP_K, tm, d), F32),
                        pltpu.SemaphoreType.DMA((2,))],
        compiler_params=_params("arbitrary"),
        name="moe_combine_ln3",
    )(dest_flat, dest_flat, gw, h2, g, b, yb)


def _pick(n, pref):
    return pref if n % pref == 0 else n


def kernel(x, mem, ln_in_g, ln_in_b, w_in, ml_gate_bias, ml_norm_g, gdn_conv_w, gdn_a_log, gdn_dt_bias, gdn_norm_g, w_branch_ml, w_branch_gdn, w_mix_out, ln1_g, ln1_b, xa_wq, xa_wk, xa_wv, xa_wo, ln2_g, ln2_b, w_router, b_router, w_gu, b_gu, w_dn, b_dn, ln3_g, ln3_b):
    bsz, seq, d = x.shape
    mem_len = mem.shape[1]
    t = bsz * seq
    ml_dv = d // ML_HEADS
    ml_dqk = ml_dv // 2
    ml_qk_w, ml_v_w = ML_HEADS * ml_dqk, ML_HEADS * ml_dv
    n_qk = d // GDN_DK
    n_v = 2 * n_qk
    gdn_qk_w, gdn_v_w = n_qk * GDN_DK, n_v * GDN_DV
    conv_ch = 2 * gdn_qk_w + gdn_v_w
    splits = (ml_qk_w, ml_qk_w, ml_v_w, ml_v_w, 2 * ML_HEADS, conv_ch, gdn_v_w, n_v, n_v, d, d)
    names = ("mq", "mk", "mv", "mo", "mif", "gqkv", "gz", "ga", "gb", "gate_ml", "gate_gdn")
    starts = {}
    acc = 0
    for nm, sz in zip(names, splits):
        starts[nm] = (acc, sz)
        acc += sz
    row2 = lambda a: a.reshape(1, -1).astype(F32)

    h = x.reshape(t, d)
    for l in range(DEPTH):
        w = w_in[l]
        seg = lambda nm: w[:, starts[nm][0]:starts[nm][0] + starts[nm][1]]
        order = ("gqkv", "gz", "mv", "mo", "gate_ml", "gate_gdn", "mq", "mk")
        cols = {}
        off = 0
        for nm in order:
            assert off % starts[nm][1] == 0
            cols[nm] = off
            off += starts[nm][1]
        w_big = jnp.concatenate([seg(nm) for nm in order], axis=1).astype(BF16)
        n_small = 2 * ML_HEADS + 2 * n_v
        w_small = jnp.concatenate([seg("mif"), seg("ga"), seg("gb"),
                                   jnp.zeros((d, LANES - n_small), F32)], axis=1).astype(BF16)
        ga_off, gb_off = 2 * ML_HEADS, 2 * ML_HEADS + n_v

        if l == 0:
            h0, proj, small_t = _ln_proj(h, row2(ln_in_g), row2(ln_in_b), w_big, w_small,
                                         _pick(t, 1024), _pick(off, 2816))
        else:
            raise NotImplementedError("DEPTH > 1")

        bias = ml_gate_bias[l].astype(F32)
        hm = _mlstm(proj, small_t, bias.reshape(-1, 1), row2(ml_norm_g[l]),
                    bsz, seq, cols, ml_dqk, ml_dv, ML_CHUNKS_PER_STEP)
        al = gdn_a_log[l].astype(F32)
        dt = gdn_dt_bias[l].astype(F32)
        og = _gdn(proj, small_t, gdn_conv_w[l].astype(F32), al.reshape(-1, 1), dt.reshape(-1, 1),
                  row2(gdn_norm_g[l]), bsz, seq, cols, n_qk, n_v, ga_off, gb_off, GDN_CHUNKS_PER_STEP)
        h1 = _merge(hm, og, proj, h0, w_branch_ml[l].astype(BF16), w_branch_gdn[l].astype(BF16),
                    w_mix_out[l].astype(BF16), row2(ln1_g[l]), row2(ln1_b[l]), cols, _pick(t, 512))

        kmem, vmem = _kv_proj(mem.reshape(bsz * mem_len, d), xa_wk[l].astype(BF16), xa_wv[l].astype(BF16),
                              _pick(bsz * mem_len, 512))
        w_r = jnp.concatenate([w_router[l], jnp.zeros((d, LANES - N_EXPERTS), F32)], axis=1).astype(BF16)
        b_r = jnp.concatenate([b_router[l].astype(F32), jnp.full((LANES - N_EXPERTS,), NEG_BIG, F32)]).reshape(1, -1)
        tm_x = _pick(seq, 512)
        h2, gw, route, cnt = _xattn(h1, kmem, vmem, xa_wq[l].astype(BF16), xa_wo[l].astype(BF16),
                                    row2(ln2_g[l]), row2(ln2_b[l]), w_r, b_r, bsz, seq, mem_len, tm_x)

        counts = cnt[0, :N_EXPERTS].astype(jnp.int32)
        padded = (counts + MOE_BLOCK - 1) // MOE_BLOCK * MOE_BLOCK
        pad_end = jnp.cumsum(padded)
        pad_start = pad_end - padded
        n_asg = t * TOP_K
        n_blocks = -(-n_asg // MOE_BLOCK) + N_EXPERTS
        n_rows = n_blocks * MOE_BLOCK
        top_e = route[:, :TOP_K]
        rank = route[:, TOP_K:2 * TOP_K]
        sel = top_e[:, :, None] == jnp.arange(N_EXPERTS, dtype=jnp.int32)[None, None, :]
        dest = (jnp.sum(jnp.where(sel, pad_start[None, None, :], 0), axis=-1) + rank).astype(jnp.int32)
        dest_flat = dest.reshape(n_asg)
        blk_row = jnp.arange(n_blocks, dtype=jnp.int32) * MOE_BLOCK
        block_e = jnp.minimum(jnp.sum(pad_end[None, :] <= blk_row[:, None], axis=1), N_EXPERTS - 1).astype(jnp.int32)
        nact = (pad_end[-1:] // MOE_BLOCK).astype(jnp.int32)
        pad_lo = (pad_start + counts).astype(jnp.int32)
        pad_n = (padded - counts).astype(jnp.int32)

        tm_d = _pick(t, 256)
        xs = _dispatch(pad_lo, pad_n, nact, dest_flat, h2, n_rows, _pick(t, 512))
        yb = _experts(block_e, nact, xs, w_gu[l], b_gu[l].astype(F32)[:, None, :],
                      w_dn[l], b_dn[l].astype(F32)[:, None, :])
        h = _combine(dest_flat, gw, h2, row2(ln3_g[l]), row2(ln3_b[l]), yb, tm_d)
    return h.reshape(bsz, seq, d)
```

```python
import functools

import jax
import jax.numpy as jnp
from jax import lax
from jax.experimental import pallas as pl
from jax.experimental.pallas import tpu as pltpu

F32 = jnp.float32
BF16 = jnp.bfloat16

CHUNK = 64
ML_HEADS = 4
GDN_DK = 128
GDN_DV = 128
CONV_K = 4
XA_HEADS = 4
N_EXPERTS = 32
TOP_K = 4
SWIGLU_LIMIT = 7.0
SWIGLU_ALPHA = 1.702
MOE_BLOCK = 512
DEPTH = 1
DN_ALPHA = (2 * DEPTH) ** 0.25
LN_EPS = 1e-5
RMS_EPS = 1e-6
LANES = 128
SUBLANES = 8
VMEM_LIMIT = 56 * 1024 * 1024
NEG_BIG = -1e30
ML_CHUNKS_PER_STEP = 8
GDN_CHUNKS_PER_STEP = 4
GDN_CONV_COLS = 512


def _params(*sem):
    return pltpu.CompilerParams(dimension_semantics=sem, vmem_limit_bytes=VMEM_LIMIT)


def _dot(a, b):
    return jnp.dot(a, b, preferred_element_type=F32)


def _dot_nt(a, b):
    return lax.dot_general(a, b, (((1,), (1,)), ((), ())), preferred_element_type=F32)


def _layer_norm(x, g, b):
    mu = jnp.mean(x, axis=-1, keepdims=True)
    xc = x - mu
    var = jnp.mean(xc * xc, axis=-1, keepdims=True)
    return xc * lax.rsqrt(var + LN_EPS) * g + b


def _sigmoid(x):
    return 1.0 / (1.0 + jnp.exp(-x))


def _log_sigmoid(x):
    return jnp.minimum(x, 0.0) - jnp.log(1.0 + jnp.exp(-jnp.abs(x)))


def _softplus(x):
    return jnp.maximum(x, 0.0) + jnp.log(1.0 + jnp.exp(-jnp.abs(x)))


def _split3(x):
    hi = x.astype(BF16)
    r = x - hi.astype(F32)
    mid = r.astype(BF16)
    lo = (r - mid.astype(F32)).astype(BF16)
    return hi, mid, lo


def _cumsum_lanes(x, upper):
    hi, mid, lo = _split3(x)
    return _dot(hi, upper) + _dot(mid, upper) + _dot(lo, upper)


def _ln_proj_kernel(x_ref, g_ref, b_ref, w_ref, ws_ref, h_ref, p_ref, s_ref, xn_ref):
    @pl.when(pl.program_id(1) == 0)
    def _():
        h = _layer_norm(x_ref[...], g_ref[...], b_ref[...])
        h_ref[...] = h
        hb = h.astype(BF16)
        xn_ref[...] = hb
        s = _dot(hb, ws_ref[...])
        for c in range(s_ref.shape[0]):
            s_ref[c] = s[c * CHUNK:(c + 1) * CHUNK, :].T

    p_ref[...] = _dot(xn_ref[...], w_ref[...]).astype(BF16)


def _ln_proj(x2, g, b, w_big, w_small, tm, tn):
    t, d = x2.shape
    nw = w_big.shape[1]
    return pl.pallas_call(
        _ln_proj_kernel,
        grid=(t // tm, nw // tn),
        in_specs=[
            pl.BlockSpec((tm, d), lambda i, j: (i, 0)),
            pl.BlockSpec((1, d), lambda i, j: (0, 0)),
            pl.BlockSpec((1, d), lambda i, j: (0, 0)),
            pl.BlockSpec((d, tn), lambda i, j: (0, j)),
            pl.BlockSpec((d, LANES), lambda i, j: (0, 0)),
        ],
        out_specs=[
            pl.BlockSpec((tm, d), lambda i, j: (i, 0)),
            pl.BlockSpec((tm, tn), lambda i, j: (i, j)),
            pl.BlockSpec((tm // CHUNK, LANES, CHUNK), lambda i, j: (i, 0, 0)),
        ],
        out_shape=[
            jax.ShapeDtypeStruct((t, d), F32),
            jax.ShapeDtypeStruct((t, nw), BF16),
            jax.ShapeDtypeStruct((t // CHUNK, LANES, CHUNK), F32),
        ],
        scratch_shapes=[pltpu.VMEM((tm, d), BF16)],
        compiler_params=_params("arbitrary", "arbitrary"),
        name="ln_in_proj",
    )(x2, g, b, w_big, w_small)


def _mlstm_kernel(q_ref, k_ref, v_ref, o_ref, gr_ref, br_ref, ng_ref,
                  out_ref, c_ref, m_ref, sqk_ref, kt_ref, p_ref, ktw_ref, b3_ref, li3_ref, kw3_ref,
                  gs_ref, mc_ref, stb_ref, stm_ref, std_ref, num_ref, kv_ref, *, dqk, dv, n_ch):
    L = CHUNK
    H = ML_HEADS

    @pl.when(pl.program_id(1) == 0)
    def _():
        c_ref[...] = jnp.zeros_like(c_ref)
        m_ref[...] = jnp.zeros_like(m_ref)

    ii = lax.broadcasted_iota(jnp.int32, (L, L), 0)
    jj = lax.broadcasted_iota(jnp.int32, (L, L), 1)
    causal = jj <= ii
    diag = ii == jj
    upper = (ii <= jj).astype(BF16)
    scale = dqk ** -0.5
    heads = [(g, h) for g in range(n_ch) for h in range(H)]

    for g, h in heads:
        rows = slice(g * L, (g + 1) * L)
        kf = k_ref[rows, h * dqk:(h + 1) * dqk].astype(F32) * scale
        sqk_ref[g * H + h] = _dot_nt(q_ref[rows, h * dqk:(h + 1) * dqk], kf.astype(BF16))
        kt_ref[g * H + h] = kf.T

    pre = gr_ref[:, 0:2 * H, :].reshape(n_ch * 2 * H, L) + jnp.concatenate([br_ref[...]] * n_ch, axis=0)
    b_all = _cumsum_lanes(_log_sigmoid(pre), upper)
    gs_all = b_all[:, L - 1:L]
    a_all = gs_all - b_all + pltpu.roll(pre, H, axis=0)
    mc_all = jnp.max(a_all, axis=1, keepdims=True)
    kw_all = jnp.exp(a_all - mc_all)
    gs_ref[...] = jnp.broadcast_to(gs_all, gs_ref.shape)
    mc_ref[...] = jnp.broadcast_to(mc_all, mc_ref.shape)
    for g, h in heads:
        r = g * 2 * H + H + h
        b3_ref[g * H + h] = b_all[r:r + 1, :]
        li3_ref[g * H + h] = pre[r - H:r - H + 1, :]
        kw3_ref[g * H + h] = kw_all[r:r + 1, :]

    b_r = b3_ref[...]
    b_c = jnp.sum(jnp.where(diag, b_r, 0.0), axis=2, keepdims=True)
    dmat = jnp.where(causal, b_c - b_r + li3_ref[...], -jnp.inf)
    m_intra = jnp.max(dmat, axis=2, keepdims=True)
    p = jnp.exp(dmat - m_intra) * sqk_ref[...]
    p_ref[...] = p.astype(BF16)
    stb_ref[...] = jnp.broadcast_to(b_c, stb_ref.shape)
    stm_ref[...] = jnp.broadcast_to(m_intra, stm_ref.shape)
    std_ref[...] = jnp.broadcast_to(jnp.sum(p, axis=2, keepdims=True), std_ref.shape)
    ktw_ref[...] = (kt_ref[...] * kw3_ref[...]).astype(BF16)

    ones = jnp.ones((L, LANES), BF16)
    for g, h in heads:
        vh = v_ref[g * L:(g + 1) * L, h * dv:(h + 1) * dv]
        num_ref[g * H + h] = _dot(p_ref[g * H + h], vh)
        kv_ref[g * H + h] = _dot(ktw_ref[g * H + h], jnp.concatenate([vh, ones], axis=1))

    rep = dv // LANES
    wide = lambda s: jnp.concatenate([s] * rep, axis=1)
    for g, h in heads:
        i = g * H + h
        r = g * 2 * H + H + h
        rows = slice(g * L, (g + 1) * L)
        c_st = c_ref[h]
        m_st = m_ref[h:h + 1, :]
        qc = _dot(q_ref[rows, h * dqk:(h + 1) * dqk], c_st.astype(BF16))
        m_intra = stm_ref[i]
        inter_log = stb_ref[i] + m_st
        m_out = jnp.maximum(inter_log, m_intra)
        s_inter = jnp.exp(inter_log - m_out)
        s_intra = jnp.exp(m_intra - m_out)
        num = wide(s_inter) * qc[:, 0:dv] + wide(s_intra) * num_ref[i]
        den = s_inter * qc[:, dv:dv + LANES] + s_intra * std_ref[i]
        hh = num / wide(jnp.maximum(jnp.abs(den), jnp.exp(-m_out)))
        gs = gs_ref[r:r + 1, :]
        mc = mc_ref[r:r + 1, :]
        m_new = jnp.maximum(gs + m_st, mc)
        dec = jnp.exp(gs + m_st - m_new)
        s_new = jnp.exp(mc - m_new)
        c_ref[h] = (jnp.concatenate([dec] * (rep + 1), axis=1) * c_st
                    + jnp.concatenate([s_new] * (rep + 1), axis=1) * kv_ref[i])
        m_ref[h:h + 1, :] = m_new

        rms = lax.rsqrt(jnp.mean(hh * hh, axis=1, keepdims=True) + RMS_EPS)
        og = _sigmoid(o_ref[rows, h * dv:(h + 1) * dv].astype(F32))
        out_ref[rows, h * dv:(h + 1) * dv] = (hh * rms * ng_ref[:, h * dv:(h + 1) * dv] * og).astype(BF16)


def _mlstm(proj, small_t, bias_r, norm_g, bsz, seq, cols, dqk, dv, n_ch):
    n = seq // (CHUNK * n_ch)
    L = CHUNK
    R = n_ch * L
    H = ML_HEADS
    qk_w, v_w = H * dqk, H * dv
    t = bsz * seq
    nh = n_ch * H
    kern = functools.partial(_mlstm_kernel, dqk=dqk, dv=dv, n_ch=n_ch)
    row = lambda b, c: b * n + c
    return pl.pallas_call(
        kern,
        grid=(bsz, n),
        in_specs=[
            pl.BlockSpec((R, qk_w), lambda b, c: (row(b, c), cols["mq"] // qk_w)),
            pl.BlockSpec((R, qk_w), lambda b, c: (row(b, c), cols["mk"] // qk_w)),
            pl.BlockSpec((R, v_w), lambda b, c: (row(b, c), cols["mv"] // v_w)),
            pl.BlockSpec((R, v_w), lambda b, c: (row(b, c), cols["mo"] // v_w)),
            pl.BlockSpec((n_ch, small_t.shape[1], L), lambda b, c: (row(b, c), 0, 0)),
            pl.BlockSpec(bias_r.shape, lambda b, c: (0, 0)),
            pl.BlockSpec((1, v_w), lambda b, c: (0, 0)),
        ],
        out_specs=pl.BlockSpec((R, v_w), lambda b, c: (row(b, c), 0)),
        out_shape=jax.ShapeDtypeStruct((t, v_w), BF16),
        scratch_shapes=[
            pltpu.VMEM((H, dqk, dv + LANES), F32),
            pltpu.VMEM((8, LANES), F32),
            pltpu.VMEM((nh, L, L), F32),
            pltpu.VMEM((nh, dqk, L), F32),
            pltpu.VMEM((nh, L, L), BF16),
            pltpu.VMEM((nh, dqk, L), BF16),
            pltpu.VMEM((nh, 1, L), F32),
            pltpu.VMEM((nh, 1, L), F32),
            pltpu.VMEM((nh, 1, L), F32),
            pltpu.VMEM((n_ch * 2 * H, LANES), F32),
            pltpu.VMEM((n_ch * 2 * H, LANES), F32),
            pltpu.VMEM((nh, L, LANES), F32),
            pltpu.VMEM((nh, L, LANES), F32),
            pltpu.VMEM((nh, L, LANES), F32),
            pltpu.VMEM((nh, L, dv), F32),
            pltpu.VMEM((nh, dqk, dv + LANES), F32),
        ],
        compiler_params=_params("arbitrary", "arbitrary"),
        name="mlstm",
    )(proj, proj, proj, proj, small_t, bias_r, norm_g)


def _gdn_kernel(x_ref, z_ref, gr_ref, cw_ref, al_ref, dt_ref, ng_ref,
                out_ref, s_ref, xs_ref, c_ref, qn_ref, kn_ref, knt_ref, kk_ref, qk_ref, pw_ref, x_ref_, rhs_ref,
                attn_ref, qd_ref, kdt_ref, gam3_ref, beta3_ref, kdwb3_ref, gt_ref, sol_ref, vn_ref,
                *, n_qk, n_v, ga_off, gb_off, n_ch):
    L = CHUNK
    R = n_ch * L
    dk, dv = GDN_DK, GDN_DV
    qk_w = n_qk * dk
    rep = n_v // n_qk
    conv_ch = 2 * qk_w + n_v * dv

    @pl.when(pl.program_id(1) == 0)
    def _():
        s_ref[...] = jnp.zeros_like(s_ref)
        xs_ref[0:8, :] = jnp.zeros((8, xs_ref.shape[1]), F32)

    xs_ref[8:8 + R, :] = x_ref[...].astype(F32)
    for cb in range(0, conv_ch, GDN_CONV_COLS):
        cs = slice(cb, cb + GDN_CONV_COLS)
        xa = xs_ref[0:8 + R, cs]
        conv = cw_ref[CONV_K - 1:CONV_K, cs] * xa[8:8 + R]
        for sh in range(1, CONV_K):
            conv = conv + cw_ref[CONV_K - 1 - sh:CONV_K - sh, cs] * pltpu.roll(xa, sh, axis=0)[8:8 + R]
        c_ref[:, cs] = conv * _sigmoid(conv)
    xs_ref[0:8, :] = xs_ref[R:R + 8, :]

    ii = lax.broadcasted_iota(jnp.int32, (L, L), 0)
    jj = lax.broadcasted_iota(jnp.int32, (L, L), 1)
    incl = jj <= ii
    strict = jj < ii
    diag = ii == jj
    eye = diag.astype(F32)
    upper = (ii <= jj).astype(BF16)

    for g in range(n_ch):
        rows = slice(g * L, (g + 1) * L)
        for hk in range(n_qk):
            iq = g * n_qk + hk
            cq = c_ref[rows, hk * dk:(hk + 1) * dk]
            ck = c_ref[rows, qk_w + hk * dk:qk_w + (hk + 1) * dk]
            qn = cq * lax.rsqrt(jnp.sum(cq * cq, axis=1, keepdims=True) + RMS_EPS) * (dk ** -0.5)
            kn = ck * lax.rsqrt(jnp.sum(ck * ck, axis=1, keepdims=True) + RMS_EPS)
            qn_ref[iq] = qn
            kn_ref[iq] = kn
            knt_ref[iq] = kn.T
            kb = kn.astype(BF16)
            kk_ref[iq] = _dot_nt(kb, kb)
            qk_ref[iq] = _dot_nt(qn.astype(BF16), kb)

    for g in range(n_ch):
        gd = -jnp.exp(al_ref[...]) * _softplus(gr_ref[g, ga_off:ga_off + n_v, :] + dt_ref[...])
        gam = _cumsum_lanes(gd, upper)
        beta = _sigmoid(gr_ref[g, gb_off:gb_off + n_v, :])
        g_tot = gam[:, L - 1:L]
        kdwb = jnp.exp(g_tot - gam) * beta
        gt_ref[g * n_v:(g + 1) * n_v, :] = jnp.broadcast_to(jnp.exp(g_tot), (n_v, LANES))
        for h in range(n_v):
            gam3_ref[g * n_v + h] = gam[h:h + 1, :]
            beta3_ref[g * n_v + h] = beta[h:h + 1, :]
            kdwb3_ref[g * n_v + h] = kdwb[h:h + 1, :]

    for g in range(n_ch):
        sv = slice(g * n_v, (g + 1) * n_v)
        sq = slice(g * n_qk, (g + 1) * n_qk)
        rows = slice(g * L, (g + 1) * L)
        gam_r = gam3_ref[sv]
        beta_r = beta3_ref[sv]
        gam_c = jnp.sum(jnp.where(diag, gam_r, 0.0), axis=2, keepdims=True)
        decm = jnp.exp(jnp.where(incl, gam_c - gam_r, -jnp.inf))
        db = decm * beta_r
        a = jnp.where(strict, jnp.repeat(kk_ref[sq], rep, axis=0) * db, 0.0)
        pw_ref[sv] = a
        x_ref_[sv] = eye - a
        attn_ref[sv] = (jnp.repeat(qk_ref[sq], rep, axis=0) * db).astype(BF16)
        eg_c = jnp.exp(gam_c)
        rhs_ref[sv, :, dv:dv + dk] = (jnp.repeat(kn_ref[sq], rep, axis=0) * eg_c).astype(BF16)
        qd_ref[sv] = (jnp.repeat(qn_ref[sq], rep, axis=0) * eg_c).astype(BF16)
        kdt_ref[sv] = (jnp.repeat(knt_ref[sq], rep, axis=0) * kdwb3_ref[sv]).astype(BF16)
        for h in range(n_v):
            rhs_ref[g * n_v + h, :, 0:dv] = c_ref[rows, 2 * qk_w + h * dv:2 * qk_w + (h + 1) * dv].astype(BF16)

    n_sq = 5
    for lvl in range(n_sq + 1):
        for i in range(n_ch * n_v):
            pwb = pw_ref[i].astype(BF16)
            if lvl > 0:
                xv = x_ref_[i]
                x_ref_[i] = xv + _dot(xv.astype(BF16), pwb)
            if lvl < n_sq:
                pw_ref[i] = _dot(pwb, pwb)

    for i in range(n_ch * n_v):
        sol_ref[i] = _dot(x_ref_[i].astype(BF16), rhs_ref[i])

    for g in range(n_ch):
        rows = slice(g * L, (g + 1) * L)
        for h in range(n_v):
            sol = sol_ref[g * n_v + h]
            vn_ref[h] = (sol[:, 0:dv] - _dot(sol[:, dv:dv + dk].astype(BF16), s_ref[h].astype(BF16))).astype(BF16)
        for h in range(n_v):
            i = g * n_v + h
            s_st = s_ref[h]
            vnb = vn_ref[h]
            o = _dot(qd_ref[i], s_st.astype(BF16)) + _dot(attn_ref[i], vnb)
            s_ref[h] = gt_ref[i:i + 1, :] * s_st + _dot(kdt_ref[i], vnb)
            rms = lax.rsqrt(jnp.mean(o * o, axis=1, keepdims=True) + RMS_EPS)
            zz = z_ref[rows, h * dv:(h + 1) * dv].astype(F32)
            out_ref[rows, h * dv:(h + 1) * dv] = (o * rms * ng_ref[...] * (zz * _sigmoid(zz))).astype(BF16)


def _gdn(proj, small_t, conv_w, al, dt, norm_g, bsz, seq, cols, n_qk, n_v, ga_off, gb_off, n_ch):
    n = seq // (CHUNK * n_ch)
    L = CHUNK
    R = n_ch * L
    conv_ch = 2 * n_qk * GDN_DK + n_v * GDN_DV
    v_w = n_v * GDN_DV
    t = bsz * seq
    nq, nv = n_ch * n_qk, n_ch * n_v
    kern = functools.partial(_gdn_kernel, n_qk=n_qk, n_v=n_v, ga_off=ga_off, gb_off=gb_off, n_ch=n_ch)
    row = lambda b, c: b * n + c
    full = lambda a: pl.BlockSpec(a.shape, lambda b, c: (0,) * a.ndim)
    return pl.pallas_call(
        kern,
        grid=(bsz, n),
        in_specs=[
            pl.BlockSpec((R, conv_ch), lambda b, c: (row(b, c), cols["gqkv"] // conv_ch)),
            pl.BlockSpec((R, v_w), lambda b, c: (row(b, c), cols["gz"] // v_w)),
            pl.BlockSpec((n_ch, small_t.shape[1], L), lambda b, c: (row(b, c), 0, 0)),
            full(conv_w), full(al), full(dt), full(norm_g),
        ],
        out_specs=pl.BlockSpec((R, v_w), lambda b, c: (row(b, c), 0)),
        out_shape=jax.ShapeDtypeStruct((t, v_w), BF16),
        scratch_shapes=[
            pltpu.VMEM((n_v, GDN_DK, GDN_DV), F32),
            pltpu.VMEM((8 + R + 8, conv_ch), F32),
            pltpu.VMEM((R, conv_ch), F32),
            pltpu.VMEM((nq, L, GDN_DK), F32),
            pltpu.VMEM((nq, L, GDN_DK), F32),
            pltpu.VMEM((nq, GDN_DK, L), F32),
            pltpu.VMEM((nq, L, L), F32),
            pltpu.VMEM((nq, L, L), F32),
            pltpu.VMEM((nv, L, L), F32),
            pltpu.VMEM((nv, L, L), F32),
            pltpu.VMEM((nv, L, GDN_DV + GDN_DK), BF16),
            pltpu.VMEM((nv, L, L), BF16),
            pltpu.VMEM((nv, L, GDN_DK), BF16),
            pltpu.VMEM((nv, GDN_DK, L), BF16),
            pltpu.VMEM((nv, 1, L), F32),
            pltpu.VMEM((nv, 1, L), F32),
            pltpu.VMEM((nv, 1, L), F32),
            pltpu.VMEM((nv, LANES), F32),
            pltpu.VMEM((nv, L, GDN_DV + GDN_DK), F32),
            pltpu.VMEM((n_v, L, GDN_DV), BF16),
        ],
        compiler_params=_params("arbitrary", "arbitrary"),
        name="gdn",
    )(proj, proj, small_t, conv_w, al, dt, norm_g)


def _kv_kernel(m_ref, wk_ref, wv_ref, k_ref, v_ref):
    mb = m_ref[...].astype(BF16)
    k_ref[...] = _dot(mb, wk_ref[...]).astype(BF16)
    v_ref[...] = _dot(mb, wv_ref[...]).astype(BF16)


def _kv_proj(mem2, wk, wv, tm):
    t, d = mem2.shape
    full = lambda a: pl.BlockSpec(a.shape, lambda i: (0,) * a.ndim)
    return pl.pallas_call(
        _kv_kernel,
        grid=(t // tm,),
        in_specs=[pl.BlockSpec((tm, d), lambda i: (i, 0)), full(wk), full(wv)],
        out_specs=[pl.BlockSpec((tm, d), lambda i: (i, 0))] * 2,
        out_shape=[jax.ShapeDtypeStruct((t, d), BF16)] * 2,
        compiler_params=_params("arbitrary"),
        name="mem_kv_proj",
    )(mem2, wk, wv)


def _mix_xattn_kernel(hm_ref, og_ref, gm_ref, gg_ref, h0_ref, wbm_ref, wbg_ref, wmo_ref, g1_ref, b1_ref,
                      k_ref, v_ref, wq_ref, wo_ref, g_ref, b_ref, wr_ref, br_ref,
                      h2_ref, gw_ref, route_ref, cnt_ref, carry_ref, *, dh):
    tm = h0_ref.shape[0]

    @pl.when((pl.program_id(0) == 0) & (pl.program_id(1) == 0))
    def _():
        carry_ref[...] = jnp.zeros_like(carry_ref)

    y_ml = _dot(hm_ref[...], wbm_ref[...])
    y_gdn = _dot(og_ref[...], wbg_ref[...])
    merged = _sigmoid(gm_ref[...].astype(F32)) * y_ml + _sigmoid(gg_ref[...].astype(F32)) * y_gdn
    mix = _dot(merged.astype(BF16), wmo_ref[...])
    h1 = _layer_norm(DN_ALPHA * h0_ref[...] + mix, g1_ref[...], b1_ref[...])

    q = _dot(h1.astype(BF16), wq_ref[...])
    outs = []
    for hd in range(XA_HEADS):
        qh = q[:, hd * dh:(hd + 1) * dh].astype(BF16)
        kh = k_ref[:, hd * dh:(hd + 1) * dh]
        vh = v_ref[:, hd * dh:(hd + 1) * dh]
        sc = _dot_nt(qh, kh) * (dh ** -0.5)
        e = jnp.exp(sc - jnp.max(sc, axis=1, keepdims=True))
        p = e / jnp.sum(e, axis=1, keepdims=True)
        outs.append(_dot(p.astype(BF16), vh))
    o = jnp.concatenate(outs, axis=1)
    xa = _dot(o.astype(BF16), wo_ref[...])
    h2 = _layer_norm(DN_ALPHA * h1 + xa, g_ref[...], b_ref[...])
    h2_ref[...] = h2

    logits = _dot(h2.astype(BF16), wr_ref[...]) + br_ref[...]
    lane = lax.broadcasted_iota(jnp.int32, (tm, LANES), 1)
    lane_f = lane.astype(F32)
    work = logits
    vals, idxs = [], []
    for _ in range(TOP_K):
        m = jnp.max(work, axis=1, keepdims=True)
        idx = jnp.min(jnp.where(work == m, lane_f, float(LANES)), axis=1, keepdims=True)
        vals.append(m)
        idxs.append(idx)
        work = jnp.where(lane_f == idx, -jnp.inf, work)
    es = [jnp.exp(v - vals[0]) for v in vals]
    tot = es[0]
    for e_ in es[1:]:
        tot = tot + e_
    onehot = jnp.zeros((tm, LANES), F32)
    for idx in idxs:
        onehot = onehot + (lane_f == idx).astype(F32)
    ri = lax.broadcasted_iota(jnp.int32, (tm, tm), 0)
    ci = lax.broadcasted_iota(jnp.int32, (tm, tm), 1)
    tri = (ci < ri).astype(BF16)
    carry = carry_ref[0:1, :]
    ranks = carry + _dot(tri, onehot.astype(BF16))
    gw = jnp.zeros((tm, LANES), F32)
    route = jnp.zeros((tm, LANES), F32)
    for k_ in range(TOP_K):
        rk = jnp.sum(jnp.where(lane_f == idxs[k_], ranks, 0.0), axis=1, keepdims=True)
        gw = gw + jnp.where(lane == k_, es[k_] / tot, 0.0)
        route = route + jnp.where(lane == k_, idxs[k_], 0.0) + jnp.where(lane == TOP_K + k_, rk, 0.0)
    gw_ref[...] = gw
    route_ref[...] = route.astype(jnp.int32)
    carry = carry + jnp.sum(onehot, axis=0, keepdims=True)
    carry_ref[...] = jnp.broadcast_to(carry, carry_ref.shape)
    cnt_ref[...] = jnp.broadcast_to(carry, cnt_ref.shape)


def _mix_xattn(hm, og, proj, h0, w_bm, w_bg, w_mo, g1, b1, kmem, vmem, wq, wo, g, b, w_r, b_r, cols,
               bsz, seq, mem_len, tm):
    t, d = h0.shape
    nt = seq // tm
    once = lambda a: pl.BlockSpec(a.shape, lambda i, j: (0,) * a.ndim, pipeline_mode=pl.Buffered(1))
    rowmap = lambda i, j: (i * nt + j, 0)
    kern = functools.partial(_mix_xattn_kernel, dh=d // XA_HEADS)
    return pl.pallas_call(
        kern,
        grid=(bsz, nt),
        in_specs=[
            pl.BlockSpec((tm, hm.shape[1]), rowmap),
            pl.BlockSpec((tm, og.shape[1]), rowmap),
            pl.BlockSpec((tm, d), lambda i, j: (i * nt + j, cols["gate_ml"] // d)),
            pl.BlockSpec((tm, d), lambda i, j: (i * nt + j, cols["gate_gdn"] // d)),
            pl.BlockSpec((tm, d), rowmap),
            once(w_bm), once(w_bg), once(w_mo), once(g1), once(b1),
            pl.BlockSpec((mem_len, d), lambda i, j: (i, 0)),
            pl.BlockSpec((mem_len, d), lambda i, j: (i, 0)),
            once(wq), once(wo), once(g), once(b), once(w_r), once(b_r),
        ],
        out_specs=[
            pl.BlockSpec((tm, d), rowmap),
            pl.BlockSpec((tm, LANES), rowmap),
            pl.BlockSpec((tm, LANES), rowmap),
            pl.BlockSpec((8, LANES), lambda i, j: (0, 0)),
        ],
        out_shape=[
            jax.ShapeDtypeStruct((t, d), F32),
            jax.ShapeDtypeStruct((t, LANES), F32),
            jax.ShapeDtypeStruct((t, LANES), jnp.int32),
            jax.ShapeDtypeStruct((8, LANES), F32),
        ],
        scratch_shapes=[pltpu.VMEM((8, LANES), F32)],
        compiler_params=_params("arbitrary", "arbitrary"),
        name="mix_xattn_ln_router",
    )(hm, og, proj, proj, h0, w_bm, w_bg, w_mo, g1, b1, kmem, vmem, wq, wo, g, b, w_r, b_r)


def _dispatch_kernel(pad_lo_ref, pad_n_ref, na_ref, dest_ref, h_ref, xs_ref, zero_ref, sem, zsem):
    tm = h_ref.shape[0]
    step = pl.program_id(0)
    n_blocks = xs_ref.shape[0] // MOE_BLOCK

    @pl.when(step == 0)
    def _():
        zero_ref[...] = jnp.zeros_like(zero_ref)

        def per_expert(e, carry):
            n = pad_n_ref[e]
            pos = pad_lo_ref[e] + n

            def pieces(wait):
                def piece(start, p):
                    cp = pltpu.make_async_copy(zero_ref.at[pl.ds(0, p), :], xs_ref.at[pl.ds(start, p), :], zsem)
                    cp.wait() if wait else cp.start()

                at = pos
                p = MOE_BLOCK // 2
                while p >= SUBLANES:

                    @pl.when((n & p) != 0)
                    def _(p=p, at=at):
                        piece(pl.multiple_of(at - p, SUBLANES), p)

                    at = at - (n & p)
                    p //= 2
                for r in range(SUBLANES - 1):

                    @pl.when(r < (n & (SUBLANES - 1)))
                    def _(r=r, at=at):
                        piece(at - 1 - r, 1)

            pieces(wait=False)
            pieces(wait=True)
            return carry

        lax.fori_loop(0, N_EXPERTS, per_expert, 0)

        def tail(b, carry):
            half = MOE_BLOCK // 2
            cp = pltpu.make_async_copy(zero_ref, xs_ref.at[pl.ds(b * half, half), :], zsem)
            cp.start()
            cp.wait()
            return carry

        lax.fori_loop(2 * na_ref[0], 2 * n_blocks, tail, 0)

    for tok in range(tm):
        src = h_ref.at[pl.ds(tok, 1), :]
        for k_ in range(TOP_K):
            pltpu.make_async_copy(src, xs_ref.at[pl.ds(dest_ref[tok * TOP_K + k_], 1), :], sem).start(priority=k_ % 2)
    for _ in range(TOP_K):
        pltpu.make_async_copy(h_ref, xs_ref.at[pl.ds(0, tm), :], sem).wait()


def _dispatch(pad_lo, pad_n, nact, dest_flat, h2, n_rows, tm):
    t, d = h2.shape
    grid_spec = pltpu.PrefetchScalarGridSpec(
        num_scalar_prefetch=3,
        grid=(t // tm,),
        in_specs=[
            pl.BlockSpec((tm * TOP_K,), lambda i, *_: (i,), memory_space=pltpu.SMEM),
            pl.BlockSpec((tm, d), lambda i, *_: (i, 0)),
        ],
        out_specs=pl.BlockSpec(memory_space=pl.ANY),
        scratch_shapes=[pltpu.VMEM((MOE_BLOCK // 2, d), F32), pltpu.SemaphoreType.DMA(()),
                        pltpu.SemaphoreType.DMA(())],
    )
    return pl.pallas_call(
        _dispatch_kernel,
        grid_spec=grid_spec,
        out_shape=jax.ShapeDtypeStruct((n_rows, d), F32),
        compiler_params=_params("arbitrary"),
        name="moe_dispatch",
    )(pad_lo, pad_n, nact, dest_flat, h2)


def _expert_kernel(be_ref, na_ref, x_ref, wgu_ref, bgu_ref, wdn_ref, bdn_ref, y_ref, wgu_s, wdn_s, *, d_exp):
    blk = pl.program_id(0)
    active = blk < na_ref[0]

    @pl.when(jnp.logical_not(active))
    def _():
        y_ref[...] = jnp.zeros_like(y_ref)

    @pl.when(active & ((blk == 0) | (be_ref[blk] != be_ref[jnp.maximum(blk - 1, 0)])))
    def _():
        wgu_s[...] = wgu_ref[0].astype(BF16)
        wdn_s[...] = wdn_ref[0].astype(BF16)

    @pl.when(active)
    def _():
        gu = _dot(x_ref[...].astype(BF16), wgu_s[...]) + bgu_ref[0]
        gate = jnp.minimum(gu[:, :d_exp], SWIGLU_LIMIT)
        up = jnp.clip(gu[:, d_exp:], -SWIGLU_LIMIT, SWIGLU_LIMIT)
        act = (up + 1.0) * (gate * _sigmoid(SWIGLU_ALPHA * gate))
        y_ref[...] = _dot(act.astype(BF16), wdn_s[...]) + bdn_ref[0]


def _experts(block_e, nact, xs, w_gu, b_gu, w_dn, b_dn):
    n_rows, d = xs.shape
    n_blocks = n_rows // MOE_BLOCK
    d_exp = w_dn.shape[1]
    kern = functools.partial(_expert_kernel, d_exp=d_exp)
    emap = lambda i, be, na: (be[i], 0, 0)
    grid_spec = pltpu.PrefetchScalarGridSpec(
        num_scalar_prefetch=2,
        grid=(n_blocks,),
        in_specs=[
            pl.BlockSpec((MOE_BLOCK, d), lambda i, be, na: (i, 0)),
            pl.BlockSpec((1, d, 2 * d_exp), emap),
            pl.BlockSpec((1, 1, 2 * d_exp), emap),
            pl.BlockSpec((1, d_exp, d), emap),
            pl.BlockSpec((1, 1, d), emap),
        ],
        out_specs=pl.BlockSpec((MOE_BLOCK, d), lambda i, be, na: (i, 0)),
        scratch_shapes=[pltpu.VMEM((d, 2 * d_exp), BF16), pltpu.VMEM((d_exp, d), BF16)],
    )
    return pl.pallas_call(
        kern,
        grid_spec=grid_spec,
        out_shape=jax.ShapeDtypeStruct((n_rows, d), F32),
        compiler_params=_params("arbitrary"),
        name="moe_experts",
    )(block_e, nact, xs, w_gu, b_gu, w_dn, b_dn)


def _combine_kernel(dest_ref, dest_nxt_ref, gw_ref, h_ref, g_ref, b_ref, y_ref, out_ref, buf_a, buf_b, sem):
    tm = buf_a.shape[1]
    step = pl.program_id(0)
    n_steps = pl.num_programs(0)

    def issue(idx_ref, half, buf, s):
        for tok in range(tm):
            for k_ in range(TOP_K):
                pltpu.make_async_copy(y_ref.at[pl.ds(idx_ref[(half * tm + tok) * TOP_K + k_], 1), :],
                                      buf.at[k_, pl.ds(tok, 1), :], sem.at[s]).start(priority=k_ % 2)

    def consume(half, buf, s):
        for k_ in range(TOP_K):
            pltpu.make_async_copy(y_ref.at[pl.ds(0, tm), :], buf.at[k_], sem.at[s]).wait()
        rows = slice(half * tm, (half + 1) * tm)
        gw = gw_ref[rows, :]
        ff = buf[0] * gw[:, 0:1]
        for k_ in range(1, TOP_K):
            ff = ff + buf[k_] * gw[:, k_:k_ + 1]
        out_ref[rows, :] = _layer_norm(DN_ALPHA * h_ref[rows, :] + ff, g_ref[...], b_ref[...])

    @pl.when(step == 0)
    def _():
        issue(dest_ref, 0, buf_a, 0)

    issue(dest_ref, 1, buf_b, 1)
    consume(0, buf_a, 0)

    @pl.when(step + 1 < n_steps)
    def _():
        issue(dest_nxt_ref, 0, buf_a, 0)

    consume(1, buf_b, 1)


def _combine(dest_flat, gw, h2, g, b, yb, tm):
    t, d = h2.shape
    n = t // (2 * tm)
    return pl.pallas_call(
        _combine_kernel,
        grid=(n,),
        in_specs=[
            pl.BlockSpec((2 * tm * TOP_K,), lambda i: (i,), memory_space=pltpu.SMEM),
            pl.BlockSpec((2 * tm * TOP_K,), lambda i: (jnp.minimum(i + 1, n - 1),), memory_space=pltpu.SMEM),
            pl.BlockSpec((2 * tm, LANES), lambda i: (i, 0)),
            pl.BlockSpec((2 * tm, d), lambda i: (i, 0)),
            pl.BlockSpec((1, d), lambda i: (0, 0)),
            pl.BlockSpec((1, d), lambda i: (0, 0)),
            pl.BlockSpec(memory_space=pl.ANY),
        ],
        out_specs=pl.BlockSpec((2 * tm, d), lambda i: (i, 0)),
        out_shape=jax.ShapeDtypeStruct((t, d), F32),
        scratch_shapes=[pltpu.VMEM((TOP_K, tm, d), F32), pltpu.VMEM((TOP_K, tm, d), F32),
                        pltpu.SemaphoreType.DMA((2,))],
        compiler_params=_params("arbitrary"),
        name="moe_combine_ln3",
    )(dest_flat, dest_flat, gw, h2, g, b, yb)


def _pick(n, pref):
    return pref if n % pref == 0 else n


def kernel(x, mem, ln_in_g, ln_in_b, w_in, ml_gate_bias, ml_norm_g, gdn_conv_w, gdn_a_log, gdn_dt_bias, gdn_norm_g, w_branch_ml, w_branch_gdn, w_mix_out, ln1_g, ln1_b, xa_wq, xa_wk, xa_wv, xa_wo, ln2_g, ln2_b, w_router, b_router, w_gu, b_gu, w_dn, b_dn, ln3_g, ln3_b):
    bsz, seq, d = x.shape
    mem_len = mem.shape[1]
    t = bsz * seq
    ml_dv = d // ML_HEADS
    ml_dqk = ml_dv // 2
    ml_qk_w, ml_v_w = ML_HEADS * ml_dqk, ML_HEADS * ml_dv
    n_qk = d // GDN_DK
    n_v = 2 * n_qk
    gdn_qk_w, gdn_v_w = n_qk * GDN_DK, n_v * GDN_DV
    conv_ch = 2 * gdn_qk_w + gdn_v_w
    splits = (ml_qk_w, ml_qk_w, ml_v_w, ml_v_w, 2 * ML_HEADS, conv_ch, gdn_v_w, n_v, n_v, d, d)
    names = ("mq", "mk", "mv", "mo", "mif", "gqkv", "gz", "ga", "gb", "gate_ml", "gate_gdn")
    starts = {}
    acc = 0
    for nm, sz in zip(names, splits):
        starts[nm] = (acc, sz)
        acc += sz
    row2 = lambda a: a.reshape(1, -1).astype(F32)

    h = x.reshape(t, d)
    for l in range(DEPTH):
        w = w_in[l]
        seg = lambda nm: w[:, starts[nm][0]:starts[nm][0] + starts[nm][1]]
        order = ("gqkv", "gz", "mv", "mo", "gate_ml", "gate_gdn", "mq", "mk")
        cols = {}
        off = 0
        for nm in order:
            assert off % starts[nm][1] == 0
            cols[nm] = off
            off += starts[nm][1]
        w_big = jnp.concatenate([seg(nm) for nm in order], axis=1).astype(BF16)
        n_small = 2 * ML_HEADS + 2 * n_v
        w_small = jnp.concatenate([seg("mif"), seg("ga"), seg("gb"),
                                   jnp.zeros((d, LANES - n_small), F32)], axis=1).astype(BF16)
        ga_off, gb_off = 2 * ML_HEADS, 2 * ML_HEADS + n_v

        if l == 0:
            h0, proj, small_t = _ln_proj(h, row2(ln_in_g), row2(ln_in_b), w_big, w_small,
                                         _pick(t, 1024), _pick(off, 2816))
        else:
            raise NotImplementedError("DEPTH > 1")

        bias = ml_gate_bias[l].astype(F32)
        hm = _mlstm(proj, small_t, bias.reshape(-1, 1), row2(ml_norm_g[l]),
                    bsz, seq, cols, ml_dqk, ml_dv, ML_CHUNKS_PER_STEP)
        al = gdn_a_log[l].astype(F32)
        dt = gdn_dt_bias[l].astype(F32)
        og = _gdn(proj, small_t, gdn_conv_w[l].astype(F32), al.reshape(-1, 1), dt.reshape(-1, 1),
                  row2(gdn_norm_g[l]), bsz, seq, cols, n_qk, n_v, ga_off, gb_off, GDN_CHUNKS_PER_STEP)
        kmem, vmem = _kv_proj(mem.reshape(bsz * mem_len, d), xa_wk[l].astype(BF16), xa_wv[l].astype(BF16),
                              _pick(bsz * mem_len, 512))
        w_r = jnp.concatenate([w_router[l], jnp.zeros((d, LANES - N_EXPERTS), F32)], axis=1).astype(BF16)
        b_r = jnp.concatenate([b_router[l].astype(F32), jnp.full((LANES - N_EXPERTS,), NEG_BIG, F32)]).reshape(1, -1)
        h2, gw, route, cnt = _mix_xattn(hm, og, proj, h0, w_branch_ml[l].astype(BF16), w_branch_gdn[l].astype(BF16),
                                        w_mix_out[l].astype(BF16), row2(ln1_g[l]), row2(ln1_b[l]), kmem, vmem,
                                        xa_wq[l].astype(BF16), xa_wo[l].astype(BF16), row2(ln2_g[l]), row2(ln2_b[l]),
                                        w_r, b_r, cols, bsz, seq, mem_len, _pick(seq, 512))

        counts = cnt[0, :N_EXPERTS].astype(jnp.int32)
        padded = (counts + MOE_BLOCK - 1) // MOE_BLOCK * MOE_BLOCK
        pad_end = jnp.cumsum(padded)
        pad_start = pad_end - padded
        n_asg = t * TOP_K
        n_blocks = -(-n_asg // MOE_BLOCK) + N_EXPERTS
        n_rows = n_blocks * MOE_BLOCK
        top_e = route[:, :TOP_K]
        rank = route[:, TOP_K:2 * TOP_K]
        sel = top_e[:, :, None] == jnp.arange(N_EXPERTS, dtype=jnp.int32)[None, None, :]
        dest = (jnp.sum(jnp.where(sel, pad_start[None, None, :], 0), axis=-1) + rank).astype(jnp.int32)
        dest_flat = dest.reshape(n_asg)
        blk_row = jnp.arange(n_blocks, dtype=jnp.int32) * MOE_BLOCK
        block_e = jnp.minimum(jnp.sum(pad_end[None, :] <= blk_row[:, None], axis=1), N_EXPERTS - 1).astype(jnp.int32)
        nact = (pad_end[-1:] // MOE_BLOCK).astype(jnp.int32)
        pad_lo = (pad_start + counts).astype(jnp.int32)
        pad_n = (padded - counts).astype(jnp.int32)

        tm_d = _pick(t, 256)
        xs = _dispatch(pad_lo, pad_n, nact, dest_flat, h2, n_rows, _pick(t, 512))
        yb = _experts(block_e, nact, xs, w_gu[l], b_gu[l].astype(F32)[:, None, :],
                      w_dn[l], b_dn[l].astype(F32)[:, None, :])
        h = _combine(dest_flat, gw, h2, row2(ln3_g[l]), row2(ln3_b[l]), yb, tm_d)
    return h.reshape(bsz, seq, d)
```

```python
import functools

import jax
import jax.numpy as jnp
from jax import lax
from jax.experimental import pallas as pl
from jax.experimental.pallas import tpu as pltpu

F32 = jnp.float32
BF16 = jnp.bfloat16

CHUNK = 64
ML_HEADS = 4
GDN_DK = 128
GDN_DV = 128
CONV_K = 4
XA_HEADS = 4
N_EXPERTS = 32
TOP_K = 4
SWIGLU_LIMIT = 7.0
SWIGLU_ALPHA = 1.702
MOE_BLOCK = 512
DEPTH = 1
DN_ALPHA = (2 * DEPTH) ** 0.25
LN_EPS = 1e-5
RMS_EPS = 1e-6
LANES = 128
SUBLANES = 8
VMEM_LIMIT = 56 * 1024 * 1024
NEG_BIG = -1e30
ML_CHUNKS_PER_STEP = 8
GDN_CHUNKS_PER_STEP = 4
GDN_CONV_COLS = 512


def _params(*sem):
    return pltpu.CompilerParams(dimension_semantics=sem, vmem_limit_bytes=VMEM_LIMIT)


def _dot(a, b):
    return jnp.dot(a, b, preferred_element_type=F32)


def _dot_nt(a, b):
    return lax.dot_general(a, b, (((1,), (1,)), ((), ())), preferred_element_type=F32)


def _layer_norm(x, g, b):
    mu = jnp.mean(x, axis=-1, keepdims=True)
    xc = x - mu
    var = jnp.mean(xc * xc, axis=-1, keepdims=True)
    return xc * lax.rsqrt(var + LN_EPS) * g + b


def _sigmoid(x):
    return 1.0 / (1.0 + jnp.exp(-x))


def _log_sigmoid(x):
    return jnp.minimum(x, 0.0) - jnp.log(1.0 + jnp.exp(-jnp.abs(x)))


def _softplus(x):
    return jnp.maximum(x, 0.0) + jnp.log(1.0 + jnp.exp(-jnp.abs(x)))


def _split3(x):
    hi = x.astype(BF16)
    r = x - hi.astype(F32)
    mid = r.astype(BF16)
    lo = (r - mid.astype(F32)).astype(BF16)
    return hi, mid, lo


def _cumsum_lanes(x, upper):
    hi, mid, lo = _split3(x)
    return _dot(hi, upper) + _dot(mid, upper) + _dot(lo, upper)


def _ln_proj_kernel(x_ref, g_ref, b_ref, w_ref, ws_ref, h_ref, p_ref, s_ref, xn_ref):
    @pl.when(pl.program_id(1) == 0)
    def _():
        h = _layer_norm(x_ref[...], g_ref[...], b_ref[...])
        h_ref[...] = h
        hb = h.astype(BF16)
        xn_ref[...] = hb
        s = _dot(hb, ws_ref[...])
        for c in range(s_ref.shape[0]):
            s_ref[c] = s[c * CHUNK:(c + 1) * CHUNK, :].T

    p_ref[...] = _dot(xn_ref[...], w_ref[...]).astype(BF16)


def _ln_proj(x2, g, b, w_big, w_small, tm, tn):
    t, d = x2.shape
    nw = w_big.shape[1]
    return pl.pallas_call(
        _ln_proj_kernel,
        grid=(t // tm, nw // tn),
        in_specs=[
            pl.BlockSpec((tm, d), lambda i, j: (i, 0)),
            pl.BlockSpec((1, d), lambda i, j: (0, 0)),
            pl.BlockSpec((1, d), lambda i, j: (0, 0)),
            pl.BlockSpec((d, tn), lambda i, j: (0, j)),
            pl.BlockSpec((d, LANES), lambda i, j: (0, 0)),
        ],
        out_specs=[
            pl.BlockSpec((tm, d), lambda i, j: (i, 0)),
            pl.BlockSpec((tm, tn), lambda i, j: (i, j)),
            pl.BlockSpec((tm // CHUNK, LANES, CHUNK), lambda i, j: (i, 0, 0)),
        ],
        out_shape=[
            jax.ShapeDtypeStruct((t, d), F32),
            jax.ShapeDtypeStruct((t, nw), BF16),
            jax.ShapeDtypeStruct((t // CHUNK, LANES, CHUNK), F32),
        ],
        scratch_shapes=[pltpu.VMEM((tm, d), BF16)],
        compiler_params=_params("arbitrary", "arbitrary"),
        name="ln_in_proj",
    )(x2, g, b, w_big, w_small)


def _mlstm_kernel(q_ref, k_ref, v_ref, o_ref, gr_ref, br_ref, ng_ref,
                  out_ref, c_ref, m_ref, sqk_ref, kt_ref, p_ref, ktw_ref, b3_ref, li3_ref, kw3_ref,
                  gs_ref, mc_ref, stb_ref, stm_ref, std_ref, num_ref, kv_ref, *, dqk, dv, n_ch):
    L = CHUNK
    H = ML_HEADS

    @pl.when(pl.program_id(1) == 0)
    def _():
        c_ref[...] = jnp.zeros_like(c_ref)
        m_ref[...] = jnp.zeros_like(m_ref)

    ii = lax.broadcasted_iota(jnp.int32, (L, L), 0)
    jj = lax.broadcasted_iota(jnp.int32, (L, L), 1)
    causal = jj <= ii
    diag = ii == jj
    upper = (ii <= jj).astype(BF16)
    scale = dqk ** -0.5
    heads = [(g, h) for g in range(n_ch) for h in range(H)]

    for g, h in heads:
        rows = slice(g * L, (g + 1) * L)
        kf = k_ref[rows, h * dqk:(h + 1) * dqk].astype(F32) * scale
        sqk_ref[g * H + h] = _dot_nt(q_ref[rows, h * dqk:(h + 1) * dqk], kf.astype(BF16))
        kt_ref[g * H + h] = kf.T

    pre = gr_ref[:, 0:2 * H, :].reshape(n_ch * 2 * H, L) + jnp.concatenate([br_ref[...]] * n_ch, axis=0)
    b_all = _cumsum_lanes(_log_sigmoid(pre), upper)
    gs_all = b_all[:, L - 1:L]
    a_all = gs_all - b_all + pltpu.roll(pre, H, axis=0)
    mc_all = jnp.max(a_all, axis=1, keepdims=True)
    kw_all = jnp.exp(a_all - mc_all)
    gs_ref[...] = jnp.broadcast_to(gs_all, gs_ref.shape)
    mc_ref[...] = jnp.broadcast_to(mc_all, mc_ref.shape)
    for g, h in heads:
        r = g * 2 * H + H + h
        b3_ref[g * H + h] = b_all[r:r + 1, :]
        li3_ref[g * H + h] = pre[r - H:r - H + 1, :]
        kw3_ref[g * H + h] = kw_all[r:r + 1, :]

    b_r = b3_ref[...]
    b_c = jnp.sum(jnp.where(diag, b_r, 0.0), axis=2, keepdims=True)
    dmat = jnp.where(causal, b_c - b_r + li3_ref[...], -jnp.inf)
    m_intra = jnp.max(dmat, axis=2, keepdims=True)
    p = jnp.exp(dmat - m_intra) * sqk_ref[...]
    p_ref[...] = p.astype(BF16)
    stb_ref[...] = jnp.broadcast_to(b_c, stb_ref.shape)
    stm_ref[...] = jnp.broadcast_to(m_intra, stm_ref.shape)
    std_ref[...] = jnp.broadcast_to(jnp.sum(p, axis=2, keepdims=True), std_ref.shape)
    ktw_ref[...] = (kt_ref[...] * kw3_ref[...]).astype(BF16)

    ones = jnp.ones((L, LANES), BF16)
    for g, h in heads:
        vh = v_ref[g * L:(g + 1) * L, h * dv:(h + 1) * dv]
        num_ref[g * H + h] = _dot(p_ref[g * H + h], vh)
        kv_ref[g * H + h] = _dot(ktw_ref[g * H + h], jnp.concatenate([vh, ones], axis=1))

    rep = dv // LANES
    wide = lambda s: jnp.concatenate([s] * rep, axis=1)
    for g, h in heads:
        i = g * H + h
        r = g * 2 * H + H + h
        rows = slice(g * L, (g + 1) * L)
        c_st = c_ref[h]
        m_st = m_ref[h:h + 1, :]
        qc = _dot(q_ref[rows, h * dqk:(h + 1) * dqk], c_st.astype(BF16))
        m_intra = stm_ref[i]
        inter_log = stb_ref[i] + m_st
        m_out = jnp.maximum(inter_log, m_intra)
        s_inter = jnp.exp(inter_log - m_out)
        s_intra = jnp.exp(m_intra - m_out)
        num = wide(s_inter) * qc[:, 0:dv] + wide(s_intra) * num_ref[i]
        den = s_inter * qc[:, dv:dv + LANES] + s_intra * std_ref[i]
        hh = num / wide(jnp.maximum(jnp.abs(den), jnp.exp(-m_out)))
        gs = gs_ref[r:r + 1, :]
        mc = mc_ref[r:r + 1, :]
        m_new = jnp.maximum(gs + m_st, mc)
        dec = jnp.exp(gs + m_st - m_new)
        s_new = jnp.exp(mc - m_new)
        c_ref[h] = (jnp.concatenate([dec] * (rep + 1), axis=1) * c_st
                    + jnp.concatenate([s_new] * (rep + 1), axis=1) * kv_ref[i])
        m_ref[h:h + 1, :] = m_new

        rms = lax.rsqrt(jnp.mean(hh * hh, axis=1, keepdims=True) + RMS_EPS)
        og = _sigmoid(o_ref[rows, h * dv:(h + 1) * dv].astype(F32))
        out_ref[rows, h * dv:(h + 1) * dv] = (hh * rms * ng_ref[:, h * dv:(h + 1) * dv] * og).astype(BF16)


def _mlstm(proj, small_t, bias_r, norm_g, bsz, seq, cols, dqk, dv, n_ch):
    n = seq // (CHUNK * n_ch)
    L = CHUNK
    R = n_ch * L
    H = ML_HEADS
    qk_w, v_w = H * dqk, H * dv
    t = bsz * seq
    nh = n_ch * H
    kern = functools.partial(_mlstm_kernel, dqk=dqk, dv=dv, n_ch=n_ch)
    row = lambda b, c: b * n + c
    return pl.pallas_call(
        kern,
        grid=(bsz, n),
        in_specs=[
            pl.BlockSpec((R, qk_w), lambda b, c: (row(b, c), cols["mq"] // qk_w)),
            pl.BlockSpec((R, qk_w), lambda b, c: (row(b, c), cols["mk"] // qk_w)),
            pl.BlockSpec((R, v_w), lambda b, c: (row(b, c), cols["mv"] // v_w)),
            pl.BlockSpec((R, v_w), lambda b, c: (row(b, c), cols["mo"] // v_w)),
            pl.BlockSpec((n_ch, small_t.shape[1], L), lambda b, c: (row(b, c), 0, 0)),
            pl.BlockSpec(bias_r.shape, lambda b, c: (0, 0)),
            pl.BlockSpec((1, v_w), lambda b, c: (0, 0)),
        ],
        out_specs=pl.BlockSpec((R, v_w), lambda b, c: (row(b, c), 0)),
        out_shape=jax.ShapeDtypeStruct((t, v_w), BF16),
        scratch_shapes=[
            pltpu.VMEM((H, dqk, dv + LANES), F32),
            pltpu.VMEM((8, LANES), F32),
            pltpu.VMEM((nh, L, L), F32),
            pltpu.VMEM((nh, dqk, L), F32),
            pltpu.VMEM((nh, L, L), BF16),
            pltpu.VMEM((nh, dqk, L), BF16),
            pltpu.VMEM((nh, 1, L), F32),
            pltpu.VMEM((nh, 1, L), F32),
            pltpu.VMEM((nh, 1, L), F32),
            pltpu.VMEM((n_ch * 2 * H, LANES), F32),
            pltpu.VMEM((n_ch * 2 * H, LANES), F32),
            pltpu.VMEM((nh, L, LANES), F32),
            pltpu.VMEM((nh, L, LANES), F32),
            pltpu.VMEM((nh, L, LANES), F32),
            pltpu.VMEM((nh, L, dv), F32),
            pltpu.VMEM((nh, dqk, dv + LANES), F32),
        ],
        compiler_params=_params("arbitrary", "arbitrary"),
        name="mlstm",
    )(proj, proj, proj, proj, small_t, bias_r, norm_g)


def _gdn_kernel(x_ref, z_ref, gr_ref, cw_ref, al_ref, dt_ref, ng_ref,
                out_ref, s_ref, xs_ref, c_ref, qn_ref, kn_ref, knt_ref, kk2_ref, qk2_ref, pw2_ref, x2_ref, rhs2_ref,
                attn2_ref, qd_ref, kdt_ref, gam2_ref, beta2_ref, kdwb_ref, gt_ref, sol_ref, vn_ref,
                *, n_qk, n_v, ga_off, gb_off, n_ch):
    L = CHUNK
    R = n_ch * L
    dk, dv = GDN_DK, GDN_DV
    qk_w = n_qk * dk
    rep = n_v // n_qk
    assert rep == 2 and 2 * L == LANES and dk == LANES and dv == LANES
    conv_ch = 2 * qk_w + n_v * dv
    sw = dv + dk

    @pl.when(pl.program_id(1) == 0)
    def _():
        s_ref[...] = jnp.zeros_like(s_ref)
        xs_ref[0:8, :] = jnp.zeros((8, xs_ref.shape[1]), F32)
        rhs2_ref[...] = jnp.zeros_like(rhs2_ref)

    xs_ref[8:8 + R, :] = x_ref[...].astype(F32)
    for cb in range(0, conv_ch, GDN_CONV_COLS):
        cs = slice(cb, cb + GDN_CONV_COLS)
        xa = xs_ref[0:8 + R, cs]
        conv = cw_ref[CONV_K - 1:CONV_K, cs] * xa[8:8 + R]
        for sh in range(1, CONV_K):
            conv = conv + cw_ref[CONV_K - 1 - sh:CONV_K - sh, cs] * pltpu.roll(xa, sh, axis=0)[8:8 + R]
        c_ref[:, cs] = conv * _sigmoid(conv)
    xs_ref[0:8, :] = xs_ref[R:R + 8, :]

    ii = lax.broadcasted_iota(jnp.int32, (L, L), 0)
    jj = lax.broadcasted_iota(jnp.int32, (L, L), 1)
    upper = (ii <= jj).astype(BF16)
    row2 = lax.broadcasted_iota(jnp.int32, (L, LANES), 0)
    lane2 = lax.broadcasted_iota(jnp.int32, (L, LANES), 1)
    left = lane2 < L
    col2 = jnp.where(left, lane2, lane2 - L)
    incl2 = col2 <= row2
    strict2 = col2 < row2
    diag2 = col2 == row2
    eye2 = diag2.astype(F32)

    for g in range(n_ch):
        rows = slice(g * L, (g + 1) * L)
        for hk in range(n_qk):
            iq = g * n_qk + hk
            cq = c_ref[rows, hk * dk:(hk + 1) * dk]
            ck = c_ref[rows, qk_w + hk * dk:qk_w + (hk + 1) * dk]
            qn = cq * lax.rsqrt(jnp.sum(cq * cq, axis=1, keepdims=True) + RMS_EPS) * (dk ** -0.5)
            kn = ck * lax.rsqrt(jnp.sum(ck * ck, axis=1, keepdims=True) + RMS_EPS)
            qn_ref[iq] = qn
            kn_ref[iq] = kn
            knt_ref[iq] = kn.T
            kb = kn.astype(BF16)
            kb2 = jnp.concatenate([kb, kb], axis=0)
            kk2_ref[iq] = _dot_nt(kb, kb2)
            qk2_ref[iq] = _dot_nt(qn.astype(BF16), kb2)

    for g in range(n_ch):
        gd = -jnp.exp(al_ref[...]) * _softplus(gr_ref[g, ga_off:ga_off + n_v, :] + dt_ref[...])
        gam = _cumsum_lanes(gd, upper)
        beta = _sigmoid(gr_ref[g, gb_off:gb_off + n_v, :])
        g_tot = gam[:, L - 1:L]
        kdwb = jnp.exp(g_tot - gam) * beta
        gt_ref[g * n_v:(g + 1) * n_v, :] = jnp.broadcast_to(jnp.exp(g_tot), (n_v, LANES))
        for hk in range(n_qk):
            iq = g * n_qk + hk
            gam2_ref[iq] = jnp.concatenate([gam[2 * hk:2 * hk + 1, :], gam[2 * hk + 1:2 * hk + 2, :]], axis=1)
            beta2_ref[iq] = jnp.concatenate([beta[2 * hk:2 * hk + 1, :], beta[2 * hk + 1:2 * hk + 2, :]], axis=1)
            for r in range(rep):
                kdwb_ref[iq, r] = kdwb[2 * hk + r:2 * hk + r + 1, :]

    for g in range(n_ch):
        sq = slice(g * n_qk, (g + 1) * n_qk)
        rows = slice(g * L, (g + 1) * L)
        gam_r = gam2_ref[sq]
        picked = jnp.where(diag2, gam_r, 0.0)
        gam_c0 = jnp.sum(jnp.where(left, picked, 0.0), axis=2, keepdims=True)
        gam_c1 = jnp.sum(jnp.where(left, 0.0, picked), axis=2, keepdims=True)
        gam_c = jnp.where(left, gam_c0, gam_c1)
        decm = jnp.exp(jnp.where(incl2, gam_c - gam_r, -jnp.inf))
        db = decm * beta2_ref[sq]
        a = jnp.where(strict2, kk2_ref[sq] * db, 0.0)
        x2_ref[sq] = eye2 - a
        pw2_ref[sq, 0:L, :] = jnp.where(left, a, 0.0).astype(BF16)
        pw2_ref[sq, L:2 * L, :] = jnp.where(left, 0.0, a).astype(BF16)
        attn2_ref[sq] = (qk2_ref[sq] * db).astype(BF16)
        kn = kn_ref[sq]
        qn = qn_ref[sq]
        for r, gc in enumerate((gam_c0, gam_c1)):
            eg = jnp.exp(gc)
            rhs2_ref[sq, r * L:(r + 1) * L, r * sw + dv:(r + 1) * sw] = (kn * eg).astype(BF16)
            qd_ref[sq, r] = (qn * eg).astype(BF16)
            kdt_ref[sq, r] = (knt_ref[sq] * kdwb_ref[sq, r]).astype(BF16)
        for hk in range(n_qk):
            for r in range(rep):
                h = 2 * hk + r
                rhs2_ref[g * n_qk + hk, r * L:(r + 1) * L, r * sw:r * sw + dv] = (
                    c_ref[rows, 2 * qk_w + h * dv:2 * qk_w + (h + 1) * dv].astype(BF16))

    n_sq = 5
    for lvl in range(n_sq + 1):
        for i in range(n_ch * n_qk):
            pwb = pw2_ref[i]
            if lvl > 0:
                xv = x2_ref[i]
                x2_ref[i] = xv + _dot(xv.astype(BF16), pwb)
            if lvl < n_sq:
                pw2_ref[i] = _dot(pwb, pwb).astype(BF16)

    for i in range(n_ch * n_qk):
        sol_ref[i] = _dot(x2_ref[i].astype(BF16), rhs2_ref[i])

    zeros = jnp.zeros((L, dv), BF16)
    for g in range(n_ch):
        rows = slice(g * L, (g + 1) * L)
        for h in range(n_v):
            sol = sol_ref[g * n_qk + h // rep][:, (h % rep) * sw:(h % rep + 1) * sw]
            vn_ref[h] = (sol[:, 0:dv] - _dot(sol[:, dv:sw].astype(BF16), s_ref[h].astype(BF16))).astype(BF16)
        for hk in range(n_qk):
            iq = g * n_qk + hk
            h0, h1 = 2 * hk, 2 * hk + 1
            v0, v1 = vn_ref[h0], vn_ref[h1]
            vbd = jnp.concatenate([jnp.concatenate([v0, zeros], axis=1), jnp.concatenate([zeros, v1], axis=1)], axis=0)
            s0, s1 = s_ref[h0], s_ref[h1]
            o2 = _dot(attn2_ref[iq], vbd) + jnp.concatenate(
                [_dot(qd_ref[iq, 0], s0.astype(BF16)), _dot(qd_ref[iq, 1], s1.astype(BF16))], axis=1)
            s_ref[h0] = gt_ref[g * n_v + h0:g * n_v + h0 + 1, :] * s0 + _dot(kdt_ref[iq, 0], v0)
            s_ref[h1] = gt_ref[g * n_v + h1:g * n_v + h1 + 1, :] * s1 + _dot(kdt_ref[iq, 1], v1)
            for r, h in enumerate((h0, h1)):
                o = o2[:, r * dv:(r + 1) * dv]
                rms = lax.rsqrt(jnp.mean(o * o, axis=1, keepdims=True) + RMS_EPS)
                zz = z_ref[rows, h * dv:(h + 1) * dv].astype(F32)
                out_ref[rows, h * dv:(h + 1) * dv] = (o * rms * ng_ref[...] * (zz * _sigmoid(zz))).astype(BF16)


def _gdn(proj, small_t, conv_w, al, dt, norm_g, bsz, seq, cols, n_qk, n_v, ga_off, gb_off, n_ch):
    n = seq // (CHUNK * n_ch)
    L = CHUNK
    R = n_ch * L
    conv_ch = 2 * n_qk * GDN_DK + n_v * GDN_DV
    v_w = n_v * GDN_DV
    t = bsz * seq
    nq, nv = n_ch * n_qk, n_ch * n_v
    kern = functools.partial(_gdn_kernel, n_qk=n_qk, n_v=n_v, ga_off=ga_off, gb_off=gb_off, n_ch=n_ch)
    row = lambda b, c: b * n + c
    full = lambda a: pl.BlockSpec(a.shape, lambda b, c: (0,) * a.ndim)
    return pl.pallas_call(
        kern,
        grid=(bsz, n),
        in_specs=[
            pl.BlockSpec((R, conv_ch), lambda b, c: (row(b, c), cols["gqkv"] // conv_ch)),
            pl.BlockSpec((R, v_w), lambda b, c: (row(b, c), cols["gz"] // v_w)),
            pl.BlockSpec((n_ch, small_t.shape[1], L), lambda b, c: (row(b, c), 0, 0)),
            full(conv_w), full(al), full(dt), full(norm_g),
        ],
        out_specs=pl.BlockSpec((R, v_w), lambda b, c: (row(b, c), 0)),
        out_shape=jax.ShapeDtypeStruct((t, v_w), BF16),
        scratch_shapes=[
            pltpu.VMEM((n_v, GDN_DK, GDN_DV), F32),
            pltpu.VMEM((8 + R + 8, conv_ch), F32),
            pltpu.VMEM((R, conv_ch), F32),
            pltpu.VMEM((nq, L, GDN_DK), F32),
            pltpu.VMEM((nq, L, GDN_DK), F32),
            pltpu.VMEM((nq, GDN_DK, L), F32),
            pltpu.VMEM((nq, L, 2 * L), F32),
            pltpu.VMEM((nq, L, 2 * L), F32),
            pltpu.VMEM((nq, 2 * L, 2 * L), BF16),
            pltpu.VMEM((nq, L, 2 * L), F32),
            pltpu.VMEM((nq, 2 * L, 2 * (GDN_DV + GDN_DK)), BF16),
            pltpu.VMEM((nq, L, 2 * L), BF16),
            pltpu.VMEM((nq, 2, L, GDN_DK), BF16),
            pltpu.VMEM((nq, 2, GDN_DK, L), BF16),
            pltpu.VMEM((nq, 1, 2 * L), F32),
            pltpu.VMEM((nq, 1, 2 * L), F32),
            pltpu.VMEM((nq, 2, 1, L), F32),
            pltpu.VMEM((nv, LANES), F32),
            pltpu.VMEM((nq, L, 2 * (GDN_DV + GDN_DK)), F32),
            pltpu.VMEM((n_v, L, GDN_DV), BF16),
        ],
        compiler_params=_params("arbitrary", "arbitrary"),
        name="gdn",
    )(proj, proj, small_t, conv_w, al, dt, norm_g)


def _kv_kernel(m_ref, wk_ref, wv_ref, k_ref, v_ref):
    mb = m_ref[...].astype(BF16)
    k_ref[...] = _dot(mb, wk_ref[...]).astype(BF16)
    v_ref[...] = _dot(mb, wv_ref[...]).astype(BF16)


def _kv_proj(mem2, wk, wv, tm):
    t, d = mem2.shape
    full = lambda a: pl.BlockSpec(a.shape, lambda i: (0,) * a.ndim)
    return pl.pallas_call(
        _kv_kernel,
        grid=(t // tm,),
        in_specs=[pl.BlockSpec((tm, d), lambda i: (i, 0)), full(wk), full(wv)],
        out_specs=[pl.BlockSpec((tm, d), lambda i: (i, 0))] * 2,
        out_shape=[jax.ShapeDtypeStruct((t, d), BF16)] * 2,
        compiler_params=_params("arbitrary"),
        name="mem_kv_proj",
    )(mem2, wk, wv)


def _mix_xattn_kernel(hm_ref, og_ref, gm_ref, gg_ref, h0_ref, wbm_ref, wbg_ref, wmo_ref, g1_ref, b1_ref,
                      k_ref, v_ref, wq_ref, wo_ref, g_ref, b_ref, wr_ref, br_ref,
                      h2_ref, gw_ref, route_ref, cnt_ref, carry_ref, *, dh):
    tm = h0_ref.shape[0]

    @pl.when((pl.program_id(0) == 0) & (pl.program_id(1) == 0))
    def _():
        carry_ref[...] = jnp.zeros_like(carry_ref)

    y_ml = _dot(hm_ref[...], wbm_ref[...])
    y_gdn = _dot(og_ref[...], wbg_ref[...])
    merged = _sigmoid(gm_ref[...].astype(F32)) * y_ml + _sigmoid(gg_ref[...].astype(F32)) * y_gdn
    mix = _dot(merged.astype(BF16), wmo_ref[...])
    h1 = _layer_norm(DN_ALPHA * h0_ref[...] + mix, g1_ref[...], b1_ref[...])

    q = _dot(h1.astype(BF16), wq_ref[...])
    outs = []
    for hd in range(XA_HEADS):
        qh = q[:, hd * dh:(hd + 1) * dh].astype(BF16)
        kh = k_ref[:, hd * dh:(hd + 1) * dh]
        vh = v_ref[:, hd * dh:(hd + 1) * dh]
        sc = _dot_nt(qh, kh) * (dh ** -0.5)
        e = jnp.exp(sc - jnp.max(sc, axis=1, keepdims=True))
        p = e / jnp.sum(e, axis=1, keepdims=True)
        outs.append(_dot(p.astype(BF16), vh))
    o = jnp.concatenate(outs, axis=1)
    xa = _dot(o.astype(BF16), wo_ref[...])
    h2 = _layer_norm(DN_ALPHA * h1 + xa, g_ref[...], b_ref[...])
    h2_ref[...] = h2

    logits = _dot(h2.astype(BF16), wr_ref[...]) + br_ref[...]
    lane = lax.broadcasted_iota(jnp.int32, (tm, LANES), 1)
    lane_f = lane.astype(F32)
    work = logits
    vals, idxs = [], []
    for _ in range(TOP_K):
        m = jnp.max(work, axis=1, keepdims=True)
        idx = jnp.min(jnp.where(work == m, lane_f, float(LANES)), axis=1, keepdims=True)
        vals.append(m)
        idxs.append(idx)
        work = jnp.where(lane_f == idx, -jnp.inf, work)
    es = [jnp.exp(v - vals[0]) for v in vals]
    tot = es[0]
    for e_ in es[1:]:
        tot = tot + e_
    onehot = jnp.zeros((tm, LANES), F32)
    for idx in idxs:
        onehot = onehot + (lane_f == idx).astype(F32)
    ri = lax.broadcasted_iota(jnp.int32, (tm, tm), 0)
    ci = lax.broadcasted_iota(jnp.int32, (tm, tm), 1)
    tri = (ci < ri).astype(BF16)
    carry = carry_ref[0:1, :]
    ranks = carry + _dot(tri, onehot.astype(BF16))
    gw = jnp.zeros((tm, LANES), F32)
    route = jnp.zeros((tm, LANES), F32)
    for k_ in range(TOP_K):
        rk = jnp.sum(jnp.where(lane_f == idxs[k_], ranks, 0.0), axis=1, keepdims=True)
        gw = gw + jnp.where(lane == k_, es[k_] / tot, 0.0)
        route = route + jnp.where(lane == k_, idxs[k_], 0.0) + jnp.where(lane == TOP_K + k_, rk, 0.0)
    gw_ref[...] = gw
    route_ref[...] = route.astype(jnp.int32)
    carry = carry + jnp.sum(onehot, axis=0, keepdims=True)
    carry_ref[...] = jnp.broadcast_to(carry, carry_ref.shape)
    cnt_ref[...] = jnp.broadcast_to(carry, cnt_ref.shape)


def _mix_xattn(hm, og, proj, h0, w_bm, w_bg, w_mo, g1, b1, kmem, vmem, wq, wo, g, b, w_r, b_r, cols,
               bsz, seq, mem_len, tm):
    t, d = h0.shape
    nt = seq // tm
    once = lambda a: pl.BlockSpec(a.shape, lambda i, j: (0,) * a.ndim, pipeline_mode=pl.Buffered(1))
    rowmap = lambda i, j: (i * nt + j, 0)
    kern = functools.partial(_mix_xattn_kernel, dh=d // XA_HEADS)
    return pl.pallas_call(
        kern,
        grid=(bsz, nt),
        in_specs=[
            pl.BlockSpec((tm, hm.shape[1]), rowmap),
            pl.BlockSpec((tm, og.shape[1]), rowmap),
            pl.BlockSpec((tm, d), lambda i, j: (i * nt + j, cols["gate_ml"] // d)),
            pl.BlockSpec((tm, d), lambda i, j: (i * nt + j, cols["gate_gdn"] // d)),
            pl.BlockSpec((tm, d), rowmap),
            once(w_bm), once(w_bg), once(w_mo), once(g1), once(b1),
            pl.BlockSpec((mem_len, d), lambda i, j: (i, 0)),
            pl.BlockSpec((mem_len, d), lambda i, j: (i, 0)),
            once(wq), once(wo), once(g), once(b), once(w_r), once(b_r),
        ],
        out_specs=[
            pl.BlockSpec((tm, d), rowmap),
            pl.BlockSpec((tm, LANES), rowmap),
            pl.BlockSpec((tm, LANES), rowmap),
            pl.BlockSpec((8, LANES), lambda i, j: (0, 0)),
        ],
        out_shape=[
            jax.ShapeDtypeStruct((t, d), F32),
            jax.ShapeDtypeStruct((t, LANES), F32),
            jax.ShapeDtypeStruct((t, LANES), jnp.int32),
            jax.ShapeDtypeStruct((8, LANES), F32),
        ],
        scratch_shapes=[pltpu.VMEM((8, LANES), F32)],
        compiler_params=_params("arbitrary", "arbitrary"),
        name="mix_xattn_ln_router",
    )(hm, og, proj, proj, h0, w_bm, w_bg, w_mo, g1, b1, kmem, vmem, wq, wo, g, b, w_r, b_r)


def _dispatch_kernel(pad_lo_ref, pad_n_ref, na_ref, dest_ref, h_ref, xs_ref, zero_ref, sem, zsem):
    tm = h_ref.shape[0]
    step = pl.program_id(0)
    n_blocks = xs_ref.shape[0] // MOE_BLOCK

    @pl.when(step == 0)
    def _():
        zero_ref[...] = jnp.zeros_like(zero_ref)

        def per_expert(e, carry):
            n = pad_n_ref[e]
            pos = pad_lo_ref[e] + n

            def pieces(wait):
                def piece(start, p):
                    cp = pltpu.make_async_copy(zero_ref.at[pl.ds(0, p), :], xs_ref.at[pl.ds(start, p), :], zsem)
                    cp.wait() if wait else cp.start()

                at = pos
                p = MOE_BLOCK // 2
                while p >= SUBLANES:

                    @pl.when((n & p) != 0)
                    def _(p=p, at=at):
                        piece(pl.multiple_of(at - p, SUBLANES), p)

                    at = at - (n & p)
                    p //= 2
                for r in range(SUBLANES - 1):

                    @pl.when(r < (n & (SUBLANES - 1)))
                    def _(r=r, at=at):
                        piece(at - 1 - r, 1)

            pieces(wait=False)
            pieces(wait=True)
            return carry

        lax.fori_loop(0, N_EXPERTS, per_expert, 0)

        def tail(b, carry):
            half = MOE_BLOCK // 2
            cp = pltpu.make_async_copy(zero_ref, xs_ref.at[pl.ds(b * half, half), :], zsem)
            cp.start()
            cp.wait()
            return carry

        lax.fori_loop(2 * na_ref[0], 2 * n_blocks, tail, 0)

    for tok in range(tm):
        src = h_ref.at[pl.ds(tok, 1), :]
        for k_ in range(TOP_K):
            pltpu.make_async_copy(src, xs_ref.at[pl.ds(dest_ref[tok * TOP_K + k_], 1), :], sem).start(priority=k_ % 2)
    for _ in range(TOP_K):
        pltpu.make_async_copy(h_ref, xs_ref.at[pl.ds(0, tm), :], sem).wait()


def _dispatch(pad_lo, pad_n, nact, dest_flat, h2, n_rows, tm):
    t, d = h2.shape
    grid_spec = pltpu.PrefetchScalarGridSpec(
        num_scalar_prefetch=3,
        grid=(t // tm,),
        in_specs=[
            pl.BlockSpec((tm * TOP_K,), lambda i, *_: (i,), memory_space=pltpu.SMEM),
            pl.BlockSpec((tm, d), lambda i, *_: (i, 0)),
        ],
        out_specs=pl.BlockSpec(memory_space=pl.ANY),
        scratch_shapes=[pltpu.VMEM((MOE_BLOCK // 2, d), F32), pltpu.SemaphoreType.DMA(()),
                        pltpu.SemaphoreType.DMA(())],
    )
    return pl.pallas_call(
        _dispatch_kernel,
        grid_spec=grid_spec,
        out_shape=jax.ShapeDtypeStruct((n_rows, d), F32),
        compiler_params=_params("arbitrary"),
        name="moe_dispatch",
    )(pad_lo, pad_n, nact, dest_flat, h2)


def _expert_kernel(be_ref, na_ref, x_ref, wgu_ref, bgu_ref, wdn_ref, bdn_ref, y_ref, wgu_s, wdn_s, *, d_exp):
    blk = pl.program_id(0)
    active = blk < na_ref[0]

    @pl.when(jnp.logical_not(active))
    def _():
        y_ref[...] = jnp.zeros_like(y_ref)

    @pl.when(active & ((blk == 0) | (be_ref[blk] != be_ref[jnp.maximum(blk - 1, 0)])))
    def _():
        wgu_s[...] = wgu_ref[0].astype(BF16)
        wdn_s[...] = wdn_ref[0].astype(BF16)

    @pl.when(active)
    def _():
        gu = _dot(x_ref[...].astype(BF16), wgu_s[...]) + bgu_ref[0]
        gate = jnp.minimum(gu[:, :d_exp], SWIGLU_LIMIT)
        up = jnp.clip(gu[:, d_exp:], -SWIGLU_LIMIT, SWIGLU_LIMIT)
        act = (up + 1.0) * (gate * _sigmoid(SWIGLU_ALPHA * gate))
        y_ref[...] = _dot(act.astype(BF16), wdn_s[...]) + bdn_ref[0]


def _experts(block_e, nact, xs, w_gu, b_gu, w_dn, b_dn):
    n_rows, d = xs.shape
    n_blocks = n_rows // MOE_BLOCK
    d_exp = w_dn.shape[1]
    kern = functools.partial(_expert_kernel, d_exp=d_exp)
    emap = lambda i, be, na: (be[i], 0, 0)
    grid_spec = pltpu.PrefetchScalarGridSpec(
        num_scalar_prefetch=2,
        grid=(n_blocks,),
        in_specs=[
            pl.BlockSpec((MOE_BLOCK, d), lambda i, be, na: (i, 0)),
            pl.BlockSpec((1, d, 2 * d_exp), emap),
            pl.BlockSpec((1, 1, 2 * d_exp), emap),
            pl.BlockSpec((1, d_exp, d), emap),
            pl.BlockSpec((1, 1, d), emap),
        ],
        out_specs=pl.BlockSpec((MOE_BLOCK, d), lambda i, be, na: (i, 0)),
        scratch_shapes=[pltpu.VMEM((d, 2 * d_exp), BF16), pltpu.VMEM((d_exp, d), BF16)],
    )
    return pl.pallas_call(
        kern,
        grid_spec=grid_spec,
        out_shape=jax.ShapeDtypeStruct((n_rows, d), F32),
        compiler_params=_params("arbitrary"),
        name="moe_experts",
    )(block_e, nact, xs, w_gu, b_gu, w_dn, b_dn)


def _combine_kernel(dest_ref, dest_nxt_ref, gw_ref, h_ref, g_ref, b_ref, y_ref, out_ref, buf_a, buf_b, sem):
    tm = buf_a.shape[1]
    step = pl.program_id(0)
    n_steps = pl.num_programs(0)

    def issue(idx_ref, half, buf, s):
        for tok in range(tm):
            for k_ in range(TOP_K):
                pltpu.make_async_copy(y_ref.at[pl.ds(idx_ref[(half * tm + tok) * TOP_K + k_], 1), :],
                                      buf.at[k_, pl.ds(tok, 1), :], sem.at[s]).start(priority=k_ % 2)

    def consume(half, buf, s):
        for k_ in range(TOP_K):
            pltpu.make_async_copy(y_ref.at[pl.ds(0, tm), :], buf.at[k_], sem.at[s]).wait()
        rows = slice(half * tm, (half + 1) * tm)
        gw = gw_ref[rows, :]
        ff = buf[0] * gw[:, 0:1]
        for k_ in range(1, TOP_K):
            ff = ff + buf[k_] * gw[:, k_:k_ + 1]
        out_ref[rows, :] = _layer_norm(DN_ALPHA * h_ref[rows, :] + ff, g_ref[...], b_ref[...])

    @pl.when(step == 0)
    def _():
        issue(dest_ref, 0, buf_a, 0)

    issue(dest_ref, 1, buf_b, 1)
    consume(0, buf_a, 0)

    @pl.when(step + 1 < n_steps)
    def _():
        issue(dest_nxt_ref, 0, buf_a, 0)

    consume(1, buf_b, 1)


def _combine(dest_flat, gw, h2, g, b, yb, tm):
    t, d = h2.shape
    n = t // (2 * tm)
    return pl.pallas_call(
        _combine_kernel,
        grid=(n,),
        in_specs=[
            pl.BlockSpec((2 * tm * TOP_K,), lambda i: (i,), memory_space=pltpu.SMEM),
            pl.BlockSpec((2 * tm * TOP_K,), lambda i: (jnp.minimum(i + 1, n - 1),), memory_space=pltpu.SMEM),
            pl.BlockSpec((2 * tm, LANES), lambda i: (i, 0)),
            pl.BlockSpec((2 * tm, d), lambda i: (i, 0)),
            pl.BlockSpec((1, d), lambda i: (0, 0)),
            pl.BlockSpec((1, d), lambda i: (0, 0)),
            pl.BlockSpec(memory_space=pl.ANY),
        ],
        out_specs=pl.BlockSpec((2 * tm, d), lambda i: (i, 0)),
        out_shape=jax.ShapeDtypeStruct((t, d), F32),
        scratch_shapes=[pltpu.VMEM((TOP_K, tm, d), F32), pltpu.VMEM((TOP_K, tm, d), F32),
                        pltpu.SemaphoreType.DMA((2,))],
        compiler_params=_params("arbitrary"),
        name="moe_combine_ln3",
    )(dest_flat, dest_flat, gw, h2, g, b, yb)


def _pick(n, pref):
    return pref if n % pref == 0 else n


def kernel(x, mem, ln_in_g, ln_in_b, w_in, ml_gate_bias, ml_norm_g, gdn_conv_w, gdn_a_log, gdn_dt_bias, gdn_norm_g, w_branch_ml, w_branch_gdn, w_mix_out, ln1_g, ln1_b, xa_wq, xa_wk, xa_wv, xa_wo, ln2_g, ln2_b, w_router, b_router, w_gu, b_gu, w_dn, b_dn, ln3_g, ln3_b):
    bsz, seq, d = x.shape
    mem_len = mem.shape[1]
    t = bsz * seq
    ml_dv = d // ML_HEADS
    ml_dqk = ml_dv // 2
    ml_qk_w, ml_v_w = ML_HEADS * ml_dqk, ML_HEADS * ml_dv
    n_qk = d // GDN_DK
    n_v = 2 * n_qk
    gdn_qk_w, gdn_v_w = n_qk * GDN_DK, n_v * GDN_DV
    conv_ch = 2 * gdn_qk_w + gdn_v_w
    splits = (ml_qk_w, ml_qk_w, ml_v_w, ml_v_w, 2 * ML_HEADS, conv_ch, gdn_v_w, n_v, n_v, d, d)
    names = ("mq", "mk", "mv", "mo", "mif", "gqkv", "gz", "ga", "gb", "gate_ml", "gate_gdn")
    starts = {}
    acc = 0
    for nm, sz in zip(names, splits):
        starts[nm] = (acc, sz)
        acc += sz
    row2 = lambda a: a.reshape(1, -1).astype(F32)

    h = x.reshape(t, d)
    for l in range(DEPTH):
        w = w_in[l]
        seg = lambda nm: w[:, starts[nm][0]:starts[nm][0] + starts[nm][1]]
        order = ("gqkv", "gz", "mv", "mo", "gate_ml", "gate_gdn", "mq", "mk")
        cols = {}
        off = 0
        for nm in order:
            assert off % starts[nm][1] == 0
            cols[nm] = off
            off += starts[nm][1]
        w_big = jnp.concatenate([seg(nm) for nm in order], axis=1).astype(BF16)
        n_small = 2 * ML_HEADS + 2 * n_v
        w_small = jnp.concatenate([seg("mif"), seg("ga"), seg("gb"),
                                   jnp.zeros((d, LANES - n_small), F32)], axis=1).astype(BF16)
        ga_off, gb_off = 2 * ML_HEADS, 2 * ML_HEADS + n_v

        if l == 0:
            h0, proj, small_t = _ln_proj(h, row2(ln_in_g), row2(ln_in_b), w_big, w_small,
                                         _pick(t, 1024), _pick(off, 2816))
        else:
            raise NotImplementedError("DEPTH > 1")

        bias = ml_gate_bias[l].astype(F32)
        hm = _mlstm(proj, small_t, bias.reshape(-1, 1), row2(ml_norm_g[l]),
                    bsz, seq, cols, ml_dqk, ml_dv, ML_CHUNKS_PER_STEP)
        al = gdn_a_log[l].astype(F32)
        dt = gdn_dt_bias[l].astype(F32)
        og = _gdn(proj, small_t, gdn_conv_w[l].astype(F32), al.reshape(-1, 1), dt.reshape(-1, 1),
                  row2(gdn_norm_g[l]), bsz, seq, cols, n_qk, n_v, ga_off, gb_off, GDN_CHUNKS_PER_STEP)
        kmem, vmem = _kv_proj(mem.reshape(bsz * mem_len, d), xa_wk[l].astype(BF16), xa_wv[l].astype(BF16),
                              _pick(bsz * mem_len, 512))
        w_r = jnp.concatenate([w_router[l], jnp.zeros((d, LANES - N_EXPERTS), F32)], axis=1).astype(BF16)
        b_r = jnp.concatenate([b_router[l].astype(F32), jnp.full((LANES - N_EXPERTS,), NEG_BIG, F32)]).reshape(1, -1)
        h2, gw, route, cnt = _mix_xattn(hm, og, proj, h0, w_branch_ml[l].astype(BF16), w_branch_gdn[l].astype(BF16),
                                        w_mix_out[l].astype(BF16), row2(ln1_g[l]), row2(ln1_b[l]), kmem, vmem,
                                        xa_wq[l].astype(BF16), xa_wo[l].astype(BF16), row2(ln2_g[l]), row2(ln2_b[l]),
                                        w_r, b_r, cols, bsz, seq, mem_len, _pick(seq, 512))

        counts = cnt[0, :N_EXPERTS].astype(jnp.int32)
        padded = (counts + MOE_BLOCK - 1) // MOE_BLOCK * MOE_BLOCK
        pad_end = jnp.cumsum(padded)
        pad_start = pad_end - padded
        n_asg = t * TOP_K
        n_blocks = -(-n_asg // MOE_BLOCK) + N_EXPERTS
        n_rows = n_blocks * MOE_BLOCK
        top_e = route[:, :TOP_K]
        rank = route[:, TOP_K:2 * TOP_K]
        sel = top_e[:, :, None] == jnp.arange(N_EXPERTS, dtype=jnp.int32)[None, None, :]
        dest = (jnp.sum(jnp.where(sel, pad_start[None, None, :], 0), axis=-1) + rank).astype(jnp.int32)
        dest_flat = dest.reshape(n_asg)
        blk_row = jnp.arange(n_blocks, dtype=jnp.int32) * MOE_BLOCK
        block_e = jnp.minimum(jnp.sum(pad_end[None, :] <= blk_row[:, None], axis=1), N_EXPERTS - 1).astype(jnp.int32)
        nact = (pad_end[-1:] // MOE_BLOCK).astype(jnp.int32)
        pad_lo = (pad_start + counts).astype(jnp.int32)
        pad_n = (padded - counts).astype(jnp.int32)

        tm_d = _pick(t, 256)
        xs = _dispatch(pad_lo, pad_n, nact, dest_flat, h2, n_rows, _pick(t, 512))
        yb = _experts(block_e, nact, xs, w_gu[l], b_gu[l].astype(F32)[:, None, :],
                      w_dn[l], b_dn[l].astype(F32)[:, None, :])
        h = _combine(dest_flat, gw, h2, row2(ln3_g[l]), row2(ln3_b[l]), yb, tm_d)
    return h.reshape(bsz, seq, d)
```

```python
import functools

import jax
import jax.numpy as jnp
from jax import lax
from jax.experimental import pallas as pl
from jax.experimental.pallas import tpu as pltpu

F32 = jnp.float32
BF16 = jnp.bfloat16

CHUNK = 64
ML_HEADS = 4
GDN_DK = 128
GDN_DV = 128
CONV_K = 4
XA_HEADS = 4
N_EXPERTS = 32
TOP_K = 4
SWIGLU_LIMIT = 7.0
SWIGLU_ALPHA = 1.702
MOE_BLOCK = 512
DEPTH = 1
DN_ALPHA = (2 * DEPTH) ** 0.25
LN_EPS = 1e-5
RMS_EPS = 1e-6
LANES = 128
SUBLANES = 8
VMEM_LIMIT = 56 * 1024 * 1024
NEG_BIG = -1e30
ML_CHUNKS_PER_STEP = 8
GDN_CHUNKS_PER_STEP = 4
GDN_CONV_COLS = 512
PROJ_ROWS, PROJ_COLS = 1024, 2816
KV_ROWS = 512
MIX_ROWS = 512
DISPATCH_ROWS = 512
COMBINE_ROWS = 256


def _params(*sem):
    return pltpu.CompilerParams(dimension_semantics=sem, vmem_limit_bytes=VMEM_LIMIT)


def _dot(a, b):
    return jnp.dot(a, b, preferred_element_type=F32)


def _dot_nt(a, b):
    return lax.dot_general(a, b, (((1,), (1,)), ((), ())), preferred_element_type=F32)


def _layer_norm(x, g, b):
    mu = jnp.mean(x, axis=-1, keepdims=True)
    xc = x - mu
    var = jnp.mean(xc * xc, axis=-1, keepdims=True)
    return xc * lax.rsqrt(var + LN_EPS) * g + b


def _sigmoid(x):
    return 1.0 / (1.0 + jnp.exp(-x))


def _log_sigmoid(x):
    return jnp.minimum(x, 0.0) - jnp.log(1.0 + jnp.exp(-jnp.abs(x)))


def _softplus(x):
    return jnp.maximum(x, 0.0) + jnp.log(1.0 + jnp.exp(-jnp.abs(x)))


def _split3(x):
    hi = x.astype(BF16)
    r = x - hi.astype(F32)
    mid = r.astype(BF16)
    lo = (r - mid.astype(F32)).astype(BF16)
    return hi, mid, lo


def _cumsum_lanes(x, upper):
    hi, mid, lo = _split3(x)
    return _dot(hi, upper) + _dot(mid, upper) + _dot(lo, upper)


def _ln_proj_kernel(x_ref, g_ref, b_ref, w_ref, ws_ref, h_ref, p_ref, s_ref, xn_ref):
    @pl.when(pl.program_id(1) == 0)
    def _():
        h = _layer_norm(x_ref[...], g_ref[...], b_ref[...])
        h_ref[...] = h
        hb = h.astype(BF16)
        xn_ref[...] = hb
        s = _dot(hb, ws_ref[...])
        for c in range(s_ref.shape[0]):
            s_ref[c] = s[c * CHUNK:(c + 1) * CHUNK, :].T

    p_ref[...] = _dot(xn_ref[...], w_ref[...]).astype(BF16)


def _ln_proj(x2, g, b, w_big, w_small, tm, tn):
    t, d = x2.shape
    nw = w_big.shape[1]
    return pl.pallas_call(
        _ln_proj_kernel,
        grid=(t // tm, nw // tn),
        in_specs=[
            pl.BlockSpec((tm, d), lambda i, j: (i, 0)),
            pl.BlockSpec((1, d), lambda i, j: (0, 0)),
            pl.BlockSpec((1, d), lambda i, j: (0, 0)),
            pl.BlockSpec((d, tn), lambda i, j: (0, j)),
            pl.BlockSpec((d, LANES), lambda i, j: (0, 0)),
        ],
        out_specs=[
            pl.BlockSpec((tm, d), lambda i, j: (i, 0)),
            pl.BlockSpec((tm, tn), lambda i, j: (i, j)),
            pl.BlockSpec((tm // CHUNK, LANES, CHUNK), lambda i, j: (i, 0, 0)),
        ],
        out_shape=[
            jax.ShapeDtypeStruct((t, d), F32),
            jax.ShapeDtypeStruct((t, nw), BF16),
            jax.ShapeDtypeStruct((t // CHUNK, LANES, CHUNK), F32),
        ],
        scratch_shapes=[pltpu.VMEM((tm, d), BF16)],
        compiler_params=_params("arbitrary", "arbitrary"),
        name="ln_in_proj",
    )(x2, g, b, w_big, w_small)


def _mlstm_kernel(q_ref, k_ref, v_ref, o_ref, gr_ref, br_ref, ng_ref,
                  out_ref, c_ref, m_ref, sqk_ref, kt_ref, p_ref, ktw_ref, b3_ref, li3_ref, kw3_ref,
                  gs_ref, mc_ref, stb_ref, stm_ref, std_ref, num_ref, kv_ref, *, dqk, dv, n_ch):
    L = CHUNK
    H = ML_HEADS

    @pl.when(pl.program_id(1) == 0)
    def _():
        c_ref[...] = jnp.zeros_like(c_ref)
        m_ref[...] = jnp.zeros_like(m_ref)

    ii = lax.broadcasted_iota(jnp.int32, (L, L), 0)
    jj = lax.broadcasted_iota(jnp.int32, (L, L), 1)
    causal = jj <= ii
    diag = ii == jj
    upper = (ii <= jj).astype(BF16)
    scale = dqk ** -0.5
    heads = [(g, h) for g in range(n_ch) for h in range(H)]

    for g, h in heads:
        rows = slice(g * L, (g + 1) * L)
        kf = k_ref[rows, h * dqk:(h + 1) * dqk].astype(F32) * scale
        sqk_ref[g * H + h] = _dot_nt(q_ref[rows, h * dqk:(h + 1) * dqk], kf.astype(BF16))
        kt_ref[g * H + h] = kf.T

    pre = gr_ref[:, 0:2 * H, :].reshape(n_ch * 2 * H, L) + jnp.concatenate([br_ref[...]] * n_ch, axis=0)
    b_all = _cumsum_lanes(_log_sigmoid(pre), upper)
    gs_all = b_all[:, L - 1:L]
    a_all = gs_all - b_all + pltpu.roll(pre, H, axis=0)
    mc_all = jnp.max(a_all, axis=1, keepdims=True)
    kw_all = jnp.exp(a_all - mc_all)
    gs_ref[...] = jnp.broadcast_to(gs_all, gs_ref.shape)
    mc_ref[...] = jnp.broadcast_to(mc_all, mc_ref.shape)
    for g, h in heads:
        r = g * 2 * H + H + h
        b3_ref[g * H + h] = b_all[r:r + 1, :]
        li3_ref[g * H + h] = pre[r - H:r - H + 1, :]
        kw3_ref[g * H + h] = kw_all[r:r + 1, :]

    b_r = b3_ref[...]
    b_c = jnp.sum(jnp.where(diag, b_r, 0.0), axis=2, keepdims=True)
    dmat = jnp.where(causal, b_c - b_r + li3_ref[...], -jnp.inf)
    m_intra = jnp.max(dmat, axis=2, keepdims=True)
    p = jnp.exp(dmat - m_intra) * sqk_ref[...]
    p_ref[...] = p.astype(BF16)
    stb_ref[...] = jnp.broadcast_to(b_c, stb_ref.shape)
    stm_ref[...] = jnp.broadcast_to(m_intra, stm_ref.shape)
    std_ref[...] = jnp.broadcast_to(jnp.sum(p, axis=2, keepdims=True), std_ref.shape)
    ktw_ref[...] = (kt_ref[...] * kw3_ref[...]).astype(BF16)

    ones = jnp.ones((L, LANES), BF16)
    for g, h in heads:
        vh = v_ref[g * L:(g + 1) * L, h * dv:(h + 1) * dv]
        num_ref[g * H + h] = _dot(p_ref[g * H + h], vh)
        kv_ref[g * H + h] = _dot(ktw_ref[g * H + h], jnp.concatenate([vh, ones], axis=1))

    rep = dv // LANES
    wide = lambda s: jnp.concatenate([s] * rep, axis=1)
    for g, h in heads:
        i = g * H + h
        r = g * 2 * H + H + h
        rows = slice(g * L, (g + 1) * L)
        c_st = c_ref[h]
        m_st = m_ref[h:h + 1, :]
        qc = _dot(q_ref[rows, h * dqk:(h + 1) * dqk], c_st.astype(BF16))
        m_intra = stm_ref[i]
        inter_log = stb_ref[i] + m_st
        m_out = jnp.maximum(inter_log, m_intra)
        s_inter = jnp.exp(inter_log - m_out)
        s_intra = jnp.exp(m_intra - m_out)
        num = wide(s_inter) * qc[:, 0:dv] + wide(s_intra) * num_ref[i]
        den = s_inter * qc[:, dv:dv + LANES] + s_intra * std_ref[i]
        hh = num / wide(jnp.maximum(jnp.abs(den), jnp.exp(-m_out)))
        gs = gs_ref[r:r + 1, :]
        mc = mc_ref[r:r + 1, :]
        m_new = jnp.maximum(gs + m_st, mc)
        dec = jnp.exp(gs + m_st - m_new)
        s_new = jnp.exp(mc - m_new)
        c_ref[h] = (jnp.concatenate([dec] * (rep + 1), axis=1) * c_st
                    + jnp.concatenate([s_new] * (rep + 1), axis=1) * kv_ref[i])
        m_ref[h:h + 1, :] = m_new

        rms = lax.rsqrt(jnp.mean(hh * hh, axis=1, keepdims=True) + RMS_EPS)
        og = _sigmoid(o_ref[rows, h * dv:(h + 1) * dv].astype(F32))
        out_ref[rows, h * dv:(h + 1) * dv] = (hh * rms * ng_ref[:, h * dv:(h + 1) * dv] * og).astype(BF16)


def _mlstm(proj, small_t, bias_r, norm_g, bsz, seq, cols, dqk, dv, n_ch):
    n = seq // (CHUNK * n_ch)
    L = CHUNK
    R = n_ch * L
    H = ML_HEADS
    qk_w, v_w = H * dqk, H * dv
    t = bsz * seq
    nh = n_ch * H
    kern = functools.partial(_mlstm_kernel, dqk=dqk, dv=dv, n_ch=n_ch)
    row = lambda b, c: b * n + c
    return pl.pallas_call(
        kern,
        grid=(bsz, n),
        in_specs=[
            pl.BlockSpec((R, qk_w), lambda b, c: (row(b, c), cols["mq"] // qk_w)),
            pl.BlockSpec((R, qk_w), lambda b, c: (row(b, c), cols["mk"] // qk_w)),
            pl.BlockSpec((R, v_w), lambda b, c: (row(b, c), cols["mv"] // v_w)),
            pl.BlockSpec((R, v_w), lambda b, c: (row(b, c), cols["mo"] // v_w)),
            pl.BlockSpec((n_ch, small_t.shape[1], L), lambda b, c: (row(b, c), 0, 0)),
            pl.BlockSpec(bias_r.shape, lambda b, c: (0, 0)),
            pl.BlockSpec((1, v_w), lambda b, c: (0, 0)),
        ],
        out_specs=pl.BlockSpec((R, v_w), lambda b, c: (row(b, c), 0)),
        out_shape=jax.ShapeDtypeStruct((t, v_w), BF16),
        scratch_shapes=[
            pltpu.VMEM((H, dqk, dv + LANES), F32),
            pltpu.VMEM((SUBLANES, LANES), F32),
            pltpu.VMEM((nh, L, L), F32),
            pltpu.VMEM((nh, dqk, L), F32),
            pltpu.VMEM((nh, L, L), BF16),
            pltpu.VMEM((nh, dqk, L), BF16),
            pltpu.VMEM((nh, 1, L), F32),
            pltpu.VMEM((nh, 1, L), F32),
            pltpu.VMEM((nh, 1, L), F32),
            pltpu.VMEM((n_ch * 2 * H, LANES), F32),
            pltpu.VMEM((n_ch * 2 * H, LANES), F32),
            pltpu.VMEM((nh, L, LANES), F32),
            pltpu.VMEM((nh, L, LANES), F32),
            pltpu.VMEM((nh, L, LANES), F32),
            pltpu.VMEM((nh, L, dv), F32),
            pltpu.VMEM((nh, dqk, dv + LANES), F32),
        ],
        compiler_params=_params("arbitrary", "arbitrary"),
        name="mlstm",
    )(proj, proj, proj, proj, small_t, bias_r, norm_g)


def _gdn_kernel(x_ref, z_ref, gr_ref, cw_ref, al_ref, dt_ref, ng_ref,
                out_ref, s_ref, xs_ref, c_ref, qn_ref, kn_ref, knt_ref, kk2_ref, qk2_ref, pw2_ref, x2_ref, rhs2_ref,
                attn2_ref, qd_ref, kdt_ref, gam2_ref, beta2_ref, kdwb_ref, gt_ref, sol_ref, vn_ref,
                *, n_qk, n_v, ga_off, gb_off, n_ch):
    L = CHUNK
    R = n_ch * L
    dk, dv = GDN_DK, GDN_DV
    qk_w = n_qk * dk
    rep = n_v // n_qk
    assert rep == 2 and 2 * L == LANES and dk == LANES and dv == LANES
    conv_ch = 2 * qk_w + n_v * dv
    sw = dv + dk

    @pl.when(pl.program_id(1) == 0)
    def _():
        s_ref[...] = jnp.zeros_like(s_ref)
        xs_ref[0:SUBLANES, :] = jnp.zeros((SUBLANES, xs_ref.shape[1]), F32)
        rhs2_ref[...] = jnp.zeros_like(rhs2_ref)

    T0 = SUBLANES
    xs_ref[T0:T0 + R, :] = x_ref[...].astype(F32)
    for cb in range(0, conv_ch, GDN_CONV_COLS):
        cs = slice(cb, cb + GDN_CONV_COLS)
        xa = xs_ref[0:T0 + R, cs]
        conv = cw_ref[CONV_K - 1:CONV_K, cs] * xa[T0:T0 + R]
        for sh in range(1, CONV_K):
            conv = conv + cw_ref[CONV_K - 1 - sh:CONV_K - sh, cs] * pltpu.roll(xa, sh, axis=0)[T0:T0 + R]
        c_ref[:, cs] = conv * _sigmoid(conv)
    xs_ref[0:T0, :] = xs_ref[R:R + T0, :]

    ii = lax.broadcasted_iota(jnp.int32, (L, L), 0)
    jj = lax.broadcasted_iota(jnp.int32, (L, L), 1)
    upper = (ii <= jj).astype(BF16)
    row2 = lax.broadcasted_iota(jnp.int32, (L, LANES), 0)
    lane2 = lax.broadcasted_iota(jnp.int32, (L, LANES), 1)
    left = lane2 < L
    col2 = jnp.where(left, lane2, lane2 - L)
    incl2 = col2 <= row2
    strict2 = col2 < row2
    diag2 = col2 == row2
    eye2 = diag2.astype(F32)

    for g in range(n_ch):
        rows = slice(g * L, (g + 1) * L)
        for hk in range(n_qk):
            iq = g * n_qk + hk
            cq = c_ref[rows, hk * dk:(hk + 1) * dk]
            ck = c_ref[rows, qk_w + hk * dk:qk_w + (hk + 1) * dk]
            qn = cq * lax.rsqrt(jnp.sum(cq * cq, axis=1, keepdims=True) + RMS_EPS) * (dk ** -0.5)
            kn = ck * lax.rsqrt(jnp.sum(ck * ck, axis=1, keepdims=True) + RMS_EPS)
            qn_ref[iq] = qn
            kn_ref[iq] = kn
            knt_ref[iq] = kn.T
            kb = kn.astype(BF16)
            kb2 = jnp.concatenate([kb, kb], axis=0)
            kk2_ref[iq] = _dot_nt(kb, kb2)
            qk2_ref[iq] = _dot_nt(qn.astype(BF16), kb2)

    for g in range(n_ch):
        gd = -jnp.exp(al_ref[...]) * _softplus(gr_ref[g, ga_off:ga_off + n_v, :] + dt_ref[...])
        gam = _cumsum_lanes(gd, upper)
        beta = _sigmoid(gr_ref[g, gb_off:gb_off + n_v, :])
        g_tot = gam[:, L - 1:L]
        kdwb = jnp.exp(g_tot - gam) * beta
        gt_ref[g * n_v:(g + 1) * n_v, :] = jnp.broadcast_to(jnp.exp(g_tot), (n_v, LANES))
        for hk in range(n_qk):
            iq = g * n_qk + hk
            gam2_ref[iq] = jnp.concatenate([gam[2 * hk:2 * hk + 1, :], gam[2 * hk + 1:2 * hk + 2, :]], axis=1)
            beta2_ref[iq] = jnp.concatenate([beta[2 * hk:2 * hk + 1, :], beta[2 * hk + 1:2 * hk + 2, :]], axis=1)
            for r in range(rep):
                kdwb_ref[iq, r] = kdwb[2 * hk + r:2 * hk + r + 1, :]

    for g in range(n_ch):
        sq = slice(g * n_qk, (g + 1) * n_qk)
        rows = slice(g * L, (g + 1) * L)
        gam_r = gam2_ref[sq]
        picked = jnp.where(diag2, gam_r, 0.0)
        gam_c0 = jnp.sum(jnp.where(left, picked, 0.0), axis=2, keepdims=True)
        gam_c1 = jnp.sum(jnp.where(left, 0.0, picked), axis=2, keepdims=True)
        gam_c = jnp.where(left, gam_c0, gam_c1)
        decm = jnp.exp(jnp.where(incl2, gam_c - gam_r, -jnp.inf))
        db = decm * beta2_ref[sq]
        a = jnp.where(strict2, kk2_ref[sq] * db, 0.0)
        x2_ref[sq] = eye2 - a
        pw2_ref[sq, 0:L, :] = jnp.where(left, a, 0.0).astype(BF16)
        pw2_ref[sq, L:2 * L, :] = jnp.where(left, 0.0, a).astype(BF16)
        attn2_ref[sq] = (qk2_ref[sq] * db).astype(BF16)
        kn = kn_ref[sq]
        qn = qn_ref[sq]
        for r, gc in enumerate((gam_c0, gam_c1)):
            eg = jnp.exp(gc)
            rhs2_ref[sq, r * L:(r + 1) * L, r * sw + dv:(r + 1) * sw] = (kn * eg).astype(BF16)
            qd_ref[sq, r] = (qn * eg).astype(BF16)
            kdt_ref[sq, r] = (knt_ref[sq] * kdwb_ref[sq, r]).astype(BF16)
        for hk in range(n_qk):
            for r in range(rep):
                h = 2 * hk + r
                rhs2_ref[g * n_qk + hk, r * L:(r + 1) * L, r * sw:r * sw + dv] = (
                    c_ref[rows, 2 * qk_w + h * dv:2 * qk_w + (h + 1) * dv].astype(BF16))

    n_sq = L.bit_length() - 2
    for lvl in range(n_sq + 1):
        for i in range(n_ch * n_qk):
            pwb = pw2_ref[i]
            if lvl > 0:
                xv = x2_ref[i]
                x2_ref[i] = xv + _dot(xv.astype(BF16), pwb)
            if lvl < n_sq:
                pw2_ref[i] = _dot(pwb, pwb).astype(BF16)

    for i in range(n_ch * n_qk):
        sol_ref[i] = _dot(x2_ref[i].astype(BF16), rhs2_ref[i])

    zeros = jnp.zeros((L, dv), BF16)
    for g in range(n_ch):
        rows = slice(g * L, (g + 1) * L)
        for h in range(n_v):
            sol = sol_ref[g * n_qk + h // rep][:, (h % rep) * sw:(h % rep + 1) * sw]
            vn_ref[h] = (sol[:, 0:dv] - _dot(sol[:, dv:sw].astype(BF16), s_ref[h].astype(BF16))).astype(BF16)
        for hk in range(n_qk):
            iq = g * n_qk + hk
            h0, h1 = 2 * hk, 2 * hk + 1
            v0, v1 = vn_ref[h0], vn_ref[h1]
            vbd = jnp.concatenate([jnp.concatenate([v0, zeros], axis=1), jnp.concatenate([zeros, v1], axis=1)], axis=0)
            s0, s1 = s_ref[h0], s_ref[h1]
            o2 = _dot(attn2_ref[iq], vbd) + jnp.concatenate(
                [_dot(qd_ref[iq, 0], s0.astype(BF16)), _dot(qd_ref[iq, 1], s1.astype(BF16))], axis=1)
            s_ref[h0] = gt_ref[g * n_v + h0:g * n_v + h0 + 1, :] * s0 + _dot(kdt_ref[iq, 0], v0)
            s_ref[h1] = gt_ref[g * n_v + h1:g * n_v + h1 + 1, :] * s1 + _dot(kdt_ref[iq, 1], v1)
            for r, h in enumerate((h0, h1)):
                o = o2[:, r * dv:(r + 1) * dv]
                rms = lax.rsqrt(jnp.mean(o * o, axis=1, keepdims=True) + RMS_EPS)
                zz = z_ref[rows, h * dv:(h + 1) * dv].astype(F32)
                out_ref[rows, h * dv:(h + 1) * dv] = (o * rms * ng_ref[...] * (zz * _sigmoid(zz))).astype(BF16)


def _gdn(proj, small_t, conv_w, al, dt, norm_g, bsz, seq, cols, n_qk, n_v, ga_off, gb_off, n_ch):
    n = seq // (CHUNK * n_ch)
    L = CHUNK
    R = n_ch * L
    conv_ch = 2 * n_qk * GDN_DK + n_v * GDN_DV
    v_w = n_v * GDN_DV
    t = bsz * seq
    nq, nv = n_ch * n_qk, n_ch * n_v
    kern = functools.partial(_gdn_kernel, n_qk=n_qk, n_v=n_v, ga_off=ga_off, gb_off=gb_off, n_ch=n_ch)
    row = lambda b, c: b * n + c
    full = lambda a: pl.BlockSpec(a.shape, lambda b, c: (0,) * a.ndim)
    return pl.pallas_call(
        kern,
        grid=(bsz, n),
        in_specs=[
            pl.BlockSpec((R, conv_ch), lambda b, c: (row(b, c), cols["gqkv"] // conv_ch)),
            pl.BlockSpec((R, v_w), lambda b, c: (row(b, c), cols["gz"] // v_w)),
            pl.BlockSpec((n_ch, small_t.shape[1], L), lambda b, c: (row(b, c), 0, 0)),
            full(conv_w), full(al), full(dt), full(norm_g),
        ],
        out_specs=pl.BlockSpec((R, v_w), lambda b, c: (row(b, c), 0)),
        out_shape=jax.ShapeDtypeStruct((t, v_w), BF16),
        scratch_shapes=[
            pltpu.VMEM((n_v, GDN_DK, GDN_DV), F32),
            pltpu.VMEM((SUBLANES + R, conv_ch), F32),
            pltpu.VMEM((R, conv_ch), F32),
            pltpu.VMEM((nq, L, GDN_DK), F32),
            pltpu.VMEM((nq, L, GDN_DK), F32),
            pltpu.VMEM((nq, GDN_DK, L), F32),
            pltpu.VMEM((nq, L, 2 * L), F32),
            pltpu.VMEM((nq, L, 2 * L), F32),
            pltpu.VMEM((nq, 2 * L, 2 * L), BF16),
            pltpu.VMEM((nq, L, 2 * L), F32),
            pltpu.VMEM((nq, 2 * L, 2 * (GDN_DV + GDN_DK)), BF16),
            pltpu.VMEM((nq, L, 2 * L), BF16),
            pltpu.VMEM((nq, 2, L, GDN_DK), BF16),
            pltpu.VMEM((nq, 2, GDN_DK, L), BF16),
            pltpu.VMEM((nq, 1, 2 * L), F32),
            pltpu.VMEM((nq, 1, 2 * L), F32),
            pltpu.VMEM((nq, 2, 1, L), F32),
            pltpu.VMEM((nv, LANES), F32),
            pltpu.VMEM((nq, L, 2 * (GDN_DV + GDN_DK)), F32),
            pltpu.VMEM((n_v, L, GDN_DV), BF16),
        ],
        compiler_params=_params("arbitrary", "arbitrary"),
        name="gdn",
    )(proj, proj, small_t, conv_w, al, dt, norm_g)


def _kv_kernel(m_ref, wk_ref, wv_ref, k_ref, v_ref):
    mb = m_ref[...].astype(BF16)
    k_ref[...] = _dot(mb, wk_ref[...]).astype(BF16)
    v_ref[...] = _dot(mb, wv_ref[...]).astype(BF16)


def _kv_proj(mem2, wk, wv, tm):
    t, d = mem2.shape
    full = lambda a: pl.BlockSpec(a.shape, lambda i: (0,) * a.ndim)
    return pl.pallas_call(
        _kv_kernel,
        grid=(t // tm,),
        in_specs=[pl.BlockSpec((tm, d), lambda i: (i, 0)), full(wk), full(wv)],
        out_specs=[pl.BlockSpec((tm, d), lambda i: (i, 0))] * 2,
        out_shape=[jax.ShapeDtypeStruct((t, d), BF16)] * 2,
        compiler_params=_params("arbitrary"),
        name="mem_kv_proj",
    )(mem2, wk, wv)


def _mix_xattn_kernel(hm_ref, og_ref, gm_ref, gg_ref, h0_ref, wbm_ref, wbg_ref, wmo_ref, g1_ref, b1_ref,
                      k_ref, v_ref, wq_ref, wo_ref, g_ref, b_ref, wr_ref, br_ref,
                      h2_ref, gw_ref, route_ref, cnt_ref, carry_ref, *, dh):
    tm = h0_ref.shape[0]

    @pl.when((pl.program_id(0) == 0) & (pl.program_id(1) == 0))
    def _():
        carry_ref[...] = jnp.zeros_like(carry_ref)

    y_ml = _dot(hm_ref[...], wbm_ref[...])
    y_gdn = _dot(og_ref[...], wbg_ref[...])
    merged = _sigmoid(gm_ref[...].astype(F32)) * y_ml + _sigmoid(gg_ref[...].astype(F32)) * y_gdn
    mix = _dot(merged.astype(BF16), wmo_ref[...])
    h1 = _layer_norm(DN_ALPHA * h0_ref[...] + mix, g1_ref[...], b1_ref[...])

    q = _dot(h1.astype(BF16), wq_ref[...])
    outs = []
    for hd in range(XA_HEADS):
        qh = q[:, hd * dh:(hd + 1) * dh].astype(BF16)
        kh = k_ref[:, hd * dh:(hd + 1) * dh]
        vh = v_ref[:, hd * dh:(hd + 1) * dh]
        sc = _dot_nt(qh, kh) * (dh ** -0.5)
        e = jnp.exp(sc - jnp.max(sc, axis=1, keepdims=True))
        p = e / jnp.sum(e, axis=1, keepdims=True)
        outs.append(_dot(p.astype(BF16), vh))
    o = jnp.concatenate(outs, axis=1)
    xa = _dot(o.astype(BF16), wo_ref[...])
    h2 = _layer_norm(DN_ALPHA * h1 + xa, g_ref[...], b_ref[...])
    h2_ref[...] = h2

    logits = _dot(h2.astype(BF16), wr_ref[...]) + br_ref[...]
    lane = lax.broadcasted_iota(jnp.int32, (tm, LANES), 1)
    lane_f = lane.astype(F32)
    work = logits
    vals, idxs = [], []
    for _ in range(TOP_K):
        m = jnp.max(work, axis=1, keepdims=True)
        idx = jnp.min(jnp.where(work == m, lane_f, float(LANES)), axis=1, keepdims=True)
        vals.append(m)
        idxs.append(idx)
        work = jnp.where(lane_f == idx, -jnp.inf, work)
    es = [jnp.exp(v - vals[0]) for v in vals]
    tot = es[0]
    for e_ in es[1:]:
        tot = tot + e_
    onehot = jnp.zeros((tm, LANES), F32)
    for idx in idxs:
        onehot = onehot + (lane_f == idx).astype(F32)
    ri = lax.broadcasted_iota(jnp.int32, (tm, tm), 0)
    ci = lax.broadcasted_iota(jnp.int32, (tm, tm), 1)
    tri = (ci < ri).astype(BF16)
    carry = carry_ref[0:1, :]
    ranks = carry + _dot(tri, onehot.astype(BF16))
    gw = jnp.zeros((tm, LANES), F32)
    route = jnp.zeros((tm, LANES), F32)
    for k_ in range(TOP_K):
        rk = jnp.sum(jnp.where(lane_f == idxs[k_], ranks, 0.0), axis=1, keepdims=True)
        gw = gw + jnp.where(lane == k_, es[k_] / tot, 0.0)
        route = route + jnp.where(lane == k_, idxs[k_], 0.0) + jnp.where(lane == TOP_K + k_, rk, 0.0)
    gw_ref[...] = gw
    route_ref[...] = route.astype(jnp.int32)
    carry = carry + jnp.sum(onehot, axis=0, keepdims=True)
    carry_ref[...] = jnp.broadcast_to(carry, carry_ref.shape)
    cnt_ref[...] = jnp.broadcast_to(carry, cnt_ref.shape)


def _mix_xattn(hm, og, proj, h0, w_bm, w_bg, w_mo, g1, b1, kmem, vmem, wq, wo, g, b, w_r, b_r, cols,
               bsz, seq, mem_len, tm):
    t, d = h0.shape
    nt = seq // tm
    once = lambda a: pl.BlockSpec(a.shape, lambda i, j: (0,) * a.ndim, pipeline_mode=pl.Buffered(1))
    rowmap = lambda i, j: (i * nt + j, 0)
    kern = functools.partial(_mix_xattn_kernel, dh=d // XA_HEADS)
    return pl.pallas_call(
        kern,
        grid=(bsz, nt),
        in_specs=[
            pl.BlockSpec((tm, hm.shape[1]), rowmap),
            pl.BlockSpec((tm, og.shape[1]), rowmap),
            pl.BlockSpec((tm, d), lambda i, j: (i * nt + j, cols["gate_ml"] // d)),
            pl.BlockSpec((tm, d), lambda i, j: (i * nt + j, cols["gate_gdn"] // d)),
            pl.BlockSpec((tm, d), rowmap),
            once(w_bm), once(w_bg), once(w_mo), once(g1), once(b1),
            pl.BlockSpec((mem_len, d), lambda i, j: (i, 0)),
            pl.BlockSpec((mem_len, d), lambda i, j: (i, 0)),
            once(wq), once(wo), once(g), once(b), once(w_r), once(b_r),
        ],
        out_specs=[
            pl.BlockSpec((tm, d), rowmap),
            pl.BlockSpec((tm, LANES), rowmap),
            pl.BlockSpec((tm, LANES), rowmap),
            pl.BlockSpec((SUBLANES, LANES), lambda i, j: (0, 0)),
        ],
        out_shape=[
            jax.ShapeDtypeStruct((t, d), F32),
            jax.ShapeDtypeStruct((t, LANES), F32),
            jax.ShapeDtypeStruct((t, LANES), jnp.int32),
            jax.ShapeDtypeStruct((SUBLANES, LANES), F32),
        ],
        scratch_shapes=[pltpu.VMEM((SUBLANES, LANES), F32)],
        compiler_params=_params("arbitrary", "arbitrary"),
        name="mix_xattn_ln_router",
    )(hm, og, proj, proj, h0, w_bm, w_bg, w_mo, g1, b1, kmem, vmem, wq, wo, g, b, w_r, b_r)


def _dispatch_kernel(pad_lo_ref, pad_n_ref, na_ref, dest_ref, h_ref, xs_ref, zero_ref, sem, zsem):
    tm = h_ref.shape[0]
    step = pl.program_id(0)
    n_blocks = xs_ref.shape[0] // MOE_BLOCK

    @pl.when(step == 0)
    def _():
        zero_ref[...] = jnp.zeros_like(zero_ref)

        def per_expert(e, carry):
            n = pad_n_ref[e]
            pos = pad_lo_ref[e] + n

            def pieces(wait):
                def piece(start, p):
                    cp = pltpu.make_async_copy(zero_ref.at[pl.ds(0, p), :], xs_ref.at[pl.ds(start, p), :], zsem)
                    cp.wait() if wait else cp.start()

                at = pos
                p = MOE_BLOCK // 2
                while p >= SUBLANES:

                    @pl.when((n & p) != 0)
                    def _(p=p, at=at):
                        piece(pl.multiple_of(at - p, SUBLANES), p)

                    at = at - (n & p)
                    p //= 2
                for r in range(SUBLANES - 1):

                    @pl.when(r < (n & (SUBLANES - 1)))
                    def _(r=r, at=at):
                        piece(at - 1 - r, 1)

            pieces(wait=False)
            pieces(wait=True)
            return carry

        lax.fori_loop(0, N_EXPERTS, per_expert, 0)

        def tail(b, carry):
            half = MOE_BLOCK // 2
            cp = pltpu.make_async_copy(zero_ref, xs_ref.at[pl.ds(b * half, half), :], zsem)
            cp.start()
            cp.wait()
            return carry

        lax.fori_loop(2 * na_ref[0], 2 * n_blocks, tail, 0)

    for tok in range(tm):
        src = h_ref.at[pl.ds(tok, 1), :]
        for k_ in range(TOP_K):
            pltpu.make_async_copy(src, xs_ref.at[pl.ds(dest_ref[tok * TOP_K + k_], 1), :], sem).start(priority=k_ % 2)
    for _ in range(TOP_K):
        pltpu.make_async_copy(h_ref, xs_ref.at[pl.ds(0, tm), :], sem).wait()


def _dispatch(pad_lo, pad_n, nact, dest_flat, h2, n_rows, tm):
    t, d = h2.shape
    grid_spec = pltpu.PrefetchScalarGridSpec(
        num_scalar_prefetch=3,
        grid=(t // tm,),
        in_specs=[
            pl.BlockSpec((tm * TOP_K,), lambda i, *_: (i,), memory_space=pltpu.SMEM),
            pl.BlockSpec((tm, d), lambda i, *_: (i, 0)),
        ],
        out_specs=pl.BlockSpec(memory_space=pl.ANY),
        scratch_shapes=[pltpu.VMEM((MOE_BLOCK // 2, d), F32), pltpu.SemaphoreType.DMA(()),
                        pltpu.SemaphoreType.DMA(())],
    )
    return pl.pallas_call(
        _dispatch_kernel,
        grid_spec=grid_spec,
        out_shape=jax.ShapeDtypeStruct((n_rows, d), F32),
        compiler_params=_params("arbitrary"),
        name="moe_dispatch",
    )(pad_lo, pad_n, nact, dest_flat, h2)


def _expert_kernel(be_ref, na_ref, x_ref, wgu_ref, bgu_ref, wdn_ref, bdn_ref, y_ref, wgu_s, wdn_s, *, d_exp):
    blk = pl.program_id(0)
    active = blk < na_ref[0]

    @pl.when(jnp.logical_not(active))
    def _():
        y_ref[...] = jnp.zeros_like(y_ref)

    @pl.when(active & ((blk == 0) | (be_ref[blk] != be_ref[jnp.maximum(blk - 1, 0)])))
    def _():
        wgu_s[...] = wgu_ref[0].astype(BF16)
        wdn_s[...] = wdn_ref[0].astype(BF16)

    @pl.when(active)
    def _():
        gu = _dot(x_ref[...].astype(BF16), wgu_s[...]) + bgu_ref[0]
        gate = jnp.minimum(gu[:, :d_exp], SWIGLU_LIMIT)
        up = jnp.clip(gu[:, d_exp:], -SWIGLU_LIMIT, SWIGLU_LIMIT)
        act = (up + 1.0) * (gate * _sigmoid(SWIGLU_ALPHA * gate))
        y_ref[...] = _dot(act.astype(BF16), wdn_s[...]) + bdn_ref[0]


def _experts(block_e, nact, xs, w_gu, b_gu, w_dn, b_dn):
    n_rows, d = xs.shape
    n_blocks = n_rows // MOE_BLOCK
    d_exp = w_dn.shape[1]
    kern = functools.partial(_expert_kernel, d_exp=d_exp)
    emap = lambda i, be, na: (be[i], 0, 0)
    grid_spec = pltpu.PrefetchScalarGridSpec(
        num_scalar_prefetch=2,
        grid=(n_blocks,),
        in_specs=[
            pl.BlockSpec((MOE_BLOCK, d), lambda i, be, na: (i, 0)),
            pl.BlockSpec((1, d, 2 * d_exp), emap),
            pl.BlockSpec((1, 1, 2 * d_exp), emap),
            pl.BlockSpec((1, d_exp, d), emap),
            pl.BlockSpec((1, 1, d), emap),
        ],
        out_specs=pl.BlockSpec((MOE_BLOCK, d), lambda i, be, na: (i, 0)),
        scratch_shapes=[pltpu.VMEM((d, 2 * d_exp), BF16), pltpu.VMEM((d_exp, d), BF16)],
    )
    return pl.pallas_call(
        kern,
        grid_spec=grid_spec,
        out_shape=jax.ShapeDtypeStruct((n_rows, d), F32),
        compiler_params=_params("arbitrary"),
        name="moe_experts",
    )(block_e, nact, xs, w_gu, b_gu, w_dn, b_dn)


def _combine_kernel(dest_ref, dest_nxt_ref, gw_ref, h_ref, g_ref, b_ref, y_ref, out_ref, buf_a, buf_b, sem):
    tm = buf_a.shape[1]
    step = pl.program_id(0)
    n_steps = pl.num_programs(0)

    def issue(idx_ref, half, buf, s):
        for tok in range(tm):
            for k_ in range(TOP_K):
                pltpu.make_async_copy(y_ref.at[pl.ds(idx_ref[(half * tm + tok) * TOP_K + k_], 1), :],
                                      buf.at[k_, pl.ds(tok, 1), :], sem.at[s]).start(priority=k_ % 2)

    def consume(half, buf, s):
        for k_ in range(TOP_K):
            pltpu.make_async_copy(y_ref.at[pl.ds(0, tm), :], buf.at[k_], sem.at[s]).wait()
        rows = slice(half * tm, (half + 1) * tm)
        gw = gw_ref[rows, :]
        ff = buf[0] * gw[:, 0:1]
        for k_ in range(1, TOP_K):
            ff = ff + buf[k_] * gw[:, k_:k_ + 1]
        out_ref[rows, :] = _layer_norm(DN_ALPHA * h_ref[rows, :] + ff, g_ref[...], b_ref[...])

    @pl.when(step == 0)
    def _():
        issue(dest_ref, 0, buf_a, 0)

    issue(dest_ref, 1, buf_b, 1)
    consume(0, buf_a, 0)

    @pl.when(step + 1 < n_steps)
    def _():
        issue(dest_nxt_ref, 0, buf_a, 0)

    consume(1, buf_b, 1)


def _combine(dest_flat, gw, h2, g, b, yb, tm):
    t, d = h2.shape
    n = t // (2 * tm)
    return pl.pallas_call(
        _combine_kernel,
        grid=(n,),
        in_specs=[
            pl.BlockSpec((2 * tm * TOP_K,), lambda i: (i,), memory_space=pltpu.SMEM),
            pl.BlockSpec((2 * tm * TOP_K,), lambda i: (jnp.minimum(i + 1, n - 1),), memory_space=pltpu.SMEM),
            pl.BlockSpec((2 * tm, LANES), lambda i: (i, 0)),
            pl.BlockSpec((2 * tm, d), lambda i: (i, 0)),
            pl.BlockSpec((1, d), lambda i: (0, 0)),
            pl.BlockSpec((1, d), lambda i: (0, 0)),
            pl.BlockSpec(memory_space=pl.ANY),
        ],
        out_specs=pl.BlockSpec((2 * tm, d), lambda i: (i, 0)),
        out_shape=jax.ShapeDtypeStruct((t, d), F32),
        scratch_shapes=[pltpu.VMEM((TOP_K, tm, d), F32), pltpu.VMEM((TOP_K, tm, d), F32),
                        pltpu.SemaphoreType.DMA((2,))],
        compiler_params=_params("arbitrary"),
        name="moe_combine_ln3",
    )(dest_flat, dest_flat, gw, h2, g, b, yb)


def _pick(n, pref):
    return pref if n % pref == 0 else n


def kernel(x, mem, ln_in_g, ln_in_b, w_in, ml_gate_bias, ml_norm_g, gdn_conv_w, gdn_a_log, gdn_dt_bias, gdn_norm_g, w_branch_ml, w_branch_gdn, w_mix_out, ln1_g, ln1_b, xa_wq, xa_wk, xa_wv, xa_wo, ln2_g, ln2_b, w_router, b_router, w_gu, b_gu, w_dn, b_dn, ln3_g, ln3_b):
    bsz, seq, d = x.shape
    mem_len = mem.shape[1]
    t = bsz * seq
    ml_dv = d // ML_HEADS
    ml_dqk = ml_dv // 2
    ml_qk_w, ml_v_w = ML_HEADS * ml_dqk, ML_HEADS * ml_dv
    n_qk = d // GDN_DK
    n_v = 2 * n_qk
    gdn_qk_w, gdn_v_w = n_qk * GDN_DK, n_v * GDN_DV
    conv_ch = 2 * gdn_qk_w + gdn_v_w
    splits = (ml_qk_w, ml_qk_w, ml_v_w, ml_v_w, 2 * ML_HEADS, conv_ch, gdn_v_w, n_v, n_v, d, d)
    names = ("mq", "mk", "mv", "mo", "mif", "gqkv", "gz", "ga", "gb", "gate_ml", "gate_gdn")
    starts = {}
    acc = 0
    for nm, sz in zip(names, splits):
        starts[nm] = (acc, sz)
        acc += sz
    row2 = lambda a: a.reshape(1, -1).astype(F32)

    h = x.reshape(t, d)
    for l in range(DEPTH):
        w = w_in[l]
        seg = lambda nm: w[:, starts[nm][0]:starts[nm][0] + starts[nm][1]]
        order = ("gqkv", "gz", "mv", "mo", "gate_ml", "gate_gdn", "mq", "mk")
        cols = {}
        off = 0
        for nm in order:
            assert off % starts[nm][1] == 0
            cols[nm] = off
            off += starts[nm][1]
        w_big = jnp.concatenate([seg(nm) for nm in order], axis=1).astype(BF16)
        n_small = 2 * ML_HEADS + 2 * n_v
        w_small = jnp.concatenate([seg("mif"), seg("ga"), seg("gb"),
                                   jnp.zeros((d, LANES - n_small), F32)], axis=1).astype(BF16)
        ga_off, gb_off = 2 * ML_HEADS, 2 * ML_HEADS + n_v

        if l == 0:
            h0, proj, small_t = _ln_proj(h, row2(ln_in_g), row2(ln_in_b), w_big, w_small,
                                         _pick(t, PROJ_ROWS), _pick(off, PROJ_COLS))
        else:
            raise NotImplementedError("DEPTH > 1")

        bias = ml_gate_bias[l].astype(F32)
        hm = _mlstm(proj, small_t, bias.reshape(-1, 1), row2(ml_norm_g[l]),
                    bsz, seq, cols, ml_dqk, ml_dv, ML_CHUNKS_PER_STEP)
        al = gdn_a_log[l].astype(F32)
        dt = gdn_dt_bias[l].astype(F32)
        og = _gdn(proj, small_t, gdn_conv_w[l].astype(F32), al.reshape(-1, 1), dt.reshape(-1, 1),
                  row2(gdn_norm_g[l]), bsz, seq, cols, n_qk, n_v, ga_off, gb_off, GDN_CHUNKS_PER_STEP)
        kmem, vmem = _kv_proj(mem.reshape(bsz * mem_len, d), xa_wk[l].astype(BF16), xa_wv[l].astype(BF16),
                              _pick(bsz * mem_len, KV_ROWS))
        w_r = jnp.concatenate([w_router[l], jnp.zeros((d, LANES - N_EXPERTS), F32)], axis=1).astype(BF16)
        b_r = jnp.concatenate([b_router[l].astype(F32), jnp.full((LANES - N_EXPERTS,), NEG_BIG, F32)]).reshape(1, -1)
        h2, gw, route, cnt = _mix_xattn(hm, og, proj, h0, w_branch_ml[l].astype(BF16), w_branch_gdn[l].astype(BF16),
                                        w_mix_out[l].astype(BF16), row2(ln1_g[l]), row2(ln1_b[l]), kmem, vmem,
                                        xa_wq[l].astype(BF16), xa_wo[l].astype(BF16), row2(ln2_g[l]), row2(ln2_b[l]),
                                        w_r, b_r, cols, bsz, seq, mem_len, _pick(seq, MIX_ROWS))

        counts = cnt[0, :N_EXPERTS].astype(jnp.int32)
        padded = (counts + MOE_BLOCK - 1) // MOE_BLOCK * MOE_BLOCK
        pad_end = jnp.cumsum(padded)
        pad_start = pad_end - padded
        n_asg = t * TOP_K
        n_blocks = -(-n_asg // MOE_BLOCK) + N_EXPERTS
        n_rows = n_blocks * MOE_BLOCK
        top_e = route[:, :TOP_K]
        rank = route[:, TOP_K:2 * TOP_K]
        sel = top_e[:, :, None] == jnp.arange(N_EXPERTS, dtype=jnp.int32)[None, None, :]
        dest = (jnp.sum(jnp.where(sel, pad_start[None, None, :], 0), axis=-1) + rank).astype(jnp.int32)
        dest_flat = dest.reshape(n_asg)
        blk_row = jnp.arange(n_blocks, dtype=jnp.int32) * MOE_BLOCK
        block_e = jnp.minimum(jnp.sum(pad_end[None, :] <= blk_row[:, None], axis=1), N_EXPERTS - 1).astype(jnp.int32)
        nact = (pad_end[-1:] // MOE_BLOCK).astype(jnp.int32)
        pad_lo = (pad_start + counts).astype(jnp.int32)
        pad_n = (padded - counts).astype(jnp.int32)

        xs = _dispatch(pad_lo, pad_n, nact, dest_flat, h2, n_rows, _pick(t, DISPATCH_ROWS))
        yb = _experts(block_e, nact, xs, w_gu[l], b_gu[l].astype(F32)[:, None, :],
                      w_dn[l], b_dn[l].astype(F32)[:, None, :])
        h = _combine(dest_flat, gw, h2, row2(ln3_g[l]), row2(ln3_b[l]), yb, _pick(t // 2, COMBINE_ROWS))
    return h.reshape(bsz, seq, d)
```

```python
import functools

import jax
import jax.numpy as jnp
from jax import lax
from jax.experimental import pallas as pl
from jax.experimental.pallas import tpu as pltpu

F32 = jnp.float32
BF16 = jnp.bfloat16

CHUNK = 64
ML_HEADS = 4
GDN_DK = 128
GDN_DV = 128
CONV_K = 4
XA_HEADS = 4
N_EXPERTS = 32
TOP_K = 4
SWIGLU_LIMIT = 7.0
SWIGLU_ALPHA = 1.702
MOE_BLOCK = 512
DEPTH = 1
DN_ALPHA = (2 * DEPTH) ** 0.25
LN_EPS = 1e-5
RMS_EPS = 1e-6
LANES = 128
SUBLANES = 8
VMEM_LIMIT = 56 * 1024 * 1024
NEG_BIG = -1e30
ML_CHUNKS_PER_STEP = 8
GDN_CHUNKS_PER_STEP = 4
GDN_CONV_COLS = 512
PROJ_ROWS, PROJ_COLS = 1024, 2816
KV_ROWS = 512
MIX_ROWS = 512
MIX_SPLIT = 2
DISPATCH_ROWS = 512
COMBINE_ROWS = 256


def _params(*sem):
    return pltpu.CompilerParams(dimension_semantics=sem, vmem_limit_bytes=VMEM_LIMIT)


def _dot(a, b):
    return jnp.dot(a, b, preferred_element_type=F32)


def _dot_nt(a, b):
    return lax.dot_general(a, b, (((1,), (1,)), ((), ())), preferred_element_type=F32)


def _layer_norm(x, g, b):
    mu = jnp.mean(x, axis=-1, keepdims=True)
    xc = x - mu
    var = jnp.mean(xc * xc, axis=-1, keepdims=True)
    return xc * lax.rsqrt(var + LN_EPS) * g + b


def _sigmoid(x):
    return 1.0 / (1.0 + jnp.exp(-x))


def _log_sigmoid(x):
    return jnp.minimum(x, 0.0) - jnp.log(1.0 + jnp.exp(-jnp.abs(x)))


def _softplus(x):
    return jnp.maximum(x, 0.0) + jnp.log(1.0 + jnp.exp(-jnp.abs(x)))


def _split3(x):
    hi = x.astype(BF16)
    r = x - hi.astype(F32)
    mid = r.astype(BF16)
    lo = (r - mid.astype(F32)).astype(BF16)
    return hi, mid, lo


def _cumsum_lanes(x, upper):
    hi, mid, lo = _split3(x)
    return _dot(hi, upper) + _dot(mid, upper) + _dot(lo, upper)


def _ln_proj_kernel(x_ref, g_ref, b_ref, w_ref, ws_ref, h_ref, p_ref, s_ref, xn_ref):
    @pl.when(pl.program_id(1) == 0)
    def _():
        h = _layer_norm(x_ref[...], g_ref[...], b_ref[...])
        h_ref[...] = h
        hb = h.astype(BF16)
        xn_ref[...] = hb
        s = _dot(hb, ws_ref[...])
        for c in range(s_ref.shape[0]):
            s_ref[c] = s[c * CHUNK:(c + 1) * CHUNK, :].T

    p_ref[...] = _dot(xn_ref[...], w_ref[...]).astype(BF16)


def _ln_proj(x2, g, b, w_big, w_small, tm, tn):
    t, d = x2.shape
    nw = w_big.shape[1]
    return pl.pallas_call(
        _ln_proj_kernel,
        grid=(t // tm, nw // tn),
        in_specs=[
            pl.BlockSpec((tm, d), lambda i, j: (i, 0)),
            pl.BlockSpec((1, d), lambda i, j: (0, 0)),
            pl.BlockSpec((1, d), lambda i, j: (0, 0)),
            pl.BlockSpec((d, tn), lambda i, j: (0, j)),
            pl.BlockSpec((d, LANES), lambda i, j: (0, 0)),
        ],
        out_specs=[
            pl.BlockSpec((tm, d), lambda i, j: (i, 0)),
            pl.BlockSpec((tm, tn), lambda i, j: (i, j)),
            pl.BlockSpec((tm // CHUNK, LANES, CHUNK), lambda i, j: (i, 0, 0)),
        ],
        out_shape=[
            jax.ShapeDtypeStruct((t, d), F32),
            jax.ShapeDtypeStruct((t, nw), BF16),
            jax.ShapeDtypeStruct((t // CHUNK, LANES, CHUNK), F32),
        ],
        scratch_shapes=[pltpu.VMEM((tm, d), BF16)],
        compiler_params=_params("arbitrary", "arbitrary"),
        name="ln_in_proj",
    )(x2, g, b, w_big, w_small)


def _mlstm_kernel(q_ref, k_ref, v_ref, o_ref, gr_ref, br_ref, ng_ref,
                  out_ref, c_ref, m_ref, sqk_ref, kt_ref, p_ref, ktw_ref, b3_ref, li3_ref, kw3_ref,
                  gs_ref, mc_ref, stb_ref, stm_ref, std_ref, num_ref, kv_ref, *, dqk, dv, n_ch):
    L = CHUNK
    H = ML_HEADS

    @pl.when(pl.program_id(1) == 0)
    def _():
        c_ref[...] = jnp.zeros_like(c_ref)
        m_ref[...] = jnp.zeros_like(m_ref)

    ii = lax.broadcasted_iota(jnp.int32, (L, L), 0)
    jj = lax.broadcasted_iota(jnp.int32, (L, L), 1)
    causal = jj <= ii
    diag = ii == jj
    upper = (ii <= jj).astype(BF16)
    scale = dqk ** -0.5
    heads = [(g, h) for g in range(n_ch) for h in range(H)]

    for g, h in heads:
        rows = slice(g * L, (g + 1) * L)
        kf = k_ref[rows, h * dqk:(h + 1) * dqk].astype(F32) * scale
        sqk_ref[g * H + h] = _dot_nt(q_ref[rows, h * dqk:(h + 1) * dqk], kf.astype(BF16))
        kt_ref[g * H + h] = kf.T

    pre = gr_ref[:, 0:2 * H, :].reshape(n_ch * 2 * H, L) + jnp.concatenate([br_ref[...]] * n_ch, axis=0)
    b_all = _cumsum_lanes(_log_sigmoid(pre), upper)
    gs_all = b_all[:, L - 1:L]
    a_all = gs_all - b_all + pltpu.roll(pre, H, axis=0)
    mc_all = jnp.max(a_all, axis=1, keepdims=True)
    kw_all = jnp.exp(a_all - mc_all)
    gs_ref[...] = jnp.broadcast_to(gs_all, gs_ref.shape)
    mc_ref[...] = jnp.broadcast_to(mc_all, mc_ref.shape)
    for g, h in heads:
        r = g * 2 * H + H + h
        b3_ref[g * H + h] = b_all[r:r + 1, :]
        li3_ref[g * H + h] = pre[r - H:r - H + 1, :]
        kw3_ref[g * H + h] = kw_all[r:r + 1, :]

    b_r = b3_ref[...]
    b_c = jnp.sum(jnp.where(diag, b_r, 0.0), axis=2, keepdims=True)
    dmat = jnp.where(causal, b_c - b_r + li3_ref[...], -jnp.inf)
    m_intra = jnp.max(dmat, axis=2, keepdims=True)
    p = jnp.exp(dmat - m_intra) * sqk_ref[...]
    p_ref[...] = p.astype(BF16)
    stb_ref[...] = jnp.broadcast_to(b_c, stb_ref.shape)
    stm_ref[...] = jnp.broadcast_to(m_intra, stm_ref.shape)
    std_ref[...] = jnp.broadcast_to(jnp.sum(p, axis=2, keepdims=True), std_ref.shape)
    ktw_ref[...] = (kt_ref[...] * kw3_ref[...]).astype(BF16)

    ones = jnp.ones((L, LANES), BF16)
    for g, h in heads:
        vh = v_ref[g * L:(g + 1) * L, h * dv:(h + 1) * dv]
        num_ref[g * H + h] = _dot(p_ref[g * H + h], vh)
        kv_ref[g * H + h] = _dot(ktw_ref[g * H + h], jnp.concatenate([vh, ones], axis=1))

    rep = dv // LANES
    wide = lambda s: jnp.concatenate([s] * rep, axis=1)
    for g, h in heads:
        i = g * H + h
        r = g * 2 * H + H + h
        rows = slice(g * L, (g + 1) * L)
        c_st = c_ref[h]
        m_st = m_ref[h:h + 1, :]
        qc = _dot(q_ref[rows, h * dqk:(h + 1) * dqk], c_st.astype(BF16))
        m_intra = stm_ref[i]
        inter_log = stb_ref[i] + m_st
        m_out = jnp.maximum(inter_log, m_intra)
        s_inter = jnp.exp(inter_log - m_out)
        s_intra = jnp.exp(m_intra - m_out)
        num = wide(s_inter) * qc[:, 0:dv] + wide(s_intra) * num_ref[i]
        den = s_inter * qc[:, dv:dv + LANES] + s_intra * std_ref[i]
        hh = num / wide(jnp.maximum(jnp.abs(den), jnp.exp(-m_out)))
        gs = gs_ref[r:r + 1, :]
        mc = mc_ref[r:r + 1, :]
        m_new = jnp.maximum(gs + m_st, mc)
        dec = jnp.exp(gs + m_st - m_new)
        s_new = jnp.exp(mc - m_new)
        c_ref[h] = (jnp.concatenate([dec] * (rep + 1), axis=1) * c_st
                    + jnp.concatenate([s_new] * (rep + 1), axis=1) * kv_ref[i])
        m_ref[h:h + 1, :] = m_new

        rms = lax.rsqrt(jnp.mean(hh * hh, axis=1, keepdims=True) + RMS_EPS)
        og = _sigmoid(o_ref[rows, h * dv:(h + 1) * dv].astype(F32))
        out_ref[rows, h * dv:(h + 1) * dv] = (hh * rms * ng_ref[:, h * dv:(h + 1) * dv] * og).astype(BF16)


def _mlstm(proj, small_t, bias_r, norm_g, bsz, seq, cols, dqk, dv, n_ch):
    n = seq // (CHUNK * n_ch)
    L = CHUNK
    R = n_ch * L
    H = ML_HEADS
    qk_w, v_w = H * dqk, H * dv
    t = bsz * seq
    nh = n_ch * H
    kern = functools.partial(_mlstm_kernel, dqk=dqk, dv=dv, n_ch=n_ch)
    row = lambda b, c: b * n + c
    return pl.pallas_call(
        kern,
        grid=(bsz, n),
        in_specs=[
            pl.BlockSpec((R, qk_w), lambda b, c: (row(b, c), cols["mq"] // qk_w)),
            pl.BlockSpec((R, qk_w), lambda b, c: (row(b, c), cols["mk"] // qk_w)),
            pl.BlockSpec((R, v_w), lambda b, c: (row(b, c), cols["mv"] // v_w)),
            pl.BlockSpec((R, v_w), lambda b, c: (row(b, c), cols["mo"] // v_w)),
            pl.BlockSpec((n_ch, small_t.shape[1], L), lambda b, c: (row(b, c), 0, 0)),
            pl.BlockSpec(bias_r.shape, lambda b, c: (0, 0)),
            pl.BlockSpec((1, v_w), lambda b, c: (0, 0)),
        ],
        out_specs=pl.BlockSpec((R, v_w), lambda b, c: (row(b, c), 0)),
        out_shape=jax.ShapeDtypeStruct((t, v_w), BF16),
        scratch_shapes=[
            pltpu.VMEM((H, dqk, dv + LANES), F32),
            pltpu.VMEM((SUBLANES, LANES), F32),
            pltpu.VMEM((nh, L, L), F32),
            pltpu.VMEM((nh, dqk, L), F32),
            pltpu.VMEM((nh, L, L), BF16),
            pltpu.VMEM((nh, dqk, L), BF16),
            pltpu.VMEM((nh, 1, L), F32),
            pltpu.VMEM((nh, 1, L), F32),
            pltpu.VMEM((nh, 1, L), F32),
            pltpu.VMEM((n_ch * 2 * H, LANES), F32),
            pltpu.VMEM((n_ch * 2 * H, LANES), F32),
            pltpu.VMEM((nh, L, LANES), F32),
            pltpu.VMEM((nh, L, LANES), F32),
            pltpu.VMEM((nh, L, LANES), F32),
            pltpu.VMEM((nh, L, dv), F32),
            pltpu.VMEM((nh, dqk, dv + LANES), F32),
        ],
        compiler_params=_params("arbitrary", "arbitrary"),
        name="mlstm",
    )(proj, proj, proj, proj, small_t, bias_r, norm_g)


def _gdn_kernel(x_ref, z_ref, gr_ref, cw_ref, al_ref, dt_ref, ng_ref,
                out_ref, s_ref, xs_ref, c_ref, qn_ref, kn_ref, knt_ref, kk2_ref, qk2_ref, pw2_ref, x2_ref, rhs2_ref,
                attn2_ref, qd_ref, kdt_ref, gam2_ref, beta2_ref, kdwb_ref, gt_ref, sol_ref, vn_ref,
                *, n_qk, n_v, ga_off, gb_off, n_ch):
    L = CHUNK
    R = n_ch * L
    dk, dv = GDN_DK, GDN_DV
    qk_w = n_qk * dk
    rep = n_v // n_qk
    assert rep == 2 and 2 * L == LANES and dk == LANES and dv == LANES
    conv_ch = 2 * qk_w + n_v * dv
    sw = dv + dk

    @pl.when(pl.program_id(1) == 0)
    def _():
        s_ref[...] = jnp.zeros_like(s_ref)
        xs_ref[0:SUBLANES, :] = jnp.zeros((SUBLANES, xs_ref.shape[1]), F32)
        rhs2_ref[...] = jnp.zeros_like(rhs2_ref)

    T0 = SUBLANES
    xs_ref[T0:T0 + R, :] = x_ref[...].astype(F32)
    for cb in range(0, conv_ch, GDN_CONV_COLS):
        cs = slice(cb, cb + GDN_CONV_COLS)
        xa = xs_ref[0:T0 + R, cs]
        conv = cw_ref[CONV_K - 1:CONV_K, cs] * xa[T0:T0 + R]
        for sh in range(1, CONV_K):
            conv = conv + cw_ref[CONV_K - 1 - sh:CONV_K - sh, cs] * pltpu.roll(xa, sh, axis=0)[T0:T0 + R]
        c_ref[:, cs] = conv * _sigmoid(conv)
    xs_ref[0:T0, :] = xs_ref[R:R + T0, :]

    ii = lax.broadcasted_iota(jnp.int32, (L, L), 0)
    jj = lax.broadcasted_iota(jnp.int32, (L, L), 1)
    upper = (ii <= jj).astype(BF16)
    row2 = lax.broadcasted_iota(jnp.int32, (L, LANES), 0)
    lane2 = lax.broadcasted_iota(jnp.int32, (L, LANES), 1)
    left = lane2 < L
    col2 = jnp.where(left, lane2, lane2 - L)
    incl2 = col2 <= row2
    strict2 = col2 < row2
    diag2 = col2 == row2
    eye2 = diag2.astype(F32)

    for g in range(n_ch):
        rows = slice(g * L, (g + 1) * L)
        for hk in range(n_qk):
            iq = g * n_qk + hk
            cq = c_ref[rows, hk * dk:(hk + 1) * dk]
            ck = c_ref[rows, qk_w + hk * dk:qk_w + (hk + 1) * dk]
            qn = cq * lax.rsqrt(jnp.sum(cq * cq, axis=1, keepdims=True) + RMS_EPS) * (dk ** -0.5)
            kn = ck * lax.rsqrt(jnp.sum(ck * ck, axis=1, keepdims=True) + RMS_EPS)
            qn_ref[iq] = qn
            kn_ref[iq] = kn
            knt_ref[iq] = kn.T
            kb = kn.astype(BF16)
            kb2 = jnp.concatenate([kb, kb], axis=0)
            kk2_ref[iq] = _dot_nt(kb, kb2)
            qk2_ref[iq] = _dot_nt(qn.astype(BF16), kb2)

    for g in range(n_ch):
        gd = -jnp.exp(al_ref[...]) * _softplus(gr_ref[g, ga_off:ga_off + n_v, :] + dt_ref[...])
        gam = _cumsum_lanes(gd, upper)
        beta = _sigmoid(gr_ref[g, gb_off:gb_off + n_v, :])
        g_tot = gam[:, L - 1:L]
        kdwb = jnp.exp(g_tot - gam) * beta
        gt_ref[g * n_v:(g + 1) * n_v, :] = jnp.broadcast_to(jnp.exp(g_tot), (n_v, LANES))
        for hk in range(n_qk):
            iq = g * n_qk + hk
            gam2_ref[iq] = jnp.concatenate([gam[2 * hk:2 * hk + 1, :], gam[2 * hk + 1:2 * hk + 2, :]], axis=1)
            beta2_ref[iq] = jnp.concatenate([beta[2 * hk:2 * hk + 1, :], beta[2 * hk + 1:2 * hk + 2, :]], axis=1)
            for r in range(rep):
                kdwb_ref[iq, r] = kdwb[2 * hk + r:2 * hk + r + 1, :]

    for g in range(n_ch):
        sq = slice(g * n_qk, (g + 1) * n_qk)
        rows = slice(g * L, (g + 1) * L)
        gam_r = gam2_ref[sq]
        picked = jnp.where(diag2, gam_r, 0.0)
        gam_c0 = jnp.sum(jnp.where(left, picked, 0.0), axis=2, keepdims=True)
        gam_c1 = jnp.sum(jnp.where(left, 0.0, picked), axis=2, keepdims=True)
        gam_c = jnp.where(left, gam_c0, gam_c1)
        decm = jnp.exp(jnp.where(incl2, gam_c - gam_r, -jnp.inf))
        db = decm * beta2_ref[sq]
        a = jnp.where(strict2, kk2_ref[sq] * db, 0.0)
        x2_ref[sq] = eye2 - a
        pw2_ref[sq, 0:L, :] = jnp.where(left, a, 0.0).astype(BF16)
        pw2_ref[sq, L:2 * L, :] = jnp.where(left, 0.0, a).astype(BF16)
        attn2_ref[sq] = (qk2_ref[sq] * db).astype(BF16)
        kn = kn_ref[sq]
        qn = qn_ref[sq]
        for r, gc in enumerate((gam_c0, gam_c1)):
            eg = jnp.exp(gc)
            rhs2_ref[sq, r * L:(r + 1) * L, r * sw + dv:(r + 1) * sw] = (kn * eg).astype(BF16)
            qd_ref[sq, r] = (qn * eg).astype(BF16)
            kdt_ref[sq, r] = (knt_ref[sq] * kdwb_ref[sq, r]).astype(BF16)
        for hk in range(n_qk):
            for r in range(rep):
                h = 2 * hk + r
                rhs2_ref[g * n_qk + hk, r * L:(r + 1) * L, r * sw:r * sw + dv] = (
                    c_ref[rows, 2 * qk_w + h * dv:2 * qk_w + (h + 1) * dv].astype(BF16))

    n_sq = L.bit_length() - 2
    for lvl in range(n_sq + 1):
        for i in range(n_ch * n_qk):
            pwb = pw2_ref[i]
            if lvl > 0:
                xv = x2_ref[i]
                x2_ref[i] = xv + _dot(xv.astype(BF16), pwb)
            if lvl < n_sq:
                pw2_ref[i] = _dot(pwb, pwb).astype(BF16)

    for i in range(n_ch * n_qk):
        sol_ref[i] = _dot(x2_ref[i].astype(BF16), rhs2_ref[i])

    zeros = jnp.zeros((L, dv), BF16)
    for g in range(n_ch):
        rows = slice(g * L, (g + 1) * L)
        for h in range(n_v):
            sol = sol_ref[g * n_qk + h // rep][:, (h % rep) * sw:(h % rep + 1) * sw]
            vn_ref[h] = (sol[:, 0:dv] - _dot(sol[:, dv:sw].astype(BF16), s_ref[h].astype(BF16))).astype(BF16)
        for hk in range(n_qk):
            iq = g * n_qk + hk
            h0, h1 = 2 * hk, 2 * hk + 1
            v0, v1 = vn_ref[h0], vn_ref[h1]
            vbd = jnp.concatenate([jnp.concatenate([v0, zeros], axis=1), jnp.concatenate([zeros, v1], axis=1)], axis=0)
            s0, s1 = s_ref[h0], s_ref[h1]
            o2 = _dot(attn2_ref[iq], vbd) + jnp.concatenate(
                [_dot(qd_ref[iq, 0], s0.astype(BF16)), _dot(qd_ref[iq, 1], s1.astype(BF16))], axis=1)
            s_ref[h0] = gt_ref[g * n_v + h0:g * n_v + h0 + 1, :] * s0 + _dot(kdt_ref[iq, 0], v0)
            s_ref[h1] = gt_ref[g * n_v + h1:g * n_v + h1 + 1, :] * s1 + _dot(kdt_ref[iq, 1], v1)
            for r, h in enumerate((h0, h1)):
                o = o2[:, r * dv:(r + 1) * dv]
                rms = lax.rsqrt(jnp.mean(o * o, axis=1, keepdims=True) + RMS_EPS)
                zz = z_ref[rows, h * dv:(h + 1) * dv].astype(F32)
                out_ref[rows, h * dv:(h + 1) * dv] = (o * rms * ng_ref[...] * (zz * _sigmoid(zz))).astype(BF16)


def _gdn(proj, small_t, conv_w, al, dt, norm_g, bsz, seq, cols, n_qk, n_v, ga_off, gb_off, n_ch):
    n = seq // (CHUNK * n_ch)
    L = CHUNK
    R = n_ch * L
    conv_ch = 2 * n_qk * GDN_DK + n_v * GDN_DV
    v_w = n_v * GDN_DV
    t = bsz * seq
    nq, nv = n_ch * n_qk, n_ch * n_v
    kern = functools.partial(_gdn_kernel, n_qk=n_qk, n_v=n_v, ga_off=ga_off, gb_off=gb_off, n_ch=n_ch)
    row = lambda b, c: b * n + c
    full = lambda a: pl.BlockSpec(a.shape, lambda b, c: (0,) * a.ndim)
    return pl.pallas_call(
        kern,
        grid=(bsz, n),
        in_specs=[
            pl.BlockSpec((R, conv_ch), lambda b, c: (row(b, c), cols["gqkv"] // conv_ch)),
            pl.BlockSpec((R, v_w), lambda b, c: (row(b, c), cols["gz"] // v_w)),
            pl.BlockSpec((n_ch, small_t.shape[1], L), lambda b, c: (row(b, c), 0, 0)),
            full(conv_w), full(al), full(dt), full(norm_g),
        ],
        out_specs=pl.BlockSpec((R, v_w), lambda b, c: (row(b, c), 0)),
        out_shape=jax.ShapeDtypeStruct((t, v_w), BF16),
        scratch_shapes=[
            pltpu.VMEM((n_v, GDN_DK, GDN_DV), F32),
            pltpu.VMEM((SUBLANES + R, conv_ch), F32),
            pltpu.VMEM((R, conv_ch), F32),
            pltpu.VMEM((nq, L, GDN_DK), F32),
            pltpu.VMEM((nq, L, GDN_DK), F32),
            pltpu.VMEM((nq, GDN_DK, L), F32),
            pltpu.VMEM((nq, L, 2 * L), F32),
            pltpu.VMEM((nq, L, 2 * L), F32),
            pltpu.VMEM((nq, 2 * L, 2 * L), BF16),
            pltpu.VMEM((nq, L, 2 * L), F32),
            pltpu.VMEM((nq, 2 * L, 2 * (GDN_DV + GDN_DK)), BF16),
            pltpu.VMEM((nq, L, 2 * L), BF16),
            pltpu.VMEM((nq, 2, L, GDN_DK), BF16),
            pltpu.VMEM((nq, 2, GDN_DK, L), BF16),
            pltpu.VMEM((nq, 1, 2 * L), F32),
            pltpu.VMEM((nq, 1, 2 * L), F32),
            pltpu.VMEM((nq, 2, 1, L), F32),
            pltpu.VMEM((nv, LANES), F32),
            pltpu.VMEM((nq, L, 2 * (GDN_DV + GDN_DK)), F32),
            pltpu.VMEM((n_v, L, GDN_DV), BF16),
        ],
        compiler_params=_params("arbitrary", "arbitrary"),
        name="gdn",
    )(proj, proj, small_t, conv_w, al, dt, norm_g)


def _kv_kernel(m_ref, wk_ref, wv_ref, k_ref, v_ref):
    mb = m_ref[...].astype(BF16)
    k_ref[...] = _dot(mb, wk_ref[...]).astype(BF16)
    v_ref[...] = _dot(mb, wv_ref[...]).astype(BF16)


def _kv_proj(mem2, wk, wv, tm):
    t, d = mem2.shape
    full = lambda a: pl.BlockSpec(a.shape, lambda i: (0,) * a.ndim)
    return pl.pallas_call(
        _kv_kernel,
        grid=(t // tm,),
        in_specs=[pl.BlockSpec((tm, d), lambda i: (i, 0)), full(wk), full(wv)],
        out_specs=[pl.BlockSpec((tm, d), lambda i: (i, 0))] * 2,
        out_shape=[jax.ShapeDtypeStruct((t, d), BF16)] * 2,
        compiler_params=_params("arbitrary"),
        name="mem_kv_proj",
    )(mem2, wk, wv)


def _mix_xattn_kernel(hm_ref, og_ref, gm_ref, gg_ref, h0_ref, wbm_ref, wbg_ref, wmo_ref, g1_ref, b1_ref,
                      k_ref, v_ref, wq_ref, wo_ref, g_ref, b_ref, wr_ref, br_ref,
                      h2_ref, gw_ref, route_ref, cnt_ref, carry_ref, *, dh):
    tm = h0_ref.shape[0]

    @pl.when((pl.program_id(0) == 0) & (pl.program_id(1) == 0))
    def _():
        carry_ref[...] = jnp.zeros_like(carry_ref)

    n_sub = MIX_SPLIT
    ts = tm // n_sub
    subs = [slice(u * ts, (u + 1) * ts) for u in range(n_sub)]

    mix = []
    for r in subs:
        y_ml = _dot(hm_ref[r, :], wbm_ref[...])
        y_gdn = _dot(og_ref[r, :], wbg_ref[...])
        merged = _sigmoid(gm_ref[r, :].astype(F32)) * y_ml + _sigmoid(gg_ref[r, :].astype(F32)) * y_gdn
        mix.append(_dot(merged.astype(BF16), wmo_ref[...]))
    h1 = [_layer_norm(DN_ALPHA * h0_ref[r, :] + m, g1_ref[...], b1_ref[...]) for r, m in zip(subs, mix)]
    q = [_dot(x.astype(BF16), wq_ref[...]) for x in h1]
    o = []
    for qq in q:
        outs = []
        for hd in range(XA_HEADS):
            qh = qq[:, hd * dh:(hd + 1) * dh].astype(BF16)
            kh = k_ref[:, hd * dh:(hd + 1) * dh]
            vh = v_ref[:, hd * dh:(hd + 1) * dh]
            sc = _dot_nt(qh, kh) * (dh ** -0.5)
            e = jnp.exp(sc - jnp.max(sc, axis=1, keepdims=True))
            p = e / jnp.sum(e, axis=1, keepdims=True)
            outs.append(_dot(p.astype(BF16), vh))
        o.append(jnp.concatenate(outs, axis=1))
    xa = [_dot(x.astype(BF16), wo_ref[...]) for x in o]
    h2 = [_layer_norm(DN_ALPHA * x + y, g_ref[...], b_ref[...]) for x, y in zip(h1, xa)]
    for r, x in zip(subs, h2):
        h2_ref[r, :] = x

    lane = lax.broadcasted_iota(jnp.int32, (ts, LANES), 1)
    lane_f = lane.astype(F32)
    ri = lax.broadcasted_iota(jnp.int32, (ts, ts), 0)
    ci = lax.broadcasted_iota(jnp.int32, (ts, ts), 1)
    tri = (ci < ri).astype(BF16)
    logits = [_dot(x.astype(BF16), wr_ref[...]) + br_ref[...] for x in h2]
    picks = []
    for work in logits:
        vals, idxs = [], []
        for _ in range(TOP_K):
            m = jnp.max(work, axis=1, keepdims=True)
            idx = jnp.min(jnp.where(work == m, lane_f, float(LANES)), axis=1, keepdims=True)
            vals.append(m)
            idxs.append(idx)
            work = jnp.where(lane_f == idx, -jnp.inf, work)
        picks.append((vals, idxs))
    carry = carry_ref[0:1, :]
    for r, (vals, idxs) in zip(subs, picks):
        es = [jnp.exp(v - vals[0]) for v in vals]
        tot = es[0]
        for e_ in es[1:]:
            tot = tot + e_
        onehot = jnp.zeros((ts, LANES), F32)
        for idx in idxs:
            onehot = onehot + (lane_f == idx).astype(F32)
        ranks = carry + _dot(tri, onehot.astype(BF16))
        gw = jnp.zeros((ts, LANES), F32)
        route = jnp.zeros((ts, LANES), F32)
        for k_ in range(TOP_K):
            rk = jnp.sum(jnp.where(lane_f == idxs[k_], ranks, 0.0), axis=1, keepdims=True)
            gw = gw + jnp.where(lane == k_, es[k_] / tot, 0.0)
            route = route + jnp.where(lane == k_, idxs[k_], 0.0) + jnp.where(lane == TOP_K + k_, rk, 0.0)
        gw_ref[r, :] = gw
        route_ref[r, :] = route.astype(jnp.int32)
        carry = carry + jnp.sum(onehot, axis=0, keepdims=True)
    carry_ref[...] = jnp.broadcast_to(carry, carry_ref.shape)
    cnt_ref[...] = jnp.broadcast_to(carry, cnt_ref.shape)


def _mix_xattn(hm, og, proj, h0, w_bm, w_bg, w_mo, g1, b1, kmem, vmem, wq, wo, g, b, w_r, b_r, cols,
               bsz, seq, mem_len, tm):
    t, d = h0.shape
    nt = seq // tm
    once = lambda a: pl.BlockSpec(a.shape, lambda i, j: (0,) * a.ndim, pipeline_mode=pl.Buffered(1))
    rowmap = lambda i, j: (i * nt + j, 0)
    kern = functools.partial(_mix_xattn_kernel, dh=d // XA_HEADS)
    return pl.pallas_call(
        kern,
        grid=(bsz, nt),
        in_specs=[
            pl.BlockSpec((tm, hm.shape[1]), rowmap),
            pl.BlockSpec((tm, og.shape[1]), rowmap),
            pl.BlockSpec((tm, d), lambda i, j: (i * nt + j, cols["gate_ml"] // d)),
            pl.BlockSpec((tm, d), lambda i, j: (i * nt + j, cols["gate_gdn"] // d)),
            pl.BlockSpec((tm, d), rowmap),
            once(w_bm), once(w_bg), once(w_mo), once(g1), once(b1),
            pl.BlockSpec((mem_len, d), lambda i, j: (i, 0)),
            pl.BlockSpec((mem_len, d), lambda i, j: (i, 0)),
            once(wq), once(wo), once(g), once(b), once(w_r), once(b_r),
        ],
        out_specs=[
            pl.BlockSpec((tm, d), rowmap),
            pl.BlockSpec((tm, LANES), rowmap),
            pl.BlockSpec((tm, LANES), rowmap),
            pl.BlockSpec((SUBLANES, LANES), lambda i, j: (0, 0)),
        ],
        out_shape=[
            jax.ShapeDtypeStruct((t, d), F32),
            jax.ShapeDtypeStruct((t, LANES), F32),
            jax.ShapeDtypeStruct((t, LANES), jnp.int32),
            jax.ShapeDtypeStruct((SUBLANES, LANES), F32),
        ],
        scratch_shapes=[pltpu.VMEM((SUBLANES, LANES), F32)],
        compiler_params=_params("arbitrary", "arbitrary"),
        name="mix_xattn_ln_router",
    )(hm, og, proj, proj, h0, w_bm, w_bg, w_mo, g1, b1, kmem, vmem, wq, wo, g, b, w_r, b_r)


def _dispatch_kernel(pad_lo_ref, pad_n_ref, na_ref, dest_ref, h_ref, xs_ref, zero_ref, sem, zsem):
    tm = h_ref.shape[0]
    step = pl.program_id(0)
    n_blocks = xs_ref.shape[0] // MOE_BLOCK

    @pl.when(step == 0)
    def _():
        zero_ref[...] = jnp.zeros_like(zero_ref)

        def per_expert(e, carry):
            n = pad_n_ref[e]
            pos = pad_lo_ref[e] + n

            def pieces(wait):
                def piece(start, p):
                    cp = pltpu.make_async_copy(zero_ref.at[pl.ds(0, p), :], xs_ref.at[pl.ds(start, p), :], zsem)
                    cp.wait() if wait else cp.start()

                at = pos
                p = MOE_BLOCK // 2
                while p >= SUBLANES:

                    @pl.when((n & p) != 0)
                    def _(p=p, at=at):
                        piece(pl.multiple_of(at - p, SUBLANES), p)

                    at = at - (n & p)
                    p //= 2
                for r in range(SUBLANES - 1):

                    @pl.when(r < (n & (SUBLANES - 1)))
                    def _(r=r, at=at):
                        piece(at - 1 - r, 1)

            pieces(wait=False)
            pieces(wait=True)
            return carry

        lax.fori_loop(0, N_EXPERTS, per_expert, 0)

        def tail(b, carry):
            half = MOE_BLOCK // 2
            cp = pltpu.make_async_copy(zero_ref, xs_ref.at[pl.ds(b * half, half), :], zsem)
            cp.start()
            cp.wait()
            return carry

        lax.fori_loop(2 * na_ref[0], 2 * n_blocks, tail, 0)

    for tok in range(tm):
        src = h_ref.at[pl.ds(tok, 1), :]
        for k_ in range(TOP_K):
            pltpu.make_async_copy(src, xs_ref.at[pl.ds(dest_ref[tok * TOP_K + k_], 1), :], sem).start(priority=k_ % 2)
    for _ in range(TOP_K):
        pltpu.make_async_copy(h_ref, xs_ref.at[pl.ds(0, tm), :], sem).wait()


def _dispatch(pad_lo, pad_n, nact, dest_flat, h2, n_rows, tm):
    t, d = h2.shape
    grid_spec = pltpu.PrefetchScalarGridSpec(
        num_scalar_prefetch=3,
        grid=(t // tm,),
        in_specs=[
            pl.BlockSpec((tm * TOP_K,), lambda i, *_: (i,), memory_space=pltpu.SMEM),
            pl.BlockSpec((tm, d), lambda i, *_: (i, 0)),
        ],
        out_specs=pl.BlockSpec(memory_space=pl.ANY),
        scratch_shapes=[pltpu.VMEM((MOE_BLOCK // 2, d), F32), pltpu.SemaphoreType.DMA(()),
                        pltpu.SemaphoreType.DMA(())],
    )
    return pl.pallas_call(
        _dispatch_kernel,
        grid_spec=grid_spec,
        out_shape=jax.ShapeDtypeStruct((n_rows, d), F32),
        compiler_params=_params("arbitrary"),
        name="moe_dispatch",
    )(pad_lo, pad_n, nact, dest_flat, h2)


def _expert_kernel(be_ref, na_ref, x_ref, wgu_ref, bgu_ref, wdn_ref, bdn_ref, y_ref, wgu_s, wdn_s, *, d_exp):
    blk = pl.program_id(0)
    active = blk < na_ref[0]

    @pl.when(jnp.logical_not(active))
    def _():
        y_ref[...] = jnp.zeros_like(y_ref)

    @pl.when(active & ((blk == 0) | (be_ref[blk] != be_ref[jnp.maximum(blk - 1, 0)])))
    def _():
        wgu_s[...] = wgu_ref[0].astype(BF16)
        wdn_s[...] = wdn_ref[0].astype(BF16)

    @pl.when(active)
    def _():
        gu = _dot(x_ref[...].astype(BF16), wgu_s[...]) + bgu_ref[0]
        gate = jnp.minimum(gu[:, :d_exp], SWIGLU_LIMIT)
        up = jnp.clip(gu[:, d_exp:], -SWIGLU_LIMIT, SWIGLU_LIMIT)
        act = (up + 1.0) * (gate * _sigmoid(SWIGLU_ALPHA * gate))
        y_ref[...] = _dot(act.astype(BF16), wdn_s[...]) + bdn_ref[0]


def _experts(block_e, nact, xs, w_gu, b_gu, w_dn, b_dn):
    n_rows, d = xs.shape
    n_blocks = n_rows // MOE_BLOCK
    d_exp = w_dn.shape[1]
    kern = functools.partial(_expert_kernel, d_exp=d_exp)
    emap = lambda i, be, na: (be[i], 0, 0)
    grid_spec = pltpu.PrefetchScalarGridSpec(
        num_scalar_prefetch=2,
        grid=(n_blocks,),
        in_specs=[
            pl.BlockSpec((MOE_BLOCK, d), lambda i, be, na: (i, 0)),
            pl.BlockSpec((1, d, 2 * d_exp), emap),
            pl.BlockSpec((1, 1, 2 * d_exp), emap),
            pl.BlockSpec((1, d_exp, d), emap),
            pl.BlockSpec((1, 1, d), emap),
        ],
        out_specs=pl.BlockSpec((MOE_BLOCK, d), lambda i, be, na: (i, 0)),
        scratch_shapes=[pltpu.VMEM((d, 2 * d_exp), BF16), pltpu.VMEM((d_exp, d), BF16)],
    )
    return pl.pallas_call(
        kern,
        grid_spec=grid_spec,
        out_shape=jax.ShapeDtypeStruct((n_rows, d), F32),
        compiler_params=_params("arbitrary"),
        name="moe_experts",
    )(block_e, nact, xs, w_gu, b_gu, w_dn, b_dn)


def _combine_kernel(dest_ref, dest_nxt_ref, gw_ref, h_ref, g_ref, b_ref, y_ref, out_ref, buf_a, buf_b, sem):
    tm = buf_a.shape[1]
    step = pl.program_id(0)
    n_steps = pl.num_programs(0)

    def issue(idx_ref, half, buf, s):
        for tok in range(tm):
            for k_ in range(TOP_K):
                pltpu.make_async_copy(y_ref.at[pl.ds(idx_ref[(half * tm + tok) * TOP_K + k_], 1), :],
                                      buf.at[k_, pl.ds(tok, 1), :], sem.at[s]).start(priority=k_ % 2)

    def consume(half, buf, s):
        for k_ in range(TOP_K):
            pltpu.make_async_copy(y_ref.at[pl.ds(0, tm), :], buf.at[k_], sem.at[s]).wait()
        rows = slice(half * tm, (half + 1) * tm)
        gw = gw_ref[rows, :]
        ff = buf[0] * gw[:, 0:1]
        for k_ in range(1, TOP_K):
            ff = ff + buf[k_] * gw[:, k_:k_ + 1]
        out_ref[rows, :] = _layer_norm(DN_ALPHA * h_ref[rows, :] + ff, g_ref[...], b_ref[...])

    @pl.when(step == 0)
    def _():
        issue(dest_ref, 0, buf_a, 0)

    issue(dest_ref, 1, buf_b, 1)
    consume(0, buf_a, 0)

    @pl.when(step + 1 < n_steps)
    def _():
        issue(dest_nxt_ref, 0, buf_a, 0)

    consume(1, buf_b, 1)


def _combine(dest_flat, gw, h2, g, b, yb, tm):
    t, d = h2.shape
    n = t // (2 * tm)
    return pl.pallas_call(
        _combine_kernel,
        grid=(n,),
        in_specs=[
            pl.BlockSpec((2 * tm * TOP_K,), lambda i: (i,), memory_space=pltpu.SMEM),
            pl.BlockSpec((2 * tm * TOP_K,), lambda i: (jnp.minimum(i + 1, n - 1),), memory_space=pltpu.SMEM),
            pl.BlockSpec((2 * tm, LANES), lambda i: (i, 0)),
            pl.BlockSpec((2 * tm, d), lambda i: (i, 0)),
            pl.BlockSpec((1, d), lambda i: (0, 0)),
            pl.BlockSpec((1, d), lambda i: (0, 0)),
            pl.BlockSpec(memory_space=pl.ANY),
        ],
        out_specs=pl.BlockSpec((2 * tm, d), lambda i: (i, 0)),
        out_shape=jax.ShapeDtypeStruct((t, d), F32),
        scratch_shapes=[pltpu.VMEM((TOP_K, tm, d), F32), pltpu.VMEM((TOP_K, tm, d), F32),
                        pltpu.SemaphoreType.DMA((2,))],
        compiler_params=_params("arbitrary"),
        name="moe_combine_ln3",
    )(dest_flat, dest_flat, gw, h2, g, b, yb)


def _pick(n, pref):
    return pref if n % pref == 0 else n


def kernel(x, mem, ln_in_g, ln_in_b, w_in, ml_gate_bias, ml_norm_g, gdn_conv_w, gdn_a_log, gdn_dt_bias, gdn_norm_g, w_branch_ml, w_branch_gdn, w_mix_out, ln1_g, ln1_b, xa_wq, xa_wk, xa_wv, xa_wo, ln2_g, ln2_b, w_router, b_router, w_gu, b_gu, w_dn, b_dn, ln3_g, ln3_b):
    bsz, seq, d = x.shape
    mem_len = mem.shape[1]
    t = bsz * seq
    ml_dv = d // ML_HEADS
    ml_dqk = ml_dv // 2
    ml_qk_w, ml_v_w = ML_HEADS * ml_dqk, ML_HEADS * ml_dv
    n_qk = d // GDN_DK
    n_v = 2 * n_qk
    gdn_qk_w, gdn_v_w = n_qk * GDN_DK, n_v * GDN_DV
    conv_ch = 2 * gdn_qk_w + gdn_v_w
    splits = (ml_qk_w, ml_qk_w, ml_v_w, ml_v_w, 2 * ML_HEADS, conv_ch, gdn_v_w, n_v, n_v, d, d)
    names = ("mq", "mk", "mv", "mo", "mif", "gqkv", "gz", "ga", "gb", "gate_ml", "gate_gdn")
    starts = {}
    acc = 0
    for nm, sz in zip(names, splits):
        starts[nm] = (acc, sz)
        acc += sz
    row2 = lambda a: a.reshape(1, -1).astype(F32)

    h = x.reshape(t, d)
    for l in range(DEPTH):
        w = w_in[l]
        seg = lambda nm: w[:, starts[nm][0]:starts[nm][0] + starts[nm][1]]
        order = ("gqkv", "gz", "mv", "mo", "gate_ml", "gate_gdn", "mq", "mk")
        cols = {}
        off = 0
        for nm in order:
            assert off % starts[nm][1] == 0
            cols[nm] = off
            off += starts[nm][1]
        w_big = jnp.concatenate([seg(nm) for nm in order], axis=1).astype(BF16)
        n_small = 2 * ML_HEADS + 2 * n_v
        w_small = jnp.concatenate([seg("mif"), seg("ga"), seg("gb"),
                                   jnp.zeros((d, LANES - n_small), F32)], axis=1).astype(BF16)
        ga_off, gb_off = 2 * ML_HEADS, 2 * ML_HEADS + n_v

        if l == 0:
            h0, proj, small_t = _ln_proj(h, row2(ln_in_g), row2(ln_in_b), w_big, w_small,
                                         _pick(t, PROJ_ROWS), _pick(off, PROJ_COLS))
        else:
            raise NotImplementedError("DEPTH > 1")

        bias = ml_gate_bias[l].astype(F32)
        hm = _mlstm(proj, small_t, bias.reshape(-1, 1), row2(ml_norm_g[l]),
                    bsz, seq, cols, ml_dqk, ml_dv, ML_CHUNKS_PER_STEP)
        al = gdn_a_log[l].astype(F32)
        dt = gdn_dt_bias[l].astype(F32)
        og = _gdn(proj, small_t, gdn_conv_w[l].astype(F32), al.reshape(-1, 1), dt.reshape(-1, 1),
                  row2(gdn_norm_g[l]), bsz, seq, cols, n_qk, n_v, ga_off, gb_off, GDN_CHUNKS_PER_STEP)
        kmem, vmem = _kv_proj(mem.reshape(bsz * mem_len, d), xa_wk[l].astype(BF16), xa_wv[l].astype(BF16),
                              _pick(bsz * mem_len, KV_ROWS))
        w_r = jnp.concatenate([w_router[l], jnp.zeros((d, LANES - N_EXPERTS), F32)], axis=1).astype(BF16)
        b_r = jnp.concatenate([b_router[l].astype(F32), jnp.full((LANES - N_EXPERTS,), NEG_BIG, F32)]).reshape(1, -1)
        h2, gw, route, cnt = _mix_xattn(hm, og, proj, h0, w_branch_ml[l].astype(BF16), w_branch_gdn[l].astype(BF16),
                                        w_mix_out[l].astype(BF16), row2(ln1_g[l]), row2(ln1_b[l]), kmem, vmem,
                                        xa_wq[l].astype(BF16), xa_wo[l].astype(BF16), row2(ln2_g[l]), row2(ln2_b[l]),
                                        w_r, b_r, cols, bsz, seq, mem_len, _pick(seq, MIX_ROWS))

        counts = cnt[0, :N_EXPERTS].astype(jnp.int32)
        padded = (counts + MOE_BLOCK - 1) // MOE_BLOCK * MOE_BLOCK
        pad_end = jnp.cumsum(padded)
        pad_start = pad_end - padded
        n_asg = t * TOP_K
        n_blocks = -(-n_asg // MOE_BLOCK) + N_EXPERTS
        n_rows = n_blocks * MOE_BLOCK
        top_e = route[:, :TOP_K]
        rank = route[:, TOP_K:2 * TOP_K]
        sel = top_e[:, :, None] == jnp.arange(N_EXPERTS, dtype=jnp.int32)[None, None, :]
        dest = (jnp.sum(jnp.where(sel, pad_start[None, None, :], 0), axis=-1) + rank).astype(jnp.int32)
        dest_flat = dest.reshape(n_asg)
        blk_row = jnp.arange(n_blocks, dtype=jnp.int32) * MOE_BLOCK
        block_e = jnp.minimum(jnp.sum(pad_end[None, :] <= blk_row[:, None], axis=1), N_EXPERTS - 1).astype(jnp.int32)
        nact = (pad_end[-1:] // MOE_BLOCK).astype(jnp.int32)
        pad_lo = (pad_start + counts).astype(jnp.int32)
        pad_n = (padded - counts).astype(jnp.int32)

        xs = _dispatch(pad_lo, pad_n, nact, dest_flat, h2, n_rows, _pick(t, DISPATCH_ROWS))
        yb = _experts(block_e, nact, xs, w_gu[l], b_gu[l].astype(F32)[:, None, :],
                      w_dn[l], b_dn[l].astype(F32)[:, None, :])
        h = _combine(dest_flat, gw, h2, row2(ln3_g[l]), row2(ln3_b[l]), yb, _pick(t // 2, COMBINE_ROWS))
    return h.reshape(bsz, seq, d)
```

```python
import functools

import jax
import jax.numpy as jnp
from jax import lax
from jax.experimental import pallas as pl
from jax.experimental.pallas import tpu as pltpu

F32 = jnp.float32
BF16 = jnp.bfloat16

CHUNK = 64
ML_HEADS = 4
GDN_DK = 128
GDN_DV = 128
CONV_K = 4
XA_HEADS = 4
N_EXPERTS = 32
TOP_K = 4
SWIGLU_LIMIT = 7.0
SWIGLU_ALPHA = 1.702
MOE_BLOCK = 512
DEPTH = 1
DN_ALPHA = (2 * DEPTH) ** 0.25
LN_EPS = 1e-5
RMS_EPS = 1e-6
LANES = 128
SUBLANES = 8
VMEM_LIMIT = 56 * 1024 * 1024
NEG_BIG = -1e30
ML_CHUNKS_PER_STEP = 8
GDN_CHUNKS_PER_STEP = 4
GDN_CONV_COLS = 512
PROJ_ROWS, PROJ_COLS = 1024, 2816
KV_ROWS = 512
MIX_ROWS = 512
MIX_SPLIT = 2
DISPATCH_ROWS = 512
COMBINE_ROWS = 256


def _params(*sem):
    return pltpu.CompilerParams(dimension_semantics=sem, vmem_limit_bytes=VMEM_LIMIT)


def _dot(a, b):
    return jnp.dot(a, b, preferred_element_type=F32)


def _dot_nt(a, b):
    return lax.dot_general(a, b, (((1,), (1,)), ((), ())), preferred_element_type=F32)


def _layer_norm(x, g, b):
    mu = jnp.mean(x, axis=-1, keepdims=True)
    xc = x - mu
    var = jnp.mean(xc * xc, axis=-1, keepdims=True)
    return xc * lax.rsqrt(var + LN_EPS) * g + b


def _sigmoid(x):
    return 1.0 / (1.0 + jnp.exp(-x))


def _log_sigmoid(x):
    return jnp.minimum(x, 0.0) - jnp.log(1.0 + jnp.exp(-jnp.abs(x)))


def _softplus(x):
    return jnp.maximum(x, 0.0) + jnp.log(1.0 + jnp.exp(-jnp.abs(x)))


def _split3(x):
    hi = x.astype(BF16)
    r = x - hi.astype(F32)
    mid = r.astype(BF16)
    lo = (r - mid.astype(F32)).astype(BF16)
    return hi, mid, lo


def _cumsum_lanes(x, upper):
    hi, mid, lo = _split3(x)
    return _dot(hi, upper) + _dot(mid, upper) + _dot(lo, upper)


def _ln_proj_kernel(x_ref, g_ref, b_ref, w_ref, ws_ref, h_ref, p_ref, s_ref, xn_ref):
    @pl.when(pl.program_id(1) == 0)
    def _():
        h = _layer_norm(x_ref[...], g_ref[...], b_ref[...])
        h_ref[...] = h
        hb = h.astype(BF16)
        xn_ref[...] = hb
        s = _dot(hb, ws_ref[...])
        for c in range(s_ref.shape[0]):
            s_ref[c] = s[c * CHUNK:(c + 1) * CHUNK, :].T

    p_ref[...] = _dot(xn_ref[...], w_ref[...]).astype(BF16)


def _ln_proj(x2, g, b, w_big, w_small, tm, tn):
    t, d = x2.shape
    nw = w_big.shape[1]
    return pl.pallas_call(
        _ln_proj_kernel,
        grid=(t // tm, nw // tn),
        in_specs=[
            pl.BlockSpec((tm, d), lambda i, j: (i, 0)),
            pl.BlockSpec((1, d), lambda i, j: (0, 0)),
            pl.BlockSpec((1, d), lambda i, j: (0, 0)),
            pl.BlockSpec((d, tn), lambda i, j: (0, j)),
            pl.BlockSpec((d, LANES), lambda i, j: (0, 0)),
        ],
        out_specs=[
            pl.BlockSpec((tm, d), lambda i, j: (i, 0)),
            pl.BlockSpec((tm, tn), lambda i, j: (i, j)),
            pl.BlockSpec((tm // CHUNK, LANES, CHUNK), lambda i, j: (i, 0, 0)),
        ],
        out_shape=[
            jax.ShapeDtypeStruct((t, d), F32),
            jax.ShapeDtypeStruct((t, nw), BF16),
            jax.ShapeDtypeStruct((t // CHUNK, LANES, CHUNK), F32),
        ],
        scratch_shapes=[pltpu.VMEM((tm, d), BF16)],
        compiler_params=_params("arbitrary", "arbitrary"),
        name="ln_in_proj",
    )(x2, g, b, w_big, w_small)


def _mlstm_kernel(q_ref, k_ref, v_ref, o_ref, gr_ref, br_ref, ng_ref,
                  out_ref, c_ref, m_ref, sqk_ref, kt_ref, p_ref, ktw_ref, b3_ref, li3_ref, kw3_ref,
                  gs_ref, mc_ref, stb_ref, stm_ref, std_ref, num_ref, kv_ref, *, dqk, dv, n_ch):
    L = CHUNK
    H = ML_HEADS

    @pl.when(pl.program_id(1) == 0)
    def _():
        c_ref[...] = jnp.zeros_like(c_ref)
        m_ref[...] = jnp.zeros_like(m_ref)

    ii = lax.broadcasted_iota(jnp.int32, (L, L), 0)
    jj = lax.broadcasted_iota(jnp.int32, (L, L), 1)
    causal = jj <= ii
    diag = ii == jj
    upper = (ii <= jj).astype(BF16)
    scale = dqk ** -0.5
    heads = [(g, h) for g in range(n_ch) for h in range(H)]

    for g, h in heads:
        rows = slice(g * L, (g + 1) * L)
        kf = k_ref[rows, h * dqk:(h + 1) * dqk].astype(F32) * scale
        sqk_ref[g * H + h] = _dot_nt(q_ref[rows, h * dqk:(h + 1) * dqk], kf.astype(BF16))
        kt_ref[g * H + h] = kf.T

    pre = gr_ref[:, 0:2 * H, :].reshape(n_ch * 2 * H, L) + jnp.concatenate([br_ref[...]] * n_ch, axis=0)
    b_all = _cumsum_lanes(_log_sigmoid(pre), upper)
    gs_all = b_all[:, L - 1:L]
    a_all = gs_all - b_all + pltpu.roll(pre, H, axis=0)
    mc_all = jnp.max(a_all, axis=1, keepdims=True)
    kw_all = jnp.exp(a_all - mc_all)
    gs_ref[...] = jnp.broadcast_to(gs_all, gs_ref.shape)
    mc_ref[...] = jnp.broadcast_to(mc_all, mc_ref.shape)
    for g, h in heads:
        r = g * 2 * H + H + h
        b3_ref[g * H + h] = b_all[r:r + 1, :]
        li3_ref[g * H + h] = pre[r - H:r - H + 1, :]
        kw3_ref[g * H + h] = kw_all[r:r + 1, :]

    b_r = b3_ref[...]
    b_c = jnp.sum(jnp.where(diag, b_r, 0.0), axis=2, keepdims=True)
    dmat = jnp.where(causal, b_c - b_r + li3_ref[...], -jnp.inf)
    m_intra = jnp.max(dmat, axis=2, keepdims=True)
    p = jnp.exp(dmat - m_intra) * sqk_ref[...]
    p_ref[...] = p.astype(BF16)
    stb_ref[...] = jnp.broadcast_to(b_c, stb_ref.shape)
    stm_ref[...] = jnp.broadcast_to(m_intra, stm_ref.shape)
    std_ref[...] = jnp.broadcast_to(jnp.sum(p, axis=2, keepdims=True), std_ref.shape)
    ktw_ref[...] = (kt_ref[...] * kw3_ref[...]).astype(BF16)

    ones = jnp.ones((L, LANES), BF16)
    for g, h in heads:
        vh = v_ref[g * L:(g + 1) * L, h * dv:(h + 1) * dv]
        num_ref[g * H + h] = _dot(p_ref[g * H + h], vh)
        kv_ref[g * H + h] = _dot(ktw_ref[g * H + h], jnp.concatenate([vh, ones], axis=1))

    rep = dv // LANES
    wide = lambda s: jnp.concatenate([s] * rep, axis=1)
    for g, h in heads:
        i = g * H + h
        r = g * 2 * H + H + h
        rows = slice(g * L, (g + 1) * L)
        c_st = c_ref[h]
        m_st = m_ref[h:h + 1, :]
        qc = _dot(q_ref[rows, h * dqk:(h + 1) * dqk], c_st.astype(BF16))
        m_intra = stm_ref[i]
        inter_log = stb_ref[i] + m_st
        m_out = jnp.maximum(inter_log, m_intra)
        s_inter = jnp.exp(inter_log - m_out)
        s_intra = jnp.exp(m_intra - m_out)
        num = wide(s_inter) * qc[:, 0:dv] + wide(s_intra) * num_ref[i]
        den = s_inter * qc[:, dv:dv + LANES] + s_intra * std_ref[i]
        hh = num / wide(jnp.maximum(jnp.abs(den), jnp.exp(-m_out)))
        gs = gs_ref[r:r + 1, :]
        mc = mc_ref[r:r + 1, :]
        m_new = jnp.maximum(gs + m_st, mc)
        dec = jnp.exp(gs + m_st - m_new)
        s_new = jnp.exp(mc - m_new)
        c_ref[h] = (jnp.concatenate([dec] * (rep + 1), axis=1) * c_st
                    + jnp.concatenate([s_new] * (rep + 1), axis=1) * kv_ref[i])
        m_ref[h:h + 1, :] = m_new

        rms = lax.rsqrt(jnp.mean(hh * hh, axis=1, keepdims=True) + RMS_EPS)
        og = _sigmoid(o_ref[rows, h * dv:(h + 1) * dv].astype(F32))
        out_ref[rows, h * dv:(h + 1) * dv] = (hh * rms * ng_ref[:, h * dv:(h + 1) * dv] * og).astype(BF16)


def _mlstm(proj, small_t, bias_r, norm_g, bsz, seq, cols, dqk, dv, n_ch):
    n = seq // (CHUNK * n_ch)
    L = CHUNK
    R = n_ch * L
    H = ML_HEADS
    qk_w, v_w = H * dqk, H * dv
    t = bsz * seq
    nh = n_ch * H
    kern = functools.partial(_mlstm_kernel, dqk=dqk, dv=dv, n_ch=n_ch)
    row = lambda b, c: b * n + c
    return pl.pallas_call(
        kern,
        grid=(bsz, n),
        in_specs=[
            pl.BlockSpec((R, qk_w), lambda b, c: (row(b, c), cols["mq"] // qk_w)),
            pl.BlockSpec((R, qk_w), lambda b, c: (row(b, c), cols["mk"] // qk_w)),
            pl.BlockSpec((R, v_w), lambda b, c: (row(b, c), cols["mv"] // v_w)),
            pl.BlockSpec((R, v_w), lambda b, c: (row(b, c), cols["mo"] // v_w)),
            pl.BlockSpec((n_ch, small_t.shape[1], L), lambda b, c: (row(b, c), 0, 0)),
            pl.BlockSpec(bias_r.shape, lambda b, c: (0, 0)),
            pl.BlockSpec((1, v_w), lambda b, c: (0, 0)),
        ],
        out_specs=pl.BlockSpec((R, v_w), lambda b, c: (row(b, c), 0)),
        out_shape=jax.ShapeDtypeStruct((t, v_w), BF16),
        scratch_shapes=[
            pltpu.VMEM((H, dqk, dv + LANES), F32),
            pltpu.VMEM((SUBLANES, LANES), F32),
            pltpu.VMEM((nh, L, L), F32),
            pltpu.VMEM((nh, dqk, L), F32),
            pltpu.VMEM((nh, L, L), BF16),
            pltpu.VMEM((nh, dqk, L), BF16),
            pltpu.VMEM((nh, 1, L), F32),
            pltpu.VMEM((nh, 1, L), F32),
            pltpu.VMEM((nh, 1, L), F32),
            pltpu.VMEM((n_ch * 2 * H, LANES), F32),
            pltpu.VMEM((n_ch * 2 * H, LANES), F32),
            pltpu.VMEM((nh, L, LANES), F32),
            pltpu.VMEM((nh, L, LANES), F32),
            pltpu.VMEM((nh, L, LANES), F32),
            pltpu.VMEM((nh, L, dv), F32),
            pltpu.VMEM((nh, dqk, dv + LANES), F32),
        ],
        compiler_params=_params("arbitrary", "arbitrary"),
        name="mlstm",
    )(proj, proj, proj, proj, small_t, bias_r, norm_g)


def _gdn_kernel(x_ref, z_ref, gr_ref, cw_ref, al_ref, dt_ref, ng_ref,
                out_ref, s_ref, xs_ref, c_ref, qn_ref, kn_ref, knt_ref, kk2_ref, qk2_ref, pw2_ref, x2_ref, rhs2_ref,
                attn2_ref, qd_ref, kdt_ref, gam2_ref, beta2_ref, kdwb_ref, gt_ref, sol_ref, vn_ref,
                *, n_qk, n_v, ga_off, gb_off, n_ch):
    L = CHUNK
    R = n_ch * L
    dk, dv = GDN_DK, GDN_DV
    qk_w = n_qk * dk
    rep = n_v // n_qk
    assert rep == 2 and 2 * L == LANES and dk == LANES and dv == LANES
    conv_ch = 2 * qk_w + n_v * dv
    sw = dv + dk

    @pl.when(pl.program_id(1) == 0)
    def _():
        s_ref[...] = jnp.zeros_like(s_ref)
        xs_ref[0:SUBLANES, :] = jnp.zeros((SUBLANES, xs_ref.shape[1]), F32)
        rhs2_ref[...] = jnp.zeros_like(rhs2_ref)

    T0 = SUBLANES
    xs_ref[T0:T0 + R, :] = x_ref[...].astype(F32)
    for cb in range(0, conv_ch, GDN_CONV_COLS):
        cs = slice(cb, cb + GDN_CONV_COLS)
        xa = xs_ref[0:T0 + R, cs]
        conv = cw_ref[CONV_K - 1:CONV_K, cs] * xa[T0:T0 + R]
        for sh in range(1, CONV_K):
            conv = conv + cw_ref[CONV_K - 1 - sh:CONV_K - sh, cs] * pltpu.roll(xa, sh, axis=0)[T0:T0 + R]
        c_ref[:, cs] = conv * _sigmoid(conv)
    xs_ref[0:T0, :] = xs_ref[R:R + T0, :]

    ii = lax.broadcasted_iota(jnp.int32, (L, L), 0)
    jj = lax.broadcasted_iota(jnp.int32, (L, L), 1)
    upper = (ii <= jj).astype(BF16)
    row2 = lax.broadcasted_iota(jnp.int32, (L, LANES), 0)
    lane2 = lax.broadcasted_iota(jnp.int32, (L, LANES), 1)
    left = lane2 < L
    col2 = jnp.where(left, lane2, lane2 - L)
    incl2 = col2 <= row2
    strict2 = col2 < row2
    diag2 = col2 == row2
    eye2 = diag2.astype(F32)

    for g in range(n_ch):
        rows = slice(g * L, (g + 1) * L)
        for hk in range(n_qk):
            iq = g * n_qk + hk
            cq = c_ref[rows, hk * dk:(hk + 1) * dk]
            ck = c_ref[rows, qk_w + hk * dk:qk_w + (hk + 1) * dk]
            qn = cq * lax.rsqrt(jnp.sum(cq * cq, axis=1, keepdims=True) + RMS_EPS) * (dk ** -0.5)
            kn = ck * lax.rsqrt(jnp.sum(ck * ck, axis=1, keepdims=True) + RMS_EPS)
            qn_ref[iq] = qn
            kn_ref[iq] = kn
            knt_ref[iq] = kn.T
            kb = kn.astype(BF16)
            kb2 = jnp.concatenate([kb, kb], axis=0)
            kk2_ref[iq] = _dot_nt(kb, kb2)
            qk2_ref[iq] = _dot_nt(qn.astype(BF16), kb2)

    for g in range(n_ch):
        gd = -jnp.exp(al_ref[...]) * _softplus(gr_ref[g, ga_off:ga_off + n_v, :] + dt_ref[...])
        gam = _cumsum_lanes(gd, upper)
        beta = _sigmoid(gr_ref[g, gb_off:gb_off + n_v, :])
        g_tot = gam[:, L - 1:L]
        kdwb = jnp.exp(g_tot - gam) * beta
        gt_ref[g * n_v:(g + 1) * n_v, :] = jnp.broadcast_to(jnp.exp(g_tot), (n_v, LANES))
        for hk in range(n_qk):
            iq = g * n_qk + hk
            gam2_ref[iq] = jnp.concatenate([gam[2 * hk:2 * hk + 1, :], gam[2 * hk + 1:2 * hk + 2, :]], axis=1)
            beta2_ref[iq] = jnp.concatenate([beta[2 * hk:2 * hk + 1, :], beta[2 * hk + 1:2 * hk + 2, :]], axis=1)
            for r in range(rep):
                kdwb_ref[iq, r] = kdwb[2 * hk + r:2 * hk + r + 1, :]

    for g in range(n_ch):
        sq = slice(g * n_qk, (g + 1) * n_qk)
        rows = slice(g * L, (g + 1) * L)
        gam_r = gam2_ref[sq]
        picked = jnp.where(diag2, gam_r, 0.0)
        gam_c0 = jnp.sum(jnp.where(left, picked, 0.0), axis=2, keepdims=True)
        gam_c1 = jnp.sum(jnp.where(left, 0.0, picked), axis=2, keepdims=True)
        gam_c = jnp.where(left, gam_c0, gam_c1)
        decm = jnp.exp(jnp.where(incl2, gam_c - gam_r, -jnp.inf))
        db = decm * beta2_ref[sq]
        a = jnp.where(strict2, kk2_ref[sq] * db, 0.0)
        x2_ref[sq] = eye2 - a
        pw2_ref[sq, 0:L, :] = jnp.where(left, a, 0.0).astype(BF16)
        pw2_ref[sq, L:2 * L, :] = jnp.where(left, 0.0, a).astype(BF16)
        attn2_ref[sq] = (qk2_ref[sq] * db).astype(BF16)
        kn = kn_ref[sq]
        qn = qn_ref[sq]
        for r, gc in enumerate((gam_c0, gam_c1)):
            eg = jnp.exp(gc)
            rhs2_ref[sq, r * L:(r + 1) * L, r * sw + dv:(r + 1) * sw] = (kn * eg).astype(BF16)
            qd_ref[sq, r] = (qn * eg).astype(BF16)
            kdt_ref[sq, r] = (knt_ref[sq] * kdwb_ref[sq, r]).astype(BF16)
        for hk in range(n_qk):
            for r in range(rep):
                h = 2 * hk + r
                rhs2_ref[g * n_qk + hk, r * L:(r + 1) * L, r * sw:r * sw + dv] = (
                    c_ref[rows, 2 * qk_w + h * dv:2 * qk_w + (h + 1) * dv].astype(BF16))

    n_sq = L.bit_length() - 2
    for lvl in range(n_sq + 1):
        for i in range(n_ch * n_qk):
            pwb = pw2_ref[i]
            if lvl > 0:
                xv = x2_ref[i]
                x2_ref[i] = xv + _dot(xv.astype(BF16), pwb)
            if lvl < n_sq:
                pw2_ref[i] = _dot(pwb, pwb).astype(BF16)

    for i in range(n_ch * n_qk):
        sol_ref[i] = _dot(x2_ref[i].astype(BF16), rhs2_ref[i])

    zeros = jnp.zeros((L, dv), BF16)
    for g in range(n_ch):
        rows = slice(g * L, (g + 1) * L)
        for h in range(n_v):
            sol = sol_ref[g * n_qk + h // rep][:, (h % rep) * sw:(h % rep + 1) * sw]
            vn_ref[h] = (sol[:, 0:dv] - _dot(sol[:, dv:sw].astype(BF16), s_ref[h].astype(BF16))).astype(BF16)
        for hk in range(n_qk):
            iq = g * n_qk + hk
            h0, h1 = 2 * hk, 2 * hk + 1
            v0, v1 = vn_ref[h0], vn_ref[h1]
            vbd = jnp.concatenate([jnp.concatenate([v0, zeros], axis=1), jnp.concatenate([zeros, v1], axis=1)], axis=0)
            s0, s1 = s_ref[h0], s_ref[h1]
            o2 = _dot(attn2_ref[iq], vbd) + jnp.concatenate(
                [_dot(qd_ref[iq, 0], s0.astype(BF16)), _dot(qd_ref[iq, 1], s1.astype(BF16))], axis=1)
            s_ref[h0] = gt_ref[g * n_v + h0:g * n_v + h0 + 1, :] * s0 + _dot(kdt_ref[iq, 0], v0)
            s_ref[h1] = gt_ref[g * n_v + h1:g * n_v + h1 + 1, :] * s1 + _dot(kdt_ref[iq, 1], v1)
            for r, h in enumerate((h0, h1)):
                o = o2[:, r * dv:(r + 1) * dv]
                rms = lax.rsqrt(jnp.mean(o * o, axis=1, keepdims=True) + RMS_EPS)
                zz = z_ref[rows, h * dv:(h + 1) * dv].astype(F32)
                out_ref[rows, h * dv:(h + 1) * dv] = (o * rms * ng_ref[...] * (zz * _sigmoid(zz))).astype(BF16)


def _gdn(proj, small_t, conv_w, al, dt, norm_g, bsz, seq, cols, n_qk, n_v, ga_off, gb_off, n_ch):
    n = seq // (CHUNK * n_ch)
    L = CHUNK
    R = n_ch * L
    conv_ch = 2 * n_qk * GDN_DK + n_v * GDN_DV
    v_w = n_v * GDN_DV
    t = bsz * seq
    nq, nv = n_ch * n_qk, n_ch * n_v
    kern = functools.partial(_gdn_kernel, n_qk=n_qk, n_v=n_v, ga_off=ga_off, gb_off=gb_off, n_ch=n_ch)
    row = lambda b, c: b * n + c
    full = lambda a: pl.BlockSpec(a.shape, lambda b, c: (0,) * a.ndim)
    return pl.pallas_call(
        kern,
        grid=(bsz, n),
        in_specs=[
            pl.BlockSpec((R, conv_ch), lambda b, c: (row(b, c), cols["gqkv"] // conv_ch)),
            pl.BlockSpec((R, v_w), lambda b, c: (row(b, c), cols["gz"] // v_w)),
            pl.BlockSpec((n_ch, small_t.shape[1], L), lambda b, c: (row(b, c), 0, 0)),
            full(conv_w), full(al), full(dt), full(norm_g),
        ],
        out_specs=pl.BlockSpec((R, v_w), lambda b, c: (row(b, c), 0)),
        out_shape=jax.ShapeDtypeStruct((t, v_w), BF16),
        scratch_shapes=[
            pltpu.VMEM((n_v, GDN_DK, GDN_DV), F32),
            pltpu.VMEM((SUBLANES + R, conv_ch), F32),
            pltpu.VMEM((R, conv_ch), F32),
            pltpu.VMEM((nq, L, GDN_DK), F32),
            pltpu.VMEM((nq, L, GDN_DK), F32),
            pltpu.VMEM((nq, GDN_DK, L), F32),
            pltpu.VMEM((nq, L, 2 * L), F32),
            pltpu.VMEM((nq, L, 2 * L), F32),
            pltpu.VMEM((nq, 2 * L, 2 * L), BF16),
            pltpu.VMEM((nq, L, 2 * L), F32),
            pltpu.VMEM((nq, 2 * L, 2 * (GDN_DV + GDN_DK)), BF16),
            pltpu.VMEM((nq, L, 2 * L), BF16),
            pltpu.VMEM((nq, 2, L, GDN_DK), BF16),
            pltpu.VMEM((nq, 2, GDN_DK, L), BF16),
            pltpu.VMEM((nq, 1, 2 * L), F32),
            pltpu.VMEM((nq, 1, 2 * L), F32),
            pltpu.VMEM((nq, 2, 1, L), F32),
            pltpu.VMEM((nv, LANES), F32),
            pltpu.VMEM((nq, L, 2 * (GDN_DV + GDN_DK)), F32),
            pltpu.VMEM((n_v, L, GDN_DV), BF16),
        ],
        compiler_params=_params("arbitrary", "arbitrary"),
        name="gdn",
    )(proj, proj, small_t, conv_w, al, dt, norm_g)


def _kv_kernel(m_ref, wk_ref, wv_ref, k_ref, v_ref):
    mb = m_ref[...].astype(BF16)
    k_ref[...] = _dot(mb, wk_ref[...]).astype(BF16)
    v_ref[...] = _dot(mb, wv_ref[...]).astype(BF16)


def _kv_proj(mem2, wk, wv, tm):
    t, d = mem2.shape
    full = lambda a: pl.BlockSpec(a.shape, lambda i: (0,) * a.ndim)
    return pl.pallas_call(
        _kv_kernel,
        grid=(t // tm,),
        in_specs=[pl.BlockSpec((tm, d), lambda i: (i, 0)), full(wk), full(wv)],
        out_specs=[pl.BlockSpec((tm, d), lambda i: (i, 0))] * 2,
        out_shape=[jax.ShapeDtypeStruct((t, d), BF16)] * 2,
        compiler_params=_params("arbitrary"),
        name="mem_kv_proj",
    )(mem2, wk, wv)


def _mix_xattn_kernel(hm_ref, og_ref, gm_ref, gg_ref, h0_ref, wbm_ref, wbg_ref, wmo_ref, g1_ref, b1_ref,
                      k_ref, v_ref, wq_ref, wo_ref, g_ref, b_ref, wr_ref, br_ref,
                      h2_ref, gw_ref, route_ref, cnt_ref, carry_ref, *, dh):
    tm = h0_ref.shape[0]

    @pl.when((pl.program_id(0) == 0) & (pl.program_id(1) == 0))
    def _():
        carry_ref[...] = jnp.zeros_like(carry_ref)

    n_sub = MIX_SPLIT
    ts = tm // n_sub
    subs = [slice(u * ts, (u + 1) * ts) for u in range(n_sub)]

    mix = []
    for r in subs:
        y_ml = _dot(hm_ref[r, :], wbm_ref[...])
        y_gdn = _dot(og_ref[r, :], wbg_ref[...])
        merged = _sigmoid(gm_ref[r, :].astype(F32)) * y_ml + _sigmoid(gg_ref[r, :].astype(F32)) * y_gdn
        mix.append(_dot(merged.astype(BF16), wmo_ref[...]))
    h1 = [_layer_norm(DN_ALPHA * h0_ref[r, :] + m, g1_ref[...], b1_ref[...]) for r, m in zip(subs, mix)]
    q = [_dot(x.astype(BF16), wq_ref[...]) for x in h1]
    o = []
    for qq in q:
        outs = []
        for hd in range(XA_HEADS):
            qh = qq[:, hd * dh:(hd + 1) * dh].astype(BF16)
            kh = k_ref[:, hd * dh:(hd + 1) * dh]
            vh = v_ref[:, hd * dh:(hd + 1) * dh]
            sc = _dot_nt(qh, kh) * (dh ** -0.5)
            e = jnp.exp(sc - jnp.max(sc, axis=1, keepdims=True))
            p = e / jnp.sum(e, axis=1, keepdims=True)
            outs.append(_dot(p.astype(BF16), vh))
        o.append(jnp.concatenate(outs, axis=1))
    xa = [_dot(x.astype(BF16), wo_ref[...]) for x in o]
    h2 = [_layer_norm(DN_ALPHA * x + y, g_ref[...], b_ref[...]) for x, y in zip(h1, xa)]
    for r, x in zip(subs, h2):
        h2_ref[r, :] = x

    lane = lax.broadcasted_iota(jnp.int32, (ts, LANES), 1)
    lane_f = lane.astype(F32)
    ri = lax.broadcasted_iota(jnp.int32, (ts, ts), 0)
    ci = lax.broadcasted_iota(jnp.int32, (ts, ts), 1)
    tri = (ci < ri).astype(BF16)
    logits = [_dot(x.astype(BF16), wr_ref[...]) + br_ref[...] for x in h2]
    picks = []
    for work in logits:
        vals, idxs = [], []
        for _ in range(TOP_K):
            m = jnp.max(work, axis=1, keepdims=True)
            idx = jnp.min(jnp.where(work == m, lane_f, float(LANES)), axis=1, keepdims=True)
            vals.append(m)
            idxs.append(idx)
            work = jnp.where(lane_f == idx, -jnp.inf, work)
        picks.append((vals, idxs))
    carry = carry_ref[0:1, :]
    for r, (vals, idxs) in zip(subs, picks):
        es = [jnp.exp(v - vals[0]) for v in vals]
        tot = es[0]
        for e_ in es[1:]:
            tot = tot + e_
        onehot = jnp.zeros((ts, LANES), F32)
        for idx in idxs:
            onehot = onehot + (lane_f == idx).astype(F32)
        ranks = carry + _dot(tri, onehot.astype(BF16))
        gw = jnp.zeros((ts, LANES), F32)
        route = jnp.zeros((ts, LANES), F32)
        for k_ in range(TOP_K):
            rk = jnp.sum(jnp.where(lane_f == idxs[k_], ranks, 0.0), axis=1, keepdims=True)
            gw = gw + jnp.where(lane == k_, es[k_] / tot, 0.0)
            route = route + jnp.where(lane == k_, idxs[k_], 0.0) + jnp.where(lane == TOP_K + k_, rk, 0.0)
        gw_ref[r, :] = gw
        route_ref[r, :] = route.astype(jnp.int32)
        carry = carry + jnp.sum(onehot, axis=0, keepdims=True)
    carry_ref[...] = jnp.broadcast_to(carry, carry_ref.shape)
    cnt_ref[...] = jnp.broadcast_to(carry, cnt_ref.shape)


def _mix_xattn(hm, og, proj, h0, w_bm, w_bg, w_mo, g1, b1, kmem, vmem, wq, wo, g, b, w_r, b_r, cols,
               bsz, seq, mem_len, tm):
    t, d = h0.shape
    nt = seq // tm
    once = lambda a: pl.BlockSpec(a.shape, lambda i, j: (0,) * a.ndim, pipeline_mode=pl.Buffered(1))
    rowmap = lambda i, j: (i * nt + j, 0)
    kern = functools.partial(_mix_xattn_kernel, dh=d // XA_HEADS)
    return pl.pallas_call(
        kern,
        grid=(bsz, nt),
        in_specs=[
            pl.BlockSpec((tm, hm.shape[1]), rowmap),
            pl.BlockSpec((tm, og.shape[1]), rowmap),
            pl.BlockSpec((tm, d), lambda i, j: (i * nt + j, cols["gate_ml"] // d)),
            pl.BlockSpec((tm, d), lambda i, j: (i * nt + j, cols["gate_gdn"] // d)),
            pl.BlockSpec((tm, d), rowmap),
            once(w_bm), once(w_bg), once(w_mo), once(g1), once(b1),
            pl.BlockSpec((mem_len, d), lambda i, j: (i, 0)),
            pl.BlockSpec((mem_len, d), lambda i, j: (i, 0)),
            once(wq), once(wo), once(g), once(b), once(w_r), once(b_r),
        ],
        out_specs=[
            pl.BlockSpec((tm, d), rowmap),
            pl.BlockSpec((tm, LANES), rowmap),
            pl.BlockSpec((tm, LANES), rowmap),
            pl.BlockSpec((SUBLANES, LANES), lambda i, j: (0, 0)),
        ],
        out_shape=[
            jax.ShapeDtypeStruct((t, d), F32),
            jax.ShapeDtypeStruct((t, LANES), F32),
            jax.ShapeDtypeStruct((t, LANES), jnp.int32),
            jax.ShapeDtypeStruct((SUBLANES, LANES), F32),
        ],
        scratch_shapes=[pltpu.VMEM((SUBLANES, LANES), F32)],
        compiler_params=_params("arbitrary", "arbitrary"),
        name="mix_xattn_ln_router",
    )(hm, og, proj, proj, h0, w_bm, w_bg, w_mo, g1, b1, kmem, vmem, wq, wo, g, b, w_r, b_r)


def _dispatch_kernel(pad_lo_ref, pad_n_ref, na_ref, dest_ref, h_ref, xs_ref, zero_ref, sem, zsem):
    tm = h_ref.shape[0]
    step = pl.program_id(0)
    n_blocks = xs_ref.shape[0] // MOE_BLOCK

    @pl.when(step == 0)
    def _():
        zero_ref[...] = jnp.zeros_like(zero_ref)

        def per_expert(e, carry):
            n = pad_n_ref[e]
            pos = pad_lo_ref[e] + n

            def pieces(wait):
                def piece(start, p):
                    cp = pltpu.make_async_copy(zero_ref.at[pl.ds(0, p), :], xs_ref.at[pl.ds(start, p), :], zsem)
                    cp.wait() if wait else cp.start()

                at = pos
                p = MOE_BLOCK // 2
                while p >= SUBLANES:

                    @pl.when((n & p) != 0)
                    def _(p=p, at=at):
                        piece(pl.multiple_of(at - p, SUBLANES), p)

                    at = at - (n & p)
                    p //= 2
                for r in range(SUBLANES - 1):

                    @pl.when(r < (n & (SUBLANES - 1)))
                    def _(r=r, at=at):
                        piece(at - 1 - r, 1)

            pieces(wait=False)
            pieces(wait=True)
            return carry

        lax.fori_loop(0, N_EXPERTS, per_expert, 0)

        def tail(b, carry):
            half = MOE_BLOCK // 2
            cp = pltpu.make_async_copy(zero_ref, xs_ref.at[pl.ds(b * half, half), :], zsem)
            cp.start()
            cp.wait()
            return carry

        lax.fori_loop(2 * na_ref[0], 2 * n_blocks, tail, 0)

    for tok in range(tm):
        src = h_ref.at[pl.ds(tok, 1), :]
        for k_ in range(TOP_K):
            pltpu.make_async_copy(src, xs_ref.at[pl.ds(dest_ref[tok * TOP_K + k_], 1), :], sem).start(priority=k_ % 2)
    for _ in range(TOP_K):
        pltpu.make_async_copy(h_ref, xs_ref.at[pl.ds(0, tm), :], sem).wait()


def _dispatch(pad_lo, pad_n, nact, dest_flat, h2, n_rows, tm):
    t, d = h2.shape
    grid_spec = pltpu.PrefetchScalarGridSpec(
        num_scalar_prefetch=3,
        grid=(t // tm,),
        in_specs=[
            pl.BlockSpec((tm * TOP_K,), lambda i, *_: (i,), memory_space=pltpu.SMEM),
            pl.BlockSpec((tm, d), lambda i, *_: (i, 0)),
        ],
        out_specs=pl.BlockSpec(memory_space=pl.ANY),
        scratch_shapes=[pltpu.VMEM((MOE_BLOCK // 2, d), F32), pltpu.SemaphoreType.DMA(()),
                        pltpu.SemaphoreType.DMA(())],
    )
    return pl.pallas_call(
        _dispatch_kernel,
        grid_spec=grid_spec,
        out_shape=jax.ShapeDtypeStruct((n_rows, d), F32),
        compiler_params=_params("arbitrary"),
        name="moe_dispatch",
    )(pad_lo, pad_n, nact, dest_flat, h2)


def _expert_kernel(be_ref, na_ref, tok_ref, tokn_ref, h_ref, wgu_ref, bgu_ref, wdn_ref, bdn_ref, y_ref,
                   wgu_s, wdn_s, xbuf, gsem, *, d_exp):
    blk = pl.program_id(0)
    na = na_ref[0]
    active = blk < na
    slot = blk % 2
    m = y_ref.shape[0]

    def issue(idx_ref, s):
        for r in range(m):
            pltpu.make_async_copy(h_ref.at[pl.ds(idx_ref[r], 1), :], xbuf.at[s, pl.ds(r, 1), :],
                                  gsem.at[s]).start(priority=r % 2)

    @pl.when(blk == 0)
    def _():
        issue(tok_ref, 0)

    @pl.when(jnp.logical_not(active))
    def _():
        y_ref[...] = jnp.zeros_like(y_ref)

    @pl.when(active & ((blk == 0) | (be_ref[blk] != be_ref[jnp.maximum(blk - 1, 0)])))
    def _():
        wgu_s[...] = wgu_ref[0].astype(BF16)
        wdn_s[...] = wdn_ref[0].astype(BF16)

    @pl.when(active)
    def _():
        pltpu.make_async_copy(h_ref.at[pl.ds(0, m), :], xbuf.at[slot], gsem.at[slot]).wait()

    @pl.when(blk + 1 < na)
    def _():
        issue(tokn_ref, 1 - slot)

    @pl.when(active)
    def _():
        gu = _dot(xbuf[slot].astype(BF16), wgu_s[...]) + bgu_ref[0]
        gate = jnp.minimum(gu[:, :d_exp], SWIGLU_LIMIT)
        up = jnp.clip(gu[:, d_exp:], -SWIGLU_LIMIT, SWIGLU_LIMIT)
        act = (up + 1.0) * (gate * _sigmoid(SWIGLU_ALPHA * gate))
        y_ref[...] = _dot(act.astype(BF16), wdn_s[...]) + bdn_ref[0]


def _experts(block_e, nact, row_tok, h2, w_gu, b_gu, w_dn, b_dn):
    n_rows = row_tok.shape[0]
    d = h2.shape[1]
    n_blocks = n_rows // MOE_BLOCK
    d_exp = w_dn.shape[1]
    kern = functools.partial(_expert_kernel, d_exp=d_exp)
    emap = lambda i, be, na: (be[i], 0, 0)
    grid_spec = pltpu.PrefetchScalarGridSpec(
        num_scalar_prefetch=2,
        grid=(n_blocks,),
        in_specs=[
            pl.BlockSpec((MOE_BLOCK,), lambda i, be, na: (i,), memory_space=pltpu.SMEM),
            pl.BlockSpec((MOE_BLOCK,), lambda i, be, na: (jnp.minimum(i + 1, n_blocks - 1),), memory_space=pltpu.SMEM),
            pl.BlockSpec(memory_space=pl.ANY),
            pl.BlockSpec((1, d, 2 * d_exp), emap),
            pl.BlockSpec((1, 1, 2 * d_exp), emap),
            pl.BlockSpec((1, d_exp, d), emap),
            pl.BlockSpec((1, 1, d), emap),
        ],
        out_specs=pl.BlockSpec((MOE_BLOCK, d), lambda i, be, na: (i, 0)),
        scratch_shapes=[pltpu.VMEM((d, 2 * d_exp), BF16), pltpu.VMEM((d_exp, d), BF16),
                        pltpu.VMEM((2, MOE_BLOCK, d), F32), pltpu.SemaphoreType.DMA((2,))],
    )
    return pl.pallas_call(
        kern,
        grid_spec=grid_spec,
        out_shape=jax.ShapeDtypeStruct((n_rows, d), F32),
        compiler_params=_params("arbitrary"),
        name="moe_experts",
    )(block_e, nact, row_tok, row_tok, h2, w_gu, b_gu, w_dn, b_dn)


def _combine_kernel(dest_ref, dest_nxt_ref, gw_ref, h_ref, g_ref, b_ref, y_ref, out_ref, buf_a, buf_b, sem):
    tm = buf_a.shape[1]
    step = pl.program_id(0)
    n_steps = pl.num_programs(0)

    def issue(idx_ref, half, buf, s):
        for tok in range(tm):
            for k_ in range(TOP_K):
                pltpu.make_async_copy(y_ref.at[pl.ds(idx_ref[(half * tm + tok) * TOP_K + k_], 1), :],
                                      buf.at[k_, pl.ds(tok, 1), :], sem.at[s]).start(priority=k_ % 2)

    def consume(half, buf, s):
        for k_ in range(TOP_K):
            pltpu.make_async_copy(y_ref.at[pl.ds(0, tm), :], buf.at[k_], sem.at[s]).wait()
        rows = slice(half * tm, (half + 1) * tm)
        gw = gw_ref[rows, :]
        ff = buf[0] * gw[:, 0:1]
        for k_ in range(1, TOP_K):
            ff = ff + buf[k_] * gw[:, k_:k_ + 1]
        out_ref[rows, :] = _layer_norm(DN_ALPHA * h_ref[rows, :] + ff, g_ref[...], b_ref[...])

    @pl.when(step == 0)
    def _():
        issue(dest_ref, 0, buf_a, 0)

    issue(dest_ref, 1, buf_b, 1)
    consume(0, buf_a, 0)

    @pl.when(step + 1 < n_steps)
    def _():
        issue(dest_nxt_ref, 0, buf_a, 0)

    consume(1, buf_b, 1)


def _combine(dest_flat, gw, h2, g, b, yb, tm):
    t, d = h2.shape
    n = t // (2 * tm)
    return pl.pallas_call(
        _combine_kernel,
        grid=(n,),
        in_specs=[
            pl.BlockSpec((2 * tm * TOP_K,), lambda i: (i,), memory_space=pltpu.SMEM),
            pl.BlockSpec((2 * tm * TOP_K,), lambda i: (jnp.minimum(i + 1, n - 1),), memory_space=pltpu.SMEM),
            pl.BlockSpec((2 * tm, LANES), lambda i: (i, 0)),
            pl.BlockSpec((2 * tm, d), lambda i: (i, 0)),
            pl.BlockSpec((1, d), lambda i: (0, 0)),
            pl.BlockSpec((1, d), lambda i: (0, 0)),
            pl.BlockSpec(memory_space=pl.ANY),
        ],
        out_specs=pl.BlockSpec((2 * tm, d), lambda i: (i, 0)),
        out_shape=jax.ShapeDtypeStruct((t, d), F32),
        scratch_shapes=[pltpu.VMEM((TOP_K, tm, d), F32), pltpu.VMEM((TOP_K, tm, d), F32),
                        pltpu.SemaphoreType.DMA((2,))],
        compiler_params=_params("arbitrary"),
        name="moe_combine_ln3",
    )(dest_flat, dest_flat, gw, h2, g, b, yb)


def _pick(n, pref):
    return pref if n % pref == 0 else n


def kernel(x, mem, ln_in_g, ln_in_b, w_in, ml_gate_bias, ml_norm_g, gdn_conv_w, gdn_a_log, gdn_dt_bias, gdn_norm_g, w_branch_ml, w_branch_gdn, w_mix_out, ln1_g, ln1_b, xa_wq, xa_wk, xa_wv, xa_wo, ln2_g, ln2_b, w_router, b_router, w_gu, b_gu, w_dn, b_dn, ln3_g, ln3_b):
    bsz, seq, d = x.shape
    mem_len = mem.shape[1]
    t = bsz * seq
    ml_dv = d // ML_HEADS
    ml_dqk = ml_dv // 2
    ml_qk_w, ml_v_w = ML_HEADS * ml_dqk, ML_HEADS * ml_dv
    n_qk = d // GDN_DK
    n_v = 2 * n_qk
    gdn_qk_w, gdn_v_w = n_qk * GDN_DK, n_v * GDN_DV
    conv_ch = 2 * gdn_qk_w + gdn_v_w
    splits = (ml_qk_w, ml_qk_w, ml_v_w, ml_v_w, 2 * ML_HEADS, conv_ch, gdn_v_w, n_v, n_v, d, d)
    names = ("mq", "mk", "mv", "mo", "mif", "gqkv", "gz", "ga", "gb", "gate_ml", "gate_gdn")
    starts = {}
    acc = 0
    for nm, sz in zip(names, splits):
        starts[nm] = (acc, sz)
        acc += sz
    row2 = lambda a: a.reshape(1, -1).astype(F32)

    h = x.reshape(t, d)
    for l in range(DEPTH):
        w = w_in[l]
        seg = lambda nm: w[:, starts[nm][0]:starts[nm][0] + starts[nm][1]]
        order = ("gqkv", "gz", "mv", "mo", "gate_ml", "gate_gdn", "mq", "mk")
        cols = {}
        off = 0
        for nm in order:
            assert off % starts[nm][1] == 0
            cols[nm] = off
            off += starts[nm][1]
        w_big = jnp.concatenate([seg(nm) for nm in order], axis=1).astype(BF16)
        n_small = 2 * ML_HEADS + 2 * n_v
        w_small = jnp.concatenate([seg("mif"), seg("ga"), seg("gb"),
                                   jnp.zeros((d, LANES - n_small), F32)], axis=1).astype(BF16)
        ga_off, gb_off = 2 * ML_HEADS, 2 * ML_HEADS + n_v

        if l == 0:
            h0, proj, small_t = _ln_proj(h, row2(ln_in_g), row2(ln_in_b), w_big, w_small,
                                         _pick(t, PROJ_ROWS), _pick(off, PROJ_COLS))
        else:
            raise NotImplementedError("DEPTH > 1")

        bias = ml_gate_bias[l].astype(F32)
        hm = _mlstm(proj, small_t, bias.reshape(-1, 1), row2(ml_norm_g[l]),
                    bsz, seq, cols, ml_dqk, ml_dv, ML_CHUNKS_PER_STEP)
        al = gdn_a_log[l].astype(F32)
        dt = gdn_dt_bias[l].astype(F32)
        og = _gdn(proj, small_t, gdn_conv_w[l].astype(F32), al.reshape(-1, 1), dt.reshape(-1, 1),
                  row2(gdn_norm_g[l]), bsz, seq, cols, n_qk, n_v, ga_off, gb_off, GDN_CHUNKS_PER_STEP)
        kmem, vmem = _kv_proj(mem.reshape(bsz * mem_len, d), xa_wk[l].astype(BF16), xa_wv[l].astype(BF16),
                              _pick(bsz * mem_len, KV_ROWS))
        w_r = jnp.concatenate([w_router[l], jnp.zeros((d, LANES - N_EXPERTS), F32)], axis=1).astype(BF16)
        b_r = jnp.concatenate([b_router[l].astype(F32), jnp.full((LANES - N_EXPERTS,), NEG_BIG, F32)]).reshape(1, -1)
        h2, gw, route, cnt = _mix_xattn(hm, og, proj, h0, w_branch_ml[l].astype(BF16), w_branch_gdn[l].astype(BF16),
                                        w_mix_out[l].astype(BF16), row2(ln1_g[l]), row2(ln1_b[l]), kmem, vmem,
                                        xa_wq[l].astype(BF16), xa_wo[l].astype(BF16), row2(ln2_g[l]), row2(ln2_b[l]),
                                        w_r, b_r, cols, bsz, seq, mem_len, _pick(seq, MIX_ROWS))

        counts = cnt[0, :N_EXPERTS].astype(jnp.int32)
        padded = (counts + MOE_BLOCK - 1) // MOE_BLOCK * MOE_BLOCK
        pad_end = jnp.cumsum(padded)
        pad_start = pad_end - padded
        n_asg = t * TOP_K
        n_blocks = -(-n_asg // MOE_BLOCK) + N_EXPERTS
        n_rows = n_blocks * MOE_BLOCK
        top_e = route[:, :TOP_K]
        rank = route[:, TOP_K:2 * TOP_K]
        sel = top_e[:, :, None] == jnp.arange(N_EXPERTS, dtype=jnp.int32)[None, None, :]
        dest = (jnp.sum(jnp.where(sel, pad_start[None, None, :], 0), axis=-1) + rank).astype(jnp.int32)
        dest_flat = dest.reshape(n_asg)
        blk_row = jnp.arange(n_blocks, dtype=jnp.int32) * MOE_BLOCK
        block_e = jnp.minimum(jnp.sum(pad_end[None, :] <= blk_row[:, None], axis=1), N_EXPERTS - 1).astype(jnp.int32)
        nact = (pad_end[-1:] // MOE_BLOCK).astype(jnp.int32)
        row_tok = jnp.zeros((n_rows,), jnp.int32).at[dest_flat].set(jnp.arange(n_asg, dtype=jnp.int32) // TOP_K)

        yb = _experts(block_e, nact, row_tok, h2, w_gu[l], b_gu[l].astype(F32)[:, None, :],
                      w_dn[l], b_dn[l].astype(F32)[:, None, :])
        h = _combine(dest_flat, gw, h2, row2(ln3_g[l]), row2(ln3_b[l]), yb, _pick(t // 2, COMBINE_ROWS))
    return h.reshape(bsz, seq, d)
```
